```python
import jax, jax.numpy as jnp
from jax import lax
import numpy as np

D_MODEL = 2048
BATCH = 4
SEQ = 2048
DEPTH = 2

GRID_W = 64
CTX_LEN = 256
N_MIXERS = 2
N_HEADS = 16
QK_NOPE = 128
QK_ROPE = 64
V_DIM = 128
Q_LORA = 512
KV_LORA = 512
ROPE_THETA = 10000.0
ATTN_SCALE = (QK_NOPE + QK_ROPE) ** -0.5
Q_BLOCK = 128
CONV_WIDTH = 31
D_FF = 5632
FFN_RES_WEIGHT = 0.5
N_MOD = 9
EPS = 1e-6
N_ATTN_LAYERS = (DEPTH + N_MIXERS - 1) // N_MIXERS
N_CONV_LAYERS = DEPTH // N_MIXERS

kernel_name = "hybrid_mla_conformer_dit_prefix"

F32 = jnp.float32


def rms_norm(x, g):
    xf = x.astype(F32)
    y = xf * lax.rsqrt(jnp.mean(xf * xf, axis=-1, keepdims=True) + EPS)
    return (y * g.astype(F32)).astype(x.dtype)


def layer_norm(x, g, b):
    xf = x.astype(F32)
    mu = jnp.mean(xf, axis=-1, keepdims=True)
    xc = xf - mu
    var = jnp.mean(xc * xc, axis=-1, keepdims=True)
    y = xc * lax.rsqrt(var + EPS)
    return (y * g.astype(F32) + b.astype(F32)).astype(x.dtype)


def ada_mod(cond, w, b):
    cond2 = cond.reshape(-1, cond.shape[-1])
    m = jax.nn.silu(cond2) @ w + b
    return jnp.split(m[:, None, :], N_MOD, axis=-1)


def norm_mod(h, g, shift, scale):
    return rms_norm(h, g) * (1 + scale) + shift


def swiglu(h, w1, w3, w2):
    return (jax.nn.silu(h @ w1) * (h @ w3)) @ w2


def axial_rope_tables(n_tokens):
    t = jnp.arange(n_tokens, dtype=jnp.int32)
    row = (t // GRID_W).astype(F32)
    col = (t % GRID_W).astype(F32)
    n_axis = QK_ROPE // 4
    freqs = ROPE_THETA ** (-jnp.arange(n_axis, dtype=F32) / n_axis)
    ang = jnp.concatenate([row[:, None] * freqs, col[:, None] * freqs], axis=-1)
    return jnp.cos(ang)[:, None, :], jnp.sin(ang)[:, None, :]


def apply_rope(x, cos, sin):
    xr = x.astype(F32).reshape(x.shape[:-1] + (QK_ROPE // 2, 2))
    x1, x2 = xr[..., 0], xr[..., 1]
    out = jnp.stack([x1 * cos - x2 * sin, x1 * sin + x2 * cos], axis=-1)
    return out.reshape(x.shape).astype(x.dtype)


def mla_qkv(h, wdq, gq, wuq, wdkv, gkv, wukv, rope):
    b, t, _ = h.shape
    cq = rms_norm(h @ wdq, gq)
    q = (cq @ wuq).reshape(b, t, N_HEADS, QK_NOPE + QK_ROPE)
    q_nope, q_pe = q[..., :QK_NOPE], q[..., QK_NOPE:]
    kv = h @ wdkv
    ckv = rms_norm(kv[..., :KV_LORA], gkv)
    k_pe = kv[..., KV_LORA:][:, :, None, :]
    if rope is not None:
        q_pe = apply_rope(q_pe, *rope)
        k_pe = apply_rope(k_pe, *rope)
    kvu = (ckv @ wukv).reshape(b, t, N_HEADS, QK_NOPE + V_DIM)
    k_nope, v = kvu[..., :QK_NOPE], kvu[..., QK_NOPE:]
    q = jnp.concatenate([q_nope, q_pe], axis=-1)
    k = jnp.concatenate([k_nope, jnp.broadcast_to(k_pe, (b, t, N_HEADS, QK_ROPE))], axis=-1)
    return q, k, v


def attend(q, k, v):
    s = jnp.einsum('bqhd,bkhd->bhqk', q, k).astype(F32) * ATTN_SCALE
    p = jax.nn.softmax(s, axis=-1).astype(v.dtype)
    return jnp.einsum('bhqk,bkhd->bqhd', p, v)


def block_attention(q, k, v):
    b, t, h, dq = q.shape
    nb = t // Q_BLOCK
    qb = q.reshape(b, nb, Q_BLOCK, h, dq).transpose(1, 0, 2, 3, 4)
    out = lax.map(lambda qi: attend(qi, k, v), qb)
    return out.transpose(1, 0, 2, 3, 4).reshape(b, t, h, V_DIM)


def conv_module(h, w1, b1, dw, dwb, ln_g, ln_b, w2, b2):
    d = h.shape[-1]
    a = h @ w1 + b1
    u = a[..., :d] * jax.nn.sigmoid(a[..., d:])
    pad = CONV_WIDTH // 2
    u = lax.conv_general_dilated(u, dw[:, None, :].astype(u.dtype), window_strides=(1,),
                                 padding=[(pad, pad)], dimension_numbers=('NWC', 'WIO', 'NWC'),
                                 feature_group_count=d) + dwb
    u = jax.nn.silu(layer_norm(u, ln_g, ln_b))
    return u @ w2 + b2


def setup_inputs(seed: int = 0) -> dict:
    key = jax.random.key(seed)
    ks = jax.random.split(key, 32)

    def nrm(k, shape, fan_in, scale=1.0):
        return jax.random.normal(k, shape, F32) * (scale * fan_in ** -0.5)

    D = D_MODEL
    NA, NC = N_ATTN_LAYERS, N_CONV_LAYERS
    return {
        "x": jax.random.normal(ks[0], (BATCH, SEQ, D), F32),
        "c": jax.random.normal(ks[1], (BATCH, D), F32),
        "ctx": jax.random.normal(ks[2], (BATCH, CTX_LEN, D), F32),
        "c_ctx": jax.random.normal(ks[3], (D,), F32),
        "ada_w": nrm(ks[4], (DEPTH, D, N_MOD * D), D, 0.5),
        "ada_b": 0.02 * jax.random.normal(ks[5], (DEPTH, N_MOD * D), F32),
        "norm_g": 1.0 + 0.02 * jax.random.normal(ks[6], (DEPTH, 3, D), F32),
        "ffn_w1": nrm(ks[7], (DEPTH, 2, D, D_FF), D),
        "ffn_w3": nrm(ks[8], (DEPTH, 2, D, D_FF), D),
        "ffn_w2": nrm(ks[9], (DEPTH, 2, D_FF, D), D_FF),
        "mla_wdq": nrm(ks[10], (NA, D, Q_LORA), D),
        "mla_gq": 1.0 + 0.02 * jax.random.normal(ks[11], (NA, Q_LORA), F32),
        "mla_wuq": nrm(ks[12], (NA, Q_LORA, N_HEADS * (QK_NOPE + QK_ROPE)), Q_LORA),
        "mla_wdkv": nrm(ks[13], (NA, D, KV_LORA + QK_ROPE), D),
        "mla_gkv": 1.0 + 0.02 * jax.random.normal(ks[14], (NA, KV_LORA), F32),
        "mla_wukv": nrm(ks[15], (NA, KV_LORA, N_HEADS * (QK_NOPE + V_DIM)), KV_LORA),
        "mla_wo": nrm(ks[16], (NA, N_HEADS * V_DIM, D), N_HEADS * V_DIM),
        "conv_w1": nrm(ks[17], (NC, D, 2 * D), D),
        "conv_b1": 0.02 * jax.random.normal(ks[18], (NC, 2 * D), F32),
        "conv_dw": nrm(ks[19], (NC, CONV_WIDTH, D), CONV_WIDTH),
        "conv_dwb": 0.02 * jax.random.normal(ks[20], (NC, D), F32),
        "conv_ln_g": 1.0 + 0.02 * jax.random.normal(ks[21], (NC, D), F32),
        "conv_ln_b": 0.02 * jax.random.normal(ks[22], (NC, D), F32),
        "conv_w2": nrm(ks[23], (NC, D, D), D),
        "conv_b2": 0.02 * jax.random.normal(ks[24], (NC, D), F32),
        "final_g": 1.0 + 0.02 * jax.random.normal(ks[25], (D,), F32),
    }


def reference(x, c, ctx, c_ctx, ada_w, ada_b, norm_g, ffn_w1, ffn_w3, ffn_w2,
              mla_wdq, mla_gq, mla_wuq, mla_wdkv, mla_gkv, mla_wukv, mla_wo,
              conv_w1, conv_b1, conv_dw, conv_dwb, conv_ln_g, conv_ln_b, conv_w2, conv_b2,
              final_g):
    b, s, _ = x.shape
    rope = axial_rope_tables(s)
    h, hc = x, ctx
    for i in range(DEPTH):
        last = i == DEPTH - 1
        mixer = i % N_MIXERS
        j = i // N_MIXERS
        ctx_into_mixer = (not last) or mixer == 0
        ctx_out = not last
        m_l = ada_mod(c, ada_w[i], ada_b[i])
        m_c = ada_mod(c_ctx, ada_w[i], ada_b[i])

        def ffn_sub(hh, m, k, f):
            y = swiglu(norm_mod(hh, norm_g[i, k], m[3 * k], m[3 * k + 1]),
                       ffn_w1[i, f], ffn_w3[i, f], ffn_w2[i, f])
            return hh + FFN_RES_WEIGHT * m[3 * k + 2] * y

        h = ffn_sub(h, m_l, 0, 0)
        if ctx_into_mixer:
            hc = ffn_sub(hc, m_c, 0, 0)

        u_l = norm_mod(h, norm_g[i, 1], m_l[3], m_l[4])
        if mixer == 0:
            w = (mla_wdq[j], mla_gq[j], mla_wuq[j], mla_wdkv[j], mla_gkv[j], mla_wukv[j])
            u_c = norm_mod(hc, norm_g[i, 1], m_c[3], m_c[4])
            q_c, k_c, v_c = mla_qkv(u_c, *w, None)
            q_l, k_l, v_l = mla_qkv(u_l, *w, rope)
            k_all = jnp.concatenate([k_c, k_l], axis=1)
            v_all = jnp.concatenate([v_c, v_l], axis=1)
            o_l = block_attention(q_l, k_all, v_all).reshape(b, s, N_HEADS * V_DIM) @ mla_wo[j]
            h = h + m_l[5] * o_l
            if ctx_out:
                o_c = attend(q_c, k_c, v_c).reshape(b, hc.shape[1], N_HEADS * V_DIM) @ mla_wo[j]
                hc = hc + m_c[5] * o_c
        else:
            cw = (conv_w1[j], conv_b1[j], conv_dw[j], conv_dwb[j], conv_ln_g[j], conv_ln_b[j],
                  conv_w2[j], conv_b2[j])
            h = h + m_l[5] * conv_module(u_l, *cw)
            if ctx_out:
                u_c = norm_mod(hc, norm_g[i, 1], m_c[3], m_c[4])
                hc = hc + m_c[5] * conv_module(u_c, *cw)

        h = ffn_sub(h, m_l, 2, 1)
        if ctx_out:
            hc = ffn_sub(hc, m_c, 2, 1)
    return rms_norm(h, final_g)
```

```python
import functools
import math

import numpy as np
import jax
import jax.numpy as jnp
from jax import lax
from jax.experimental import pallas as pl
from jax.experimental.pallas import tpu as pltpu

F32 = jnp.float32
BF16 = jnp.bfloat16

GRID_W = 64
QK_NOPE = 128
QK_ROPE = 64
V_DIM = 128
ROPE_THETA = 10000.0
FFN_RES_WEIGHT = 0.5
N_MOD = 9
EPS = 1e-6

LANES = 128
SUBLANES = 8
VMEM_LIMIT_BYTES = 56 * 1024 * 1024

MOD_TILE = 256
COND_ROWS = 8
HALO = 16
CAST_ROWS = 256


def _params(*sem):
    return pltpu.CompilerParams(dimension_semantics=sem, vmem_limit_bytes=VMEM_LIMIT_BYTES)


def _tile(n, pref, unit):
    if n <= pref:
        return n
    t = (pref // unit) * unit
    while t > unit and n % t:
        t -= unit
    assert n % t == 0, (n, pref, unit)
    return t


def _cast_weight(src_ref, dst_ref):
    k = src_ref.shape[0]
    rows = _tile(k, CAST_ROWS, SUBLANES)

    def body(c, carry):
        s = pl.multiple_of(c * rows, rows)
        dst_ref[pl.ds(s, rows), :] = src_ref[pl.ds(s, rows), :].astype(BF16)
        return carry

    lax.fori_loop(0, k // rows, body, 0)


def _sigmoid(x):
    return 1.0 / (1.0 + jnp.exp(-x))


def _ada_kernel(c_ref, w_ref, b_ref, o_ref):
    c = c_ref[...]
    s = (c * _sigmoid(c)).astype(BF16)
    o_ref[0] = jnp.dot(s, w_ref[0].astype(BF16), preferred_element_type=F32) + b_ref[0]


def ada_mods(cond, ada_w, ada_b):
    depth, d, n = ada_w.shape
    tn = _tile(n, 1024, LANES)
    return pl.pallas_call(
        _ada_kernel,
        grid=(depth, n // tn),
        in_specs=[
            pl.BlockSpec((COND_ROWS, d), lambda l, j: (0, 0)),
            pl.BlockSpec((1, d, tn), lambda l, j: (l, 0, j)),
            pl.BlockSpec((1, 1, tn), lambda l, j: (l, 0, j)),
        ],
        out_specs=pl.BlockSpec((1, COND_ROWS, tn), lambda l, j: (l, 0, j)),
        out_shape=jax.ShapeDtypeStruct((depth, COND_ROWS, n), F32),
        compiler_params=_params("arbitrary", "arbitrary"),
        name="ada_mods",
    )(cond, ada_w, ada_b.reshape(depth, 1, n))


def _norm_kernel(*refs, modulate, n_sub):
    if modulate:
        x_ref, g_ref, sh_ref, sc_ref, o_ref = refs
    else:
        x_ref, g_ref, o_ref = refs
    for r in range(n_sub):
        rows = slice(r * MOD_TILE, (r + 1) * MOD_TILE)
        x = x_ref[rows, :]
        y = x * lax.rsqrt(jnp.mean(x * x, axis=-1, keepdims=True) + EPS) * g_ref[...]
        if modulate:
            y = y * (1.0 + sc_ref[r]) + sh_ref[r]
        o_ref[rows, :] = y.astype(o_ref.dtype)


def norm_mod(h, g, shift=None, scale=None, out_dtype=BF16):
    m, d = h.shape
    tm = _tile(m, 2 * MOD_TILE, MOD_TILE)
    n_sub = tm // MOD_TILE
    modulate = shift is not None
    in_specs = [pl.BlockSpec((tm, d), lambda i: (i, 0)),
                pl.BlockSpec((1, d), lambda i: (0, 0))]
    args = [h, g.reshape(1, d)]
    if modulate:
        in_specs += [pl.BlockSpec((n_sub, 1, d), lambda i: (i, 0, 0))] * 2
        args += [shift, scale]
    return pl.pallas_call(
        functools.partial(_norm_kernel, modulate=modulate, n_sub=n_sub),
        grid=(m // tm,),
        in_specs=in_specs,
        out_specs=pl.BlockSpec((tm, d), lambda i: (i, 0)),
        out_shape=jax.ShapeDtypeStruct((m, d), out_dtype),
        compiler_params=_params("arbitrary"),
        name="norm_mod",
    )(*args)


def _up_kernel(*refs, act, has_bias):
    if has_bias:
        x_ref, wa_ref, wb_ref, ba_ref, bb_ref, o_ref, wa_s, wb_s = refs
    else:
        x_ref, wa_ref, wb_ref, o_ref, wa_s, wb_s = refs

    @pl.when(pl.program_id(1) == 0)
    def _():
        _cast_weight(wa_ref, wa_s)
        _cast_weight(wb_ref, wb_s)

    x = x_ref[...]
    a = jnp.dot(x, wa_s[...], preferred_element_type=F32)
    b = jnp.dot(x, wb_s[...], preferred_element_type=F32)
    if has_bias:
        a = a + ba_ref[...]
        b = b + bb_ref[...]
    if act == "swiglu":
        o = (a * _sigmoid(a)) * b
    else:
        o = a * _sigmoid(b)
    o_ref[...] = o.astype(o_ref.dtype)


def gated_up(x, wa, wa_idx, wa_col, wb, wb_idx, wb_col, n, act, bias=None, out_dtype=BF16):
    m, k = x.shape
    tm = _tile(m, 1024, MOD_TILE)
    tn = _tile(n, 512, LANES)
    nj = n // tn

    def wspec(idx, col):
        lead = (None,) * len(idx)
        return pl.BlockSpec(lead + (k, tn), lambda j, i: idx + (0, col * nj + j))

    in_specs = [pl.BlockSpec((tm, k), lambda j, i: (i, 0)), wspec(wa_idx, wa_col), wspec(wb_idx, wb_col)]
    args = [x, wa, wb]
    if bias is not None:
        ba, bb = bias
        in_specs += [pl.BlockSpec((1, tn), lambda j, i: (0, j))] * 2
        args += [ba.reshape(1, n), bb.reshape(1, n)]
    return pl.pallas_call(
        functools.partial(_up_kernel, act=act, has_bias=bias is not None),
        grid=(nj, m // tm),
        in_specs=in_specs,
        out_specs=pl.BlockSpec((tm, tn), lambda j, i: (i, j)),
        out_shape=jax.ShapeDtypeStruct((m, n), out_dtype),
        scratch_shapes=[pltpu.VMEM((k, tn), BF16), pltpu.VMEM((k, tn), BF16)],
        compiler_params=_params("arbitrary", "arbitrary"),
        name="gated_up_" + act,
    )(*args)


def _res_kernel(*refs, coef, n_sub, has_bias):
    if has_bias:
        x_ref, w_ref, b_ref, h_ref, gate_ref, o_ref, w_s = refs
    else:
        x_ref, w_ref, h_ref, gate_ref, o_ref, w_s = refs

    @pl.when(pl.program_id(1) == 0)
    def _():
        _cast_weight(w_ref, w_s)

    y = jnp.dot(x_ref[...], w_s[...], preferred_element_type=F32)
    if has_bias:
        y = y + b_ref[...]
    for r in range(n_sub):
        rows = slice(r * MOD_TILE, (r + 1) * MOD_TILE)
        o_ref[rows, :] = h_ref[rows, :] + (coef * gate_ref[r]) * y[rows, :]


def proj_residual(x, w, w_idx, h, gate, coef, bias=None):
    m, k = x.shape
    n = h.shape[1]
    tm = _tile(m, 512, MOD_TILE)
    tn = _tile(n, 512, LANES)
    n_sub = tm // MOD_TILE
    lead = (None,) * len(w_idx)
    in_specs = [pl.BlockSpec((tm, k), lambda j, i: (i, 0)),
                pl.BlockSpec(lead + (k, tn), lambda j, i: w_idx + (0, j))]
    args = [x, w]
    if bias is not None:
        in_specs.append(pl.BlockSpec((1, tn), lambda j, i: (0, j)))
        args.append(bias.reshape(1, n))
    in_specs += [pl.BlockSpec((tm, tn), lambda j, i: (i, j)),
                 pl.BlockSpec((n_sub, 1, tn), lambda j, i: (i, 0, j))]
    args += [h, gate]
    return pl.pallas_call(
        functools.partial(_res_kernel, coef=coef, n_sub=n_sub, has_bias=bias is not None),
        grid=(n // tn, m // tm),
        in_specs=in_specs,
        out_specs=pl.BlockSpec((tm, tn), lambda j, i: (i, j)),
        out_shape=jax.ShapeDtypeStruct((m, n), F32),
        scratch_shapes=[pltpu.VMEM((k, tn), BF16)],
        compiler_params=_params("arbitrary", "arbitrary"),
        name="proj_residual",
    )(*args)


def _rms(x, g):
    return x * lax.rsqrt(jnp.mean(x * x, axis=-1, keepdims=True) + EPS) * g


def _mla_down_kernel(x_ref, wq_ref, wkv_ref, gq_ref, gkv_ref, cos_ref, sin_ref,
                     cq_ref, ckv_ref, kpe_ref, wq_s, wkv_s, *, lora):
    @pl.when(pl.program_id(0) == 0)
    def _():
        _cast_weight(wq_ref, wq_s)
        _cast_weight(wkv_ref, wkv_s)

    x = x_ref[...]
    cq = jnp.dot(x, wq_s[...], preferred_element_type=F32)
    cq_ref[...] = _rms(cq, gq_ref[...]).astype(BF16)
    kv = jnp.dot(x, wkv_s[...], preferred_element_type=F32)
    ckv_ref[...] = _rms(kv[:, :lora], gkv_ref[...]).astype(BF16)
    kpe = kv[:, lora:lora + LANES] * cos_ref[...] + kv[:, lora + LANES:] * sin_ref[...]
    kpe_ref[...] = kpe.astype(BF16)


def mla_down(xn, wdq, wkv, gq, gkv, cos, sin):
    m, d = xn.shape
    q_lora = wdq.shape[1]
    kv_lora = gkv.shape[0]
    nkv = wkv.shape[1]
    tm = _tile(m, 512, MOD_TILE)
    const = lambda i: (0, 0)
    row = lambda i: (i, 0)
    return pl.pallas_call(
        functools.partial(_mla_down_kernel, lora=kv_lora),
        grid=(m // tm,),
        in_specs=[pl.BlockSpec((tm, d), row),
                  pl.BlockSpec((d, q_lora), const),
                  pl.BlockSpec((d, nkv), const),
                  pl.BlockSpec((1, q_lora), const),
                  pl.BlockSpec((1, kv_lora), const),
                  pl.BlockSpec((tm, LANES), row),
                  pl.BlockSpec((tm, LANES), row)],
        out_specs=[pl.BlockSpec((tm, q_lora), row),
                   pl.BlockSpec((tm, kv_lora), row),
                   pl.BlockSpec((tm, LANES), row)],
        out_shape=[jax.ShapeDtypeStruct((m, q_lora), BF16),
                   jax.ShapeDtypeStruct((m, kv_lora), BF16),
                   jax.ShapeDtypeStruct((m, LANES), BF16)],
        scratch_shapes=[pltpu.VMEM((d, q_lora), BF16), pltpu.VMEM((d, nkv), BF16)],
        compiler_params=_params("arbitrary"),
        name="mla_down",
    )(xn, wdq, wkv, gq.reshape(1, -1), gkv.reshape(1, -1), cos, sin)


def _mla_up_kernel(cq_ref, ckv_ref, kpe_ref, cos_ref, sin_ref, wq_ref, wkv_ref,
                   q_ref, k_ref, v_ref, *, q_scale):
    qf = jnp.dot(cq_ref[0], wq_ref[0].astype(BF16), preferred_element_type=F32)
    q_pe = qf[:, QK_NOPE:QK_NOPE + LANES] * cos_ref[...] + qf[:, QK_NOPE + LANES:] * sin_ref[...]
    q_ref[0, 0, :, :QK_NOPE] = (qf[:, :QK_NOPE] * q_scale).astype(BF16)
    q_ref[0, 0, :, QK_NOPE:] = (q_pe * q_scale).astype(BF16)
    kvf = jnp.dot(ckv_ref[0], wkv_ref[0].astype(BF16), preferred_element_type=F32)
    k_ref[0, 0, :, :QK_NOPE] = kvf[:, :QK_NOPE].astype(BF16)
    k_ref[0, 0, :, QK_NOPE:] = kpe_ref[0]
    v_ref[0, 0] = kvf[:, QK_NOPE:].astype(BF16)


def mla_up(cq, ckv, kpe, cos, sin, wq_h, wkv_h, q_scale):
    b, t, q_lora = cq.shape
    kv_lora = ckv.shape[2]
    nh = wq_h.shape[0]
    dqk = QK_NOPE + LANES
    tok = lambda bi, hi: (bi, 0, 0)
    head = lambda bi, hi: (hi, 0, 0)
    out = lambda bi, hi: (bi, hi, 0, 0)
    return pl.pallas_call(
        functools.partial(_mla_up_kernel, q_scale=q_scale),
        grid=(b, nh),
        in_specs=[pl.BlockSpec((1, t, q_lora), tok),
                  pl.BlockSpec((1, t, kv_lora), tok),
                  pl.BlockSpec((1, t, LANES), tok),
                  pl.BlockSpec((t, LANES), lambda bi, hi: (0, 0)),
                  pl.BlockSpec((t, LANES), lambda bi, hi: (0, 0)),
                  pl.BlockSpec((1, q_lora, wq_h.shape[2]), head),
                  pl.BlockSpec((1, kv_lora, wkv_h.shape[2]), head)],
        out_specs=[pl.BlockSpec((1, 1, t, dqk), out),
                   pl.BlockSpec((1, 1, t, dqk), out),
                   pl.BlockSpec((1, 1, t, V_DIM), out)],
        out_shape=[jax.ShapeDtypeStruct((b, nh, t, dqk), BF16),
                   jax.ShapeDtypeStruct((b, nh, t, dqk), BF16),
                   jax.ShapeDtypeStruct((b, nh, t, V_DIM), BF16)],
        compiler_params=_params("arbitrary", "arbitrary"),
        name="mla_up",
    )(cq, ckv, kpe, cos, sin, wq_h, wkv_h)


def _attn_kernel(q_ref, k_ref, v_ref, o_ref):
    s = lax.dot_general(q_ref[0, 0], k_ref[0, 0], (((1,), (1,)), ((), ())),
                        preferred_element_type=F32)
    p = jnp.exp2(s - jnp.max(s, axis=-1, keepdims=True))
    l = jnp.sum(p, axis=-1, keepdims=True)
    o = jnp.dot(p.astype(BF16), v_ref[0, 0], preferred_element_type=F32)
    o_ref[0] = (o / l).astype(BF16)


def attention(q, k, v, q_start, q_len, k_start, k_len, tq):
    b, nh, _, dqk = q.shape
    assert q_start % tq == 0 and q_len % tq == 0 and k_start % k_len == 0
    q0, k0 = q_start // tq, k_start // k_len
    return pl.pallas_call(
        _attn_kernel,
        grid=(b, nh, q_len // tq),
        in_specs=[pl.BlockSpec((1, 1, tq, dqk), lambda bi, hi, i: (bi, hi, q0 + i, 0)),
                  pl.BlockSpec((1, 1, k_len, dqk), lambda bi, hi, i: (bi, hi, k0, 0)),
                  pl.BlockSpec((1, 1, k_len, V_DIM), lambda bi, hi, i: (bi, hi, k0, 0))],
        out_specs=pl.BlockSpec((1, tq, V_DIM), lambda bi, hi, i: (bi, i, hi)),
        out_shape=jax.ShapeDtypeStruct((b, q_len, nh * V_DIM), BF16),
        compiler_params=_params("arbitrary", "arbitrary", "arbitrary"),
        name="attention",
    )(q, k, v)


def _dwconv_kernel(prev_ref, cur_ref, next_ref, dw_ref, dwb_ref, g_ref, b_ref, o_ref, buf, *, width):
    i = pl.program_id(1)
    tt = cur_ref.shape[1]
    pad = width // 2
    buf[0:HALO, :] = jnp.where(i > 0, prev_ref[0], 0.0)
    buf[HALO:HALO + tt, :] = cur_ref[0]
    buf[HALO + tt:, :] = jnp.where(i < pl.num_programs(1) - 1, next_ref[0], 0.0)
    acc = jnp.zeros(cur_ref.shape[1:], F32) + dwb_ref[...]
    for kk in range(width):
        off = HALO - pad + kk
        acc = acc + buf[off:off + tt, :] * dw_ref[kk:kk + 1, :]
    mu = jnp.mean(acc, axis=-1, keepdims=True)
    xc = acc - mu
    var = jnp.mean(xc * xc, axis=-1, keepdims=True)
    y = xc * lax.rsqrt(var + EPS) * g_ref[...] + b_ref[...]
    o_ref[0] = (y * _sigmoid(y)).astype(o_ref.dtype)


def dwconv_ln_swish(u, dw, dwb, ln_g, ln_b):
    b, t, d = u.shape
    width = dw.shape[0]
    tt = _tile(t, 128, HALO)
    nh = tt // HALO
    last = t // HALO - 1
    vec = lambda bi, i: (0, 0)
    return pl.pallas_call(
        functools.partial(_dwconv_kernel, width=width),
        grid=(b, t // tt),
        in_specs=[pl.BlockSpec((1, HALO, d), lambda bi, i: (bi, jnp.maximum(i * nh - 1, 0), 0)),
                  pl.BlockSpec((1, tt, d), lambda bi, i: (bi, i, 0)),
                  pl.BlockSpec((1, HALO, d), lambda bi, i: (bi, jnp.minimum((i + 1) * nh, last), 0)),
                  pl.BlockSpec((width, d), vec),
                  pl.BlockSpec((1, d), vec), pl.BlockSpec((1, d), vec), pl.BlockSpec((1, d), vec)],
        out_specs=pl.BlockSpec((1, tt, d), lambda bi, i: (bi, i, 0)),
        out_shape=jax.ShapeDtypeStruct((b, t, d), BF16),
        scratch_shapes=[pltpu.VMEM((tt + 2 * HALO, d), F32)],
        compiler_params=_params("arbitrary", "arbitrary"),
        name="dwconv_ln_swish",
    )(u, u, u, dw, dwb.reshape(1, d), ln_g.reshape(1, d), ln_b.reshape(1, d))


def _rope_tables(seq, ctx_len):
    n_axis = QK_ROPE // 4
    t = jnp.arange(seq, dtype=jnp.int32)
    row = (t // GRID_W).astype(F32)
    col = (t % GRID_W).astype(F32)
    freqs = ROPE_THETA ** (-jnp.arange(n_axis, dtype=F32) / n_axis)
    ang = jnp.concatenate([row[:, None] * freqs, col[:, None] * freqs], axis=-1)
    ang = jnp.concatenate([ang, jnp.zeros((ctx_len, QK_ROPE // 2), F32)], axis=0)
    cos, sin = jnp.cos(ang), jnp.sin(ang)
    zeros = jnp.zeros((seq + ctx_len, LANES - QK_ROPE), F32)
    cos_t = jnp.concatenate([cos, cos, zeros], axis=-1)
    sin_t = jnp.concatenate([-sin, sin, zeros], axis=-1)
    return cos_t, sin_t


def _rope_cols(w):
    even, odd = w[..., 0::2], w[..., 1::2]
    z = jnp.zeros(w.shape[:-1] + (LANES - QK_ROPE,), w.dtype)
    return jnp.concatenate([even, odd, z, odd, even, z], axis=-1)


def _mod_tiles(m, rows, k, d):
    return m[rows, k * d:(k + 1) * d].reshape(len(rows), 1, d)


def kernel(x, c, ctx, c_ctx, ada_w, ada_b, norm_g, ffn_w1, ffn_w3, ffn_w2, mla_wdq, mla_gq, mla_wuq, mla_wdkv, mla_gkv, mla_wukv, mla_wo, conv_w1, conv_b1, conv_dw, conv_dwb, conv_ln_g, conv_ln_b, conv_w2, conv_b2, final_g):
    b, s, d = x.shape
    lc = ctx.shape[1]
    depth = ada_w.shape[0]
    d_ff = ffn_w1.shape[-1]
    nh = mla_wo.shape[1] // V_DIM
    kv_lora = mla_gkv.shape[1]
    ctx_row = b
    assert s % MOD_TILE == 0 and lc % MOD_TILE == 0 and b < COND_ROWS

    cond = jnp.concatenate([c, c_ctx[None, :], jnp.zeros((COND_ROWS - b - 1, d), F32)], axis=0)
    mods = ada_mods(cond, ada_w, ada_b)

    t_all = s + lc
    per_b = t_all // MOD_TILE
    rows_all = np.array([bi if r < s // MOD_TILE else ctx_row for bi in range(b) for r in range(per_b)])
    rows_lat = np.array([bi for bi in range(b) for _ in range(s // MOD_TILE)])

    h = jnp.concatenate([x, ctx], axis=1).reshape(b * t_all, d)
    combined = True

    def ffn(h, m, rows, i, k, f):
        xn = norm_mod(h, norm_g[i, k], _mod_tiles(m, rows, 3 * k, d), _mod_tiles(m, rows, 3 * k + 1, d))
        g = gated_up(xn, ffn_w1, (i, f), 0, ffn_w3, (i, f), 0, d_ff, "swiglu")
        return proj_residual(g, ffn_w2, (i, f), h, _mod_tiles(m, rows, 3 * k + 2, d), FFN_RES_WEIGHT)

    for i in range(depth):
        last = i == depth - 1
        mixer = i % 2
        j = i // 2
        ctx_into_mixer = (not last) or mixer == 0
        ctx_out = not last
        m = mods[i]
        if combined and not ctx_into_mixer:
            h = h.reshape(b, t_all, d)[:, :s].reshape(b * s, d)
            combined = False
        rows = rows_all if combined else rows_lat
        t_cur = t_all if combined else s

        h = ffn(h, m, rows, i, 0, 0)

        xn = norm_mod(h, norm_g[i, 1], _mod_tiles(m, rows, 3, d), _mod_tiles(m, rows, 4, d))
        gate = _mod_tiles(m, rows, 5, d)
        if mixer == 0:
            assert combined
            cos_t, sin_t = _rope_tables(s, lc)
            wkv = jnp.concatenate([mla_wdkv[j][:, :kv_lora], _rope_cols(mla_wdkv[j][:, kv_lora:])], axis=1)
            cq, ckv, kpe = mla_down(xn, mla_wdq[j], wkv, mla_gq[j], mla_gkv[j],
                                    jnp.tile(cos_t, (b, 1)), jnp.tile(sin_t, (b, 1)))
            wq3 = mla_wuq[j].reshape(-1, nh, QK_NOPE + QK_ROPE)
            wq_h = jnp.concatenate([wq3[..., :QK_NOPE], _rope_cols(wq3[..., QK_NOPE:])], axis=-1).transpose(1, 0, 2)
            wkv_h = mla_wukv[j].reshape(-1, nh, QK_NOPE + V_DIM).transpose(1, 0, 2)
            q_scale = (QK_NOPE + QK_ROPE) ** -0.5 * math.log2(math.e)
            q, k, v = mla_up(cq.reshape(b, t_all, -1), ckv.reshape(b, t_all, -1), kpe.reshape(b, t_all, -1),
                             cos_t, sin_t, wq_h, wkv_h, q_scale)
            o_l = attention(q, k, v, 0, s, 0, t_all, _tile(s, 512, MOD_TILE))
            if ctx_out:
                o_c = attention(q, k, v, s, lc, s, lc, lc)
            else:
                o_c = jnp.zeros((b, lc, nh * V_DIM), BF16)
            o = jnp.concatenate([o_l, o_c], axis=1).reshape(b * t_all, nh * V_DIM)
            h = proj_residual(o, mla_wo, (j,), h, gate, 1.0)
        else:
            assert not combined
            u = gated_up(xn, conv_w1, (j,), 0, conv_w1, (j,), 1, d, "glu",
                         bias=(conv_b1[j, :d], conv_b1[j, d:]), out_dtype=F32)
            vv = dwconv_ln_swish(u.reshape(-1, t_cur, d), conv_dw[j], conv_dwb[j], conv_ln_g[j], conv_ln_b[j])
            h = proj_residual(vv.reshape(-1, d), conv_w2, (j,), h, gate, 1.0, bias=conv_b2[j])

        if combined and not ctx_out:
            h = h.reshape(b, t_all, d)[:, :s].reshape(b * s, d)
            combined = False
            rows = rows_lat
        h = ffn(h, m, rows, i, 2, 1)

    if combined:
        h = h.reshape(b, t_all, d)[:, :s].reshape(b * s, d)
    return norm_mod(h, final_g, out_dtype=F32).reshape(b, s, d)
```

```python
import functools
import math

import numpy as np
import jax
import jax.numpy as jnp
from jax import lax
from jax.experimental import pallas as pl
from jax.experimental.pallas import tpu as pltpu

F32 = jnp.float32
BF16 = jnp.bfloat16

GRID_W = 64
QK_NOPE = 128
QK_ROPE = 64
V_DIM = 128
ROPE_THETA = 10000.0
FFN_RES_WEIGHT = 0.5
N_MOD = 9
EPS = 1e-6

LANES = 128
SUBLANES = 8
VMEM_LIMIT_BYTES = 56 * 1024 * 1024

MOD_TILE = 256
COND_ROWS = 8
HALO = 16
CAST_ROWS = 256


def _params(*sem):
    return pltpu.CompilerParams(dimension_semantics=sem, vmem_limit_bytes=VMEM_LIMIT_BYTES)


def _tile(n, pref, unit):
    if n <= pref:
        return n
    t = (pref // unit) * unit
    while t > unit and n % t:
        t -= unit
    assert n % t == 0, (n, pref, unit)
    return t


def _cast_weight(src_ref, dst_ref):
    k = src_ref.shape[0]
    rows = _tile(k, CAST_ROWS, SUBLANES)

    def body(c, carry):
        s = pl.multiple_of(c * rows, rows)
        dst_ref[pl.ds(s, rows), :] = src_ref[pl.ds(s, rows), :].astype(BF16)
        return carry

    lax.fori_loop(0, k // rows, body, 0)


def _sigmoid(x):
    return 1.0 / (1.0 + jnp.exp(-x))


def _ada_kernel(c_ref, w_ref, b_ref, o_ref):
    c = c_ref[...]
    s = (c * _sigmoid(c)).astype(BF16)
    o_ref[0] = jnp.dot(s, w_ref[0].astype(BF16), preferred_element_type=F32) + b_ref[0]


def ada_mods(cond, ada_w, ada_b):
    depth, d, n = ada_w.shape
    tn = _tile(n, 1024, LANES)
    return pl.pallas_call(
        _ada_kernel,
        grid=(depth, n // tn),
        in_specs=[
            pl.BlockSpec((COND_ROWS, d), lambda l, j: (0, 0)),
            pl.BlockSpec((1, d, tn), lambda l, j: (l, 0, j)),
            pl.BlockSpec((1, 1, tn), lambda l, j: (l, 0, j)),
        ],
        out_specs=pl.BlockSpec((1, COND_ROWS, tn), lambda l, j: (l, 0, j)),
        out_shape=jax.ShapeDtypeStruct((depth, COND_ROWS, n), F32),
        compiler_params=_params("arbitrary", "arbitrary"),
        name="ada_mods",
    )(cond, ada_w, ada_b.reshape(depth, 1, n))


def _norm_kernel(*refs, modulate, n_sub):
    if modulate:
        x_ref, g_ref, sh_ref, sc_ref, o_ref = refs
    else:
        x_ref, g_ref, o_ref = refs
    for r in range(n_sub):
        rows = slice(r * MOD_TILE, (r + 1) * MOD_TILE)
        x = x_ref[rows, :]
        y = x * lax.rsqrt(jnp.mean(x * x, axis=-1, keepdims=True) + EPS) * g_ref[...]
        if modulate:
            y = y * (1.0 + sc_ref[r]) + sh_ref[r]
        o_ref[rows, :] = y.astype(o_ref.dtype)


def norm_mod(h, g, shift=None, scale=None, out_dtype=BF16):
    m, d = h.shape
    tm = _tile(m, 2 * MOD_TILE, MOD_TILE)
    n_sub = tm // MOD_TILE
    modulate = shift is not None
    in_specs = [pl.BlockSpec((tm, d), lambda i: (i, 0)),
                pl.BlockSpec((1, d), lambda i: (0, 0))]
    args = [h, g.reshape(1, d)]
    if modulate:
        in_specs += [pl.BlockSpec((n_sub, 1, d), lambda i: (i, 0, 0))] * 2
        args += [shift, scale]
    return pl.pallas_call(
        functools.partial(_norm_kernel, modulate=modulate, n_sub=n_sub),
        grid=(m // tm,),
        in_specs=in_specs,
        out_specs=pl.BlockSpec((tm, d), lambda i: (i, 0)),
        out_shape=jax.ShapeDtypeStruct((m, d), out_dtype),
        compiler_params=_params("arbitrary"),
        name="norm_mod",
    )(*args)


def _up_kernel(*refs, act, has_bias):
    if has_bias:
        x_ref, wa_ref, wb_ref, ba_ref, bb_ref, o_ref, wa_s, wb_s = refs
    else:
        x_ref, wa_ref, wb_ref, o_ref, wa_s, wb_s = refs

    @pl.when(pl.program_id(1) == 0)
    def _():
        _cast_weight(wa_ref, wa_s)
        _cast_weight(wb_ref, wb_s)

    x = x_ref[...]
    a = jnp.dot(x, wa_s[...], preferred_element_type=F32)
    b = jnp.dot(x, wb_s[...], preferred_element_type=F32)
    if has_bias:
        a = a + ba_ref[...]
        b = b + bb_ref[...]
    if act == "swiglu":
        o = (a * _sigmoid(a)) * b
    else:
        o = a * _sigmoid(b)
    o_ref[...] = o.astype(o_ref.dtype)


def gated_up(x, wa, wa_idx, wa_col, wb, wb_idx, wb_col, n, act, bias=None, out_dtype=BF16):
    m, k = x.shape
    tm = _tile(m, 1024, MOD_TILE)
    tn = _tile(n, 512, LANES)
    nj = n // tn

    def wspec(idx, col):
        lead = (None,) * len(idx)
        return pl.BlockSpec(lead + (k, tn), lambda j, i: idx + (0, col * nj + j))

    in_specs = [pl.BlockSpec((tm, k), lambda j, i: (i, 0)), wspec(wa_idx, wa_col), wspec(wb_idx, wb_col)]
    args = [x, wa, wb]
    if bias is not None:
        ba, bb = bias
        in_specs += [pl.BlockSpec((1, tn), lambda j, i: (0, j))] * 2
        args += [ba.reshape(1, n), bb.reshape(1, n)]
    return pl.pallas_call(
        functools.partial(_up_kernel, act=act, has_bias=bias is not None),
        grid=(nj, m // tm),
        in_specs=in_specs,
        out_specs=pl.BlockSpec((tm, tn), lambda j, i: (i, j)),
        out_shape=jax.ShapeDtypeStruct((m, n), out_dtype),
        scratch_shapes=[pltpu.VMEM((k, tn), BF16), pltpu.VMEM((k, tn), BF16)],
        compiler_params=_params("arbitrary", "arbitrary"),
        name="gated_up_" + act,
    )(*args)


def _res_kernel(*refs, coef, n_sub, has_bias):
    if has_bias:
        x_ref, w_ref, b_ref, h_ref, gate_ref, o_ref, w_s = refs
    else:
        x_ref, w_ref, h_ref, gate_ref, o_ref, w_s = refs

    @pl.when(pl.program_id(1) == 0)
    def _():
        _cast_weight(w_ref, w_s)

    y = jnp.dot(x_ref[...], w_s[...], preferred_element_type=F32)
    if has_bias:
        y = y + b_ref[...]
    for r in range(n_sub):
        rows = slice(r * MOD_TILE, (r + 1) * MOD_TILE)
        o_ref[rows, :] = h_ref[rows, :] + (coef * gate_ref[r]) * y[rows, :]


def proj_residual(x, w, w_idx, h, gate, coef, bias=None):
    m, k = x.shape
    n = h.shape[1]
    pref = 1024 if k * 1024 * 14 <= VMEM_LIMIT_BYTES // 2 else 512
    tm = _tile(m, pref, MOD_TILE)
    tn = _tile(n, pref, LANES)
    n_sub = tm // MOD_TILE
    lead = (None,) * len(w_idx)
    in_specs = [pl.BlockSpec((tm, k), lambda j, i: (i, 0)),
                pl.BlockSpec(lead + (k, tn), lambda j, i: w_idx + (0, j))]
    args = [x, w]
    if bias is not None:
        in_specs.append(pl.BlockSpec((1, tn), lambda j, i: (0, j)))
        args.append(bias.reshape(1, n))
    in_specs += [pl.BlockSpec((tm, tn), lambda j, i: (i, j)),
                 pl.BlockSpec((n_sub, 1, tn), lambda j, i: (i, 0, j))]
    args += [h, gate]
    return pl.pallas_call(
        functools.partial(_res_kernel, coef=coef, n_sub=n_sub, has_bias=bias is not None),
        grid=(n // tn, m // tm),
        in_specs=in_specs,
        out_specs=pl.BlockSpec((tm, tn), lambda j, i: (i, j)),
        out_shape=jax.ShapeDtypeStruct((m, n), F32),
        scratch_shapes=[pltpu.VMEM((k, tn), BF16)],
        compiler_params=_params("arbitrary", "arbitrary"),
        name="proj_residual",
    )(*args)


def _rms(x, g):
    return x * lax.rsqrt(jnp.mean(x * x, axis=-1, keepdims=True) + EPS) * g


def _mla_down_kernel(x_ref, wq_ref, wkv_ref, gq_ref, gkv_ref, cos_ref, sin_ref,
                     cq_ref, ckv_ref, kpe_ref, wq_s, wkv_s, *, lora):
    @pl.when(pl.program_id(0) == 0)
    def _():
        _cast_weight(wq_ref, wq_s)
        _cast_weight(wkv_ref, wkv_s)

    x = x_ref[...]
    cq = jnp.dot(x, wq_s[...], preferred_element_type=F32)
    cq_ref[...] = _rms(cq, gq_ref[...]).astype(BF16)
    kv = jnp.dot(x, wkv_s[...], preferred_element_type=F32)
    ckv_ref[...] = _rms(kv[:, :lora], gkv_ref[...]).astype(BF16)
    kpe = kv[:, lora:lora + LANES] * cos_ref[...] + kv[:, lora + LANES:] * sin_ref[...]
    kpe_ref[...] = kpe.astype(BF16)


def mla_down(xn, wdq, wkv, gq, gkv, cos, sin):
    m, d = xn.shape
    q_lora = wdq.shape[1]
    kv_lora = gkv.shape[0]
    nkv = wkv.shape[1]
    tm = _tile(m, 512, MOD_TILE)
    const = lambda i: (0, 0)
    row = lambda i: (i, 0)
    return pl.pallas_call(
        functools.partial(_mla_down_kernel, lora=kv_lora),
        grid=(m // tm,),
        in_specs=[pl.BlockSpec((tm, d), row),
                  pl.BlockSpec((d, q_lora), const),
                  pl.BlockSpec((d, nkv), const),
                  pl.BlockSpec((1, q_lora), const),
                  pl.BlockSpec((1, kv_lora), const),
                  pl.BlockSpec((tm, LANES), row),
                  pl.BlockSpec((tm, LANES), row)],
        out_specs=[pl.BlockSpec((tm, q_lora), row),
                   pl.BlockSpec((tm, kv_lora), row),
                   pl.BlockSpec((tm, LANES), row)],
        out_shape=[jax.ShapeDtypeStruct((m, q_lora), BF16),
                   jax.ShapeDtypeStruct((m, kv_lora), BF16),
                   jax.ShapeDtypeStruct((m, LANES), BF16)],
        scratch_shapes=[pltpu.VMEM((d, q_lora), BF16), pltpu.VMEM((d, nkv), BF16)],
        compiler_params=_params("arbitrary"),
        name="mla_down",
    )(xn, wdq, wkv, gq.reshape(1, -1), gkv.reshape(1, -1), cos, sin)


def _mla_up_kernel(cq_ref, ckv_ref, kpe_ref, cos_ref, sin_ref, wq_ref, wkv_ref,
                   q_ref, k_ref, v_ref, *, q_scale):
    qf = jnp.dot(cq_ref[0], wq_ref[0].astype(BF16), preferred_element_type=F32)
    q_pe = qf[:, QK_NOPE:QK_NOPE + LANES] * cos_ref[...] + qf[:, QK_NOPE + LANES:] * sin_ref[...]
    q_ref[0, 0, :, :QK_NOPE] = (qf[:, :QK_NOPE] * q_scale).astype(BF16)
    q_ref[0, 0, :, QK_NOPE:] = (q_pe * q_scale).astype(BF16)
    kvf = jnp.dot(ckv_ref[0], wkv_ref[0].astype(BF16), preferred_element_type=F32)
    k_ref[0, 0, :, :QK_NOPE] = kvf[:, :QK_NOPE].astype(BF16)
    k_ref[0, 0, :, QK_NOPE:] = kpe_ref[0]
    v_ref[0, 0] = kvf[:, QK_NOPE:].astype(BF16)


def mla_up(cq, ckv, kpe, cos, sin, wq_h, wkv_h, q_scale):
    b, t, q_lora = cq.shape
    kv_lora = ckv.shape[2]
    nh = wq_h.shape[0]
    dqk = QK_NOPE + LANES
    tok = lambda bi, hi: (bi, 0, 0)
    head = lambda bi, hi: (hi, 0, 0)
    out = lambda bi, hi: (bi, hi, 0, 0)
    return pl.pallas_call(
        functools.partial(_mla_up_kernel, q_scale=q_scale),
        grid=(b, nh),
        in_specs=[pl.BlockSpec((1, t, q_lora), tok),
                  pl.BlockSpec((1, t, kv_lora), tok),
                  pl.BlockSpec((1, t, LANES), tok),
                  pl.BlockSpec((t, LANES), lambda bi, hi: (0, 0)),
                  pl.BlockSpec((t, LANES), lambda bi, hi: (0, 0)),
                  pl.BlockSpec((1, q_lora, wq_h.shape[2]), head),
                  pl.BlockSpec((1, kv_lora, wkv_h.shape[2]), head)],
        out_specs=[pl.BlockSpec((1, 1, t, dqk), out),
                   pl.BlockSpec((1, 1, t, dqk), out),
                   pl.BlockSpec((1, 1, t, V_DIM), out)],
        out_shape=[jax.ShapeDtypeStruct((b, nh, t, dqk), BF16),
                   jax.ShapeDtypeStruct((b, nh, t, dqk), BF16),
                   jax.ShapeDtypeStruct((b, nh, t, V_DIM), BF16)],
        compiler_params=_params("arbitrary", "arbitrary"),
        name="mla_up",
    )(cq, ckv, kpe, cos, sin, wq_h, wkv_h)


ATTN_HEADS = 2
ATTN_ROWS = 256


def _attn_kernel(q_ref, k_ref, v_ref, o_ref):
    tq = q_ref.shape[2]
    rows = min(tq, ATTN_ROWS)
    for hh in range(q_ref.shape[1]):
        k = k_ref[0, hh]
        v = v_ref[0, hh]
        for r in range(tq // rows):
            rs = slice(r * rows, (r + 1) * rows)
            s = lax.dot_general(q_ref[0, hh, rs, :], k, (((1,), (1,)), ((), ())),
                                preferred_element_type=F32)
            p = jnp.exp2(s - jnp.max(s, axis=-1, keepdims=True))
            l = jnp.sum(p, axis=-1, keepdims=True)
            o = jnp.dot(p.astype(BF16), v, preferred_element_type=F32)
            o_ref[0, rs, hh * V_DIM:(hh + 1) * V_DIM] = (o / l).astype(BF16)


def attention(q, k, v, q_start, q_len, k_start, k_len, tq):
    b, nh, _, dqk = q.shape
    assert q_start % tq == 0 and q_len % tq == 0 and k_start % k_len == 0
    q0, k0 = q_start // tq, k_start // k_len
    hb = _tile(nh, ATTN_HEADS, 1)
    return pl.pallas_call(
        _attn_kernel,
        grid=(b, nh // hb, q_len // tq),
        in_specs=[pl.BlockSpec((1, hb, tq, dqk), lambda bi, hi, i: (bi, hi, q0 + i, 0)),
                  pl.BlockSpec((1, hb, k_len, dqk), lambda bi, hi, i: (bi, hi, k0, 0)),
                  pl.BlockSpec((1, hb, k_len, V_DIM), lambda bi, hi, i: (bi, hi, k0, 0))],
        out_specs=pl.BlockSpec((1, tq, hb * V_DIM), lambda bi, hi, i: (bi, i, hi)),
        out_shape=jax.ShapeDtypeStruct((b, q_len, nh * V_DIM), BF16),
        compiler_params=_params("arbitrary", "arbitrary", "arbitrary"),
        name="attention",
    )(q, k, v)


def _dwconv_kernel(prev_ref, cur_ref, next_ref, dw_ref, dwb_ref, g_ref, b_ref, o_ref, buf, *, width):
    i = pl.program_id(1)
    tt = cur_ref.shape[1]
    pad = width // 2
    buf[0:HALO, :] = jnp.where(i > 0, prev_ref[0], 0.0)
    buf[HALO:HALO + tt, :] = cur_ref[0]
    buf[HALO + tt:, :] = jnp.where(i < pl.num_programs(1) - 1, next_ref[0], 0.0)
    acc = jnp.zeros(cur_ref.shape[1:], F32) + dwb_ref[...]
    for kk in range(width):
        off = HALO - pad + kk
        acc = acc + buf[off:off + tt, :] * dw_ref[kk:kk + 1, :]
    mu = jnp.mean(acc, axis=-1, keepdims=True)
    xc = acc - mu
    var = jnp.mean(xc * xc, axis=-1, keepdims=True)
    y = xc * lax.rsqrt(var + EPS) * g_ref[...] + b_ref[...]
    o_ref[0] = (y * _sigmoid(y)).astype(o_ref.dtype)


def dwconv_ln_swish(u, dw, dwb, ln_g, ln_b):
    b, t, d = u.shape
    width = dw.shape[0]
    tt = _tile(t, 128, HALO)
    nh = tt // HALO
    last = t // HALO - 1
    vec = lambda bi, i: (0, 0)
    return pl.pallas_call(
        functools.partial(_dwconv_kernel, width=width),
        grid=(b, t // tt),
        in_specs=[pl.BlockSpec((1, HALO, d), lambda bi, i: (bi, jnp.maximum(i * nh - 1, 0), 0)),
                  pl.BlockSpec((1, tt, d), lambda bi, i: (bi, i, 0)),
                  pl.BlockSpec((1, HALO, d), lambda bi, i: (bi, jnp.minimum((i + 1) * nh, last), 0)),
                  pl.BlockSpec((width, d), vec),
                  pl.BlockSpec((1, d), vec), pl.BlockSpec((1, d), vec), pl.BlockSpec((1, d), vec)],
        out_specs=pl.BlockSpec((1, tt, d), lambda bi, i: (bi, i, 0)),
        out_shape=jax.ShapeDtypeStruct((b, t, d), BF16),
        scratch_shapes=[pltpu.VMEM((tt + 2 * HALO, d), F32)],
        compiler_params=_params("arbitrary", "arbitrary"),
        name="dwconv_ln_swish",
    )(u, u, u, dw, dwb.reshape(1, d), ln_g.reshape(1, d), ln_b.reshape(1, d))


def _rope_tables(seq, ctx_len):
    n_axis = QK_ROPE // 4
    t = jnp.arange(seq, dtype=jnp.int32)
    row = (t // GRID_W).astype(F32)
    col = (t % GRID_W).astype(F32)
    freqs = ROPE_THETA ** (-jnp.arange(n_axis, dtype=F32) / n_axis)
    ang = jnp.concatenate([row[:, None] * freqs, col[:, None] * freqs], axis=-1)
    ang = jnp.concatenate([ang, jnp.zeros((ctx_len, QK_ROPE // 2), F32)], axis=0)
    cos, sin = jnp.cos(ang), jnp.sin(ang)
    zeros = jnp.zeros((seq + ctx_len, LANES - QK_ROPE), F32)
    cos_t = jnp.concatenate([cos, cos, zeros], axis=-1)
    sin_t = jnp.concatenate([-sin, sin, zeros], axis=-1)
    return cos_t, sin_t


def _rope_cols(w):
    even, odd = w[..., 0::2], w[..., 1::2]
    z = jnp.zeros(w.shape[:-1] + (LANES - QK_ROPE,), w.dtype)
    return jnp.concatenate([even, odd, z, odd, even, z], axis=-1)


def _mod_tiles(m, layout, k, d):
    b, n_lat, n_ctx = layout
    mk = m[:, k * d:(k + 1) * d]
    parts = [jnp.broadcast_to(mk[:b, None, :], (b, n_lat, d))]
    if n_ctx:
        parts.append(jnp.broadcast_to(mk[b][None, None, :], (b, n_ctx, d)))
    return jnp.concatenate(parts, axis=1).reshape(b * (n_lat + n_ctx), 1, d)


def kernel(x, c, ctx, c_ctx, ada_w, ada_b, norm_g, ffn_w1, ffn_w3, ffn_w2, mla_wdq, mla_gq, mla_wuq, mla_wdkv, mla_gkv, mla_wukv, mla_wo, conv_w1, conv_b1, conv_dw, conv_dwb, conv_ln_g, conv_ln_b, conv_w2, conv_b2, final_g):
    b, s, d = x.shape
    lc = ctx.shape[1]
    depth = ada_w.shape[0]
    d_ff = ffn_w1.shape[-1]
    nh = mla_wo.shape[1] // V_DIM
    kv_lora = mla_gkv.shape[1]
    assert s % MOD_TILE == 0 and lc % MOD_TILE == 0 and b < COND_ROWS

    cond = jnp.concatenate([c, c_ctx[None, :], jnp.zeros((COND_ROWS - b - 1, d), F32)], axis=0)
    mods = ada_mods(cond, ada_w, ada_b)

    t_all = s + lc
    rows_all = (b, s // MOD_TILE, lc // MOD_TILE)
    rows_lat = (b, s // MOD_TILE, 0)

    h = jnp.concatenate([x, ctx], axis=1).reshape(b * t_all, d)
    combined = True

    def ffn(h, m, rows, i, k, f):
        xn = norm_mod(h, norm_g[i, k], _mod_tiles(m, rows, 3 * k, d), _mod_tiles(m, rows, 3 * k + 1, d))
        g = gated_up(xn, ffn_w1, (i, f), 0, ffn_w3, (i, f), 0, d_ff, "swiglu")
        return proj_residual(g, ffn_w2, (i, f), h, _mod_tiles(m, rows, 3 * k + 2, d), FFN_RES_WEIGHT)

    for i in range(depth):
        last = i == depth - 1
        mixer = i % 2
        j = i // 2
        ctx_into_mixer = (not last) or mixer == 0
        ctx_out = not last
        m = mods[i]
        if combined and not ctx_into_mixer:
            h = h.reshape(b, t_all, d)[:, :s].reshape(b * s, d)
            combined = False
        rows = rows_all if combined else rows_lat
        t_cur = t_all if combined else s

        h = ffn(h, m, rows, i, 0, 0)

        xn = norm_mod(h, norm_g[i, 1], _mod_tiles(m, rows, 3, d), _mod_tiles(m, rows, 4, d))
        gate = _mod_tiles(m, rows, 5, d)
        if mixer == 0:
            assert combined
            cos_t, sin_t = _rope_tables(s, lc)
            wkv = jnp.concatenate([mla_wdkv[j][:, :kv_lora], _rope_cols(mla_wdkv[j][:, kv_lora:])], axis=1)
            cq, ckv, kpe = mla_down(xn, mla_wdq[j], wkv, mla_gq[j], mla_gkv[j],
                                    jnp.tile(cos_t, (b, 1)), jnp.tile(sin_t, (b, 1)))
            wq3 = mla_wuq[j].reshape(-1, nh, QK_NOPE + QK_ROPE)
            wq_h = jnp.concatenate([wq3[..., :QK_NOPE], _rope_cols(wq3[..., QK_NOPE:])], axis=-1).transpose(1, 0, 2)
            wkv_h = mla_wukv[j].reshape(-1, nh, QK_NOPE + V_DIM).transpose(1, 0, 2)
            q_scale = (QK_NOPE + QK_ROPE) ** -0.5 * math.log2(math.e)
            q, k, v = mla_up(cq.reshape(b, t_all, -1), ckv.reshape(b, t_all, -1), kpe.reshape(b, t_all, -1),
                             cos_t, sin_t, wq_h, wkv_h, q_scale)
            o_l = attention(q, k, v, 0, s, 0, t_all, _tile(s, 512, MOD_TILE))
            if ctx_out:
                o_c = attention(q, k, v, s, lc, s, lc, lc)
            else:
                o_c = jnp.zeros((b, lc, nh * V_DIM), BF16)
            o = jnp.concatenate([o_l, o_c], axis=1).reshape(b * t_all, nh * V_DIM)
            h = proj_residual(o, mla_wo, (j,), h, gate, 1.0)
        else:
            assert not combined
            u = gated_up(xn, conv_w1, (j,), 0, conv_w1, (j,), 1, d, "glu",
                         bias=(conv_b1[j, :d], conv_b1[j, d:]), out_dtype=F32)
            vv = dwconv_ln_swish(u.reshape(-1, t_cur, d), conv_dw[j], conv_dwb[j], conv_ln_g[j], conv_ln_b[j])
            h = proj_residual(vv.reshape(-1, d), conv_w2, (j,), h, gate, 1.0, bias=conv_b2[j])

        if combined and not ctx_out:
            h = h.reshape(b, t_all, d)[:, :s].reshape(b * s, d)
            combined = False
            rows = rows_lat
        h = ffn(h, m, rows, i, 2, 1)

    if combined:
        h = h.reshape(b, t_all, d)[:, :s].reshape(b * s, d)
    return norm_mod(h, final_g, out_dtype=F32).reshape(b, s, d)
```

```python
import functools
import math

import numpy as np
import jax
import jax.numpy as jnp
from jax import lax
from jax.experimental import pallas as pl
from jax.experimental.pallas import tpu as pltpu

F32 = jnp.float32
BF16 = jnp.bfloat16

GRID_W = 64
QK_NOPE = 128
QK_ROPE = 64
V_DIM = 128
ROPE_THETA = 10000.0
FFN_RES_WEIGHT = 0.5
N_MOD = 9
EPS = 1e-6

LANES = 128
SUBLANES = 8
VMEM_LIMIT_BYTES = 56 * 1024 * 1024

MOD_TILE = 256
COND_ROWS = 8
HALO = 16
CAST_ROWS = 256


def _params(*sem):
    return pltpu.CompilerParams(dimension_semantics=sem, vmem_limit_bytes=VMEM_LIMIT_BYTES)


def _tile(n, pref, unit):
    if n <= pref:
        return n
    t = (pref // unit) * unit
    while t > unit and n % t:
        t -= unit
    assert n % t == 0, (n, pref, unit)
    return t


def _cast_weight(src_ref, dst_ref):
    k = src_ref.shape[0]
    rows = _tile(k, CAST_ROWS, SUBLANES)

    def body(c, carry):
        s = pl.multiple_of(c * rows, rows)
        dst_ref[pl.ds(s, rows), :] = src_ref[pl.ds(s, rows), :].astype(BF16)
        return carry

    lax.fori_loop(0, k // rows, body, 0)


def _sigmoid(x):
    return 1.0 / (1.0 + jnp.exp(-x))


def _ada_kernel(c_ref, w_ref, b_ref, o_ref):
    c = c_ref[...]
    s = (c * _sigmoid(c)).astype(BF16)
    o_ref[0] = jnp.dot(s, w_ref[0].astype(BF16), preferred_element_type=F32) + b_ref[0]


def ada_mods(cond, ada_w, ada_b):
    depth, d, n = ada_w.shape
    tn = _tile(n, 1024, LANES)
    return pl.pallas_call(
        _ada_kernel,
        grid=(depth, n // tn),
        in_specs=[
            pl.BlockSpec((COND_ROWS, d), lambda l, j: (0, 0)),
            pl.BlockSpec((1, d, tn), lambda l, j: (l, 0, j)),
            pl.BlockSpec((1, 1, tn), lambda l, j: (l, 0, j)),
        ],
        out_specs=pl.BlockSpec((1, COND_ROWS, tn), lambda l, j: (l, 0, j)),
        out_shape=jax.ShapeDtypeStruct((depth, COND_ROWS, n), F32),
        compiler_params=_params("arbitrary", "arbitrary"),
        name="ada_mods",
    )(cond, ada_w, ada_b.reshape(depth, 1, n))


def _norm_kernel(*refs, modulate, n_sub):
    if modulate:
        x_ref, g_ref, sh_ref, sc_ref, o_ref = refs
    else:
        x_ref, g_ref, o_ref = refs
    for r in range(n_sub):
        rows = slice(r * MOD_TILE, (r + 1) * MOD_TILE)
        x = x_ref[rows, :]
        y = x * lax.rsqrt(jnp.mean(x * x, axis=-1, keepdims=True) + EPS) * g_ref[...]
        if modulate:
            y = y * (1.0 + sc_ref[r]) + sh_ref[r]
        o_ref[rows, :] = y.astype(o_ref.dtype)


def norm_mod(h, g, shift=None, scale=None, out_dtype=BF16):
    m, d = h.shape
    tm = _tile(m, 2 * MOD_TILE, MOD_TILE)
    n_sub = tm // MOD_TILE
    modulate = shift is not None
    in_specs = [pl.BlockSpec((tm, d), lambda i: (i, 0)),
                pl.BlockSpec((1, d), lambda i: (0, 0))]
    args = [h, g.reshape(1, d)]
    if modulate:
        in_specs += [pl.BlockSpec((n_sub, 1, d), lambda i: (i, 0, 0))] * 2
        args += [shift, scale]
    return pl.pallas_call(
        functools.partial(_norm_kernel, modulate=modulate, n_sub=n_sub),
        grid=(m // tm,),
        in_specs=in_specs,
        out_specs=pl.BlockSpec((tm, d), lambda i: (i, 0)),
        out_shape=jax.ShapeDtypeStruct((m, d), out_dtype),
        compiler_params=_params("arbitrary"),
        name="norm_mod",
    )(*args)


def _up_kernel(*refs, act, has_bias):
    if has_bias:
        x_ref, wa_ref, wb_ref, ba_ref, bb_ref, o_ref, wa_s, wb_s = refs
    else:
        x_ref, wa_ref, wb_ref, o_ref, wa_s, wb_s = refs

    @pl.when(pl.program_id(1) == 0)
    def _():
        _cast_weight(wa_ref, wa_s)
        _cast_weight(wb_ref, wb_s)

    x = x_ref[...]
    a = jnp.dot(x, wa_s[...], preferred_element_type=F32)
    b = jnp.dot(x, wb_s[...], preferred_element_type=F32)
    if has_bias:
        a = a + ba_ref[...]
        b = b + bb_ref[...]
    if act == "swiglu":
        o = (a * _sigmoid(a)) * b
    else:
        o = a * _sigmoid(b)
    o_ref[...] = o.astype(o_ref.dtype)


def gated_up(x, wa, wa_idx, wa_col, wb, wb_idx, wb_col, n, act, bias=None, out_dtype=BF16):
    m, k = x.shape
    tm = _tile(m, 1024, MOD_TILE)
    tn = _tile(n, 512, LANES)
    nj = n // tn

    def wspec(idx, col):
        lead = (None,) * len(idx)
        return pl.BlockSpec(lead + (k, tn), lambda j, i: idx + (0, col * nj + j))

    in_specs = [pl.BlockSpec((tm, k), lambda j, i: (i, 0)), wspec(wa_idx, wa_col), wspec(wb_idx, wb_col)]
    args = [x, wa, wb]
    if bias is not None:
        ba, bb = bias
        in_specs += [pl.BlockSpec((1, tn), lambda j, i: (0, j))] * 2
        args += [ba.reshape(1, n), bb.reshape(1, n)]
    return pl.pallas_call(
        functools.partial(_up_kernel, act=act, has_bias=bias is not None),
        grid=(nj, m // tm),
        in_specs=in_specs,
        out_specs=pl.BlockSpec((tm, tn), lambda j, i: (i, j)),
        out_shape=jax.ShapeDtypeStruct((m, n), out_dtype),
        scratch_shapes=[pltpu.VMEM((k, tn), BF16), pltpu.VMEM((k, tn), BF16)],
        compiler_params=_params("arbitrary", "arbitrary"),
        name="gated_up_" + act,
    )(*args)


def _res_kernel(*refs, coef, n_sub, has_bias):
    if has_bias:
        x_ref, w_ref, b_ref, h_ref, gate_ref, o_ref, w_s = refs
    else:
        x_ref, w_ref, h_ref, gate_ref, o_ref, w_s = refs

    @pl.when(pl.program_id(1) == 0)
    def _():
        _cast_weight(w_ref, w_s)

    y = jnp.dot(x_ref[...], w_s[...], preferred_element_type=F32)
    if has_bias:
        y = y + b_ref[...]
    for r in range(n_sub):
        rows = slice(r * MOD_TILE, (r + 1) * MOD_TILE)
        o_ref[rows, :] = h_ref[rows, :] + (coef * gate_ref[r]) * y[rows, :]


def proj_residual(x, w, w_idx, h, gate, coef, bias=None):
    m, k = x.shape
    n = h.shape[1]
    pref = 1024 if k * 1024 * 14 <= VMEM_LIMIT_BYTES // 2 else 512
    tm = _tile(m, pref, MOD_TILE)
    tn = _tile(n, pref, LANES)
    n_sub = tm // MOD_TILE
    lead = (None,) * len(w_idx)
    in_specs = [pl.BlockSpec((tm, k), lambda j, i: (i, 0)),
                pl.BlockSpec(lead + (k, tn), lambda j, i: w_idx + (0, j))]
    args = [x, w]
    if bias is not None:
        in_specs.append(pl.BlockSpec((1, tn), lambda j, i: (0, j)))
        args.append(bias.reshape(1, n))
    in_specs += [pl.BlockSpec((tm, tn), lambda j, i: (i, j)),
                 pl.BlockSpec((n_sub, 1, tn), lambda j, i: (i, 0, j))]
    args += [h, gate]
    return pl.pallas_call(
        functools.partial(_res_kernel, coef=coef, n_sub=n_sub, has_bias=bias is not None),
        grid=(n // tn, m // tm),
        in_specs=in_specs,
        out_specs=pl.BlockSpec((tm, tn), lambda j, i: (i, j)),
        out_shape=jax.ShapeDtypeStruct((m, n), F32),
        scratch_shapes=[pltpu.VMEM((k, tn), BF16)],
        compiler_params=_params("arbitrary", "arbitrary"),
        name="proj_residual",
    )(*args)


def _rms(x, g):
    return x * lax.rsqrt(jnp.mean(x * x, axis=-1, keepdims=True) + EPS) * g


def _mla_down_kernel(x_ref, wq_ref, wkv_ref, gq_ref, gkv_ref, cos_ref, sin_ref,
                     cq_ref, ckv_ref, kpe_ref, wq_s, wkv_s, *, lora):
    @pl.when(pl.program_id(0) == 0)
    def _():
        _cast_weight(wq_ref, wq_s)
        _cast_weight(wkv_ref, wkv_s)

    x = x_ref[...]
    cq = jnp.dot(x, wq_s[...], preferred_element_type=F32)
    cq_ref[...] = _rms(cq, gq_ref[...]).astype(BF16)
    kv = jnp.dot(x, wkv_s[...], preferred_element_type=F32)
    ckv_ref[...] = _rms(kv[:, :lora], gkv_ref[...]).astype(BF16)
    kpe = kv[:, lora:lora + LANES] * cos_ref[...] + kv[:, lora + LANES:] * sin_ref[...]
    kpe_ref[...] = kpe.astype(BF16)


def mla_down(xn, wdq, wkv, gq, gkv, cos, sin):
    m, d = xn.shape
    q_lora = wdq.shape[1]
    kv_lora = gkv.shape[0]
    nkv = wkv.shape[1]
    tm = _tile(m, 512, MOD_TILE)
    const = lambda i: (0, 0)
    row = lambda i: (i, 0)
    return pl.pallas_call(
        functools.partial(_mla_down_kernel, lora=kv_lora),
        grid=(m // tm,),
        in_specs=[pl.BlockSpec((tm, d), row),
                  pl.BlockSpec((d, q_lora), const),
                  pl.BlockSpec((d, nkv), const),
                  pl.BlockSpec((1, q_lora), const),
                  pl.BlockSpec((1, kv_lora), const),
                  pl.BlockSpec((tm, LANES), row),
                  pl.BlockSpec((tm, LANES), row)],
        out_specs=[pl.BlockSpec((tm, q_lora), row),
                   pl.BlockSpec((tm, kv_lora), row),
                   pl.BlockSpec((tm, LANES), row)],
        out_shape=[jax.ShapeDtypeStruct((m, q_lora), BF16),
                   jax.ShapeDtypeStruct((m, kv_lora), BF16),
                   jax.ShapeDtypeStruct((m, LANES), BF16)],
        scratch_shapes=[pltpu.VMEM((d, q_lora), BF16), pltpu.VMEM((d, nkv), BF16)],
        compiler_params=_params("arbitrary"),
        name="mla_down",
    )(xn, wdq, wkv, gq.reshape(1, -1), gkv.reshape(1, -1), cos, sin)


def _mla_up_kernel(cq_ref, ckv_ref, kpe_ref, cos_ref, sin_ref, wq_ref, wkv_ref,
                   q_ref, k_ref, v_ref, *, q_scale):
    qf = jnp.dot(cq_ref[0], wq_ref[0].astype(BF16), preferred_element_type=F32)
    q_pe = qf[:, QK_NOPE:QK_NOPE + LANES] * cos_ref[...] + qf[:, QK_NOPE + LANES:] * sin_ref[...]
    q_ref[0, 0, :, :QK_NOPE] = (qf[:, :QK_NOPE] * q_scale).astype(BF16)
    q_ref[0, 0, :, QK_NOPE:] = (q_pe * q_scale).astype(BF16)
    kvf = jnp.dot(ckv_ref[0], wkv_ref[...].astype(BF16), preferred_element_type=F32)
    k_ref[0, 0, :, :QK_NOPE] = kvf[:, :QK_NOPE].astype(BF16)
    k_ref[0, 0, :, QK_NOPE:] = kpe_ref[0]
    v_ref[0, 0] = kvf[:, QK_NOPE:].astype(BF16)


def mla_up(cq, ckv, kpe, cos, sin, wq_h, wkv_h, q_scale):
    b, t, q_lora = cq.shape
    kv_lora = ckv.shape[2]
    nh = wq_h.shape[0]
    dqk = QK_NOPE + LANES
    tok = lambda bi, hi: (bi, 0, 0)
    head = lambda bi, hi: (hi, 0, 0)
    out = lambda bi, hi: (bi, hi, 0, 0)
    return pl.pallas_call(
        functools.partial(_mla_up_kernel, q_scale=q_scale),
        grid=(b, nh),
        in_specs=[pl.BlockSpec((1, t, q_lora), tok),
                  pl.BlockSpec((1, t, kv_lora), tok),
                  pl.BlockSpec((1, t, LANES), tok),
                  pl.BlockSpec((t, LANES), lambda bi, hi: (0, 0)),
                  pl.BlockSpec((t, LANES), lambda bi, hi: (0, 0)),
                  pl.BlockSpec((1, q_lora, wq_h.shape[2]), head),
                  pl.BlockSpec((kv_lora, QK_NOPE + V_DIM), lambda bi, hi: (0, hi))],
        out_specs=[pl.BlockSpec((1, 1, t, dqk), out),
                   pl.BlockSpec((1, 1, t, dqk), out),
                   pl.BlockSpec((1, 1, t, V_DIM), out)],
        out_shape=[jax.ShapeDtypeStruct((b, nh, t, dqk), BF16),
                   jax.ShapeDtypeStruct((b, nh, t, dqk), BF16),
                   jax.ShapeDtypeStruct((b, nh, t, V_DIM), BF16)],
        compiler_params=_params("arbitrary", "arbitrary"),
        name="mla_up",
    )(cq, ckv, kpe, cos, sin, wq_h, wkv_h)


ATTN_HEADS = 4


def _attn_kernel(q_ref, k_ref, v_ref, o_ref, *, n_lat, n_ctx):
    def attend(key_rows):
        for hh in range(q_ref.shape[1]):
            s = lax.dot_general(q_ref[0, hh], k_ref[0, hh, key_rows, :], (((1,), (1,)), ((), ())),
                                preferred_element_type=F32)
            p = jnp.exp2(s - jnp.max(s, axis=-1, keepdims=True))
            l = jnp.sum(p, axis=-1, keepdims=True)
            o = jnp.dot(p.astype(BF16), v_ref[0, hh, key_rows, :], preferred_element_type=F32)
            o_ref[0, :, hh * V_DIM:(hh + 1) * V_DIM] = (o / l).astype(BF16)

    is_latent = pl.program_id(2) < n_lat // MOD_TILE

    @pl.when(is_latent)
    def _():
        attend(slice(0, n_lat + n_ctx))

    @pl.when(jnp.logical_not(is_latent))
    def _():
        attend(slice(n_lat, n_lat + n_ctx))


def attention(q, k, v, n_lat):
    b, nh, t, dqk = q.shape
    tq = MOD_TILE
    hb = _tile(nh, ATTN_HEADS, 1)
    return pl.pallas_call(
        functools.partial(_attn_kernel, n_lat=n_lat, n_ctx=t - n_lat),
        grid=(b, nh // hb, t // tq),
        in_specs=[pl.BlockSpec((1, hb, tq, dqk), lambda bi, hi, i: (bi, hi, i, 0)),
                  pl.BlockSpec((1, hb, t, dqk), lambda bi, hi, i: (bi, hi, 0, 0)),
                  pl.BlockSpec((1, hb, t, V_DIM), lambda bi, hi, i: (bi, hi, 0, 0))],
        out_specs=pl.BlockSpec((1, tq, hb * V_DIM), lambda bi, hi, i: (bi, i, hi)),
        out_shape=jax.ShapeDtypeStruct((b, t, nh * V_DIM), BF16),
        compiler_params=_params("arbitrary", "arbitrary", "arbitrary"),
        name="attention",
    )(q, k, v)


CONV_TILE = 128
CONV_GROUPS = CONV_TILE // SUBLANES


def _dwconv_kernel(prev_ref, cur_ref, next_ref, dw_ref, dwb_ref, g_ref, b_ref, o_ref, buf, cv, *, width):
    i = pl.program_id(1)
    tt = cur_ref.shape[1]
    d = cur_ref.shape[2]
    pad = width // 2
    first = i == 0
    final = i == pl.num_programs(1) - 1
    for c in range(d // LANES):
        cols = slice(c * LANES, (c + 1) * LANES)
        buf[c, 0:HALO, :] = jnp.where(first, 0.0, prev_ref[0, :, cols])
        buf[c, HALO:HALO + tt, :] = cur_ref[0, :, cols]
        buf[c, HALO + tt:, :] = jnp.where(final, 0.0, next_ref[0, :, cols])
        acc = [jnp.broadcast_to(dwb_ref[:, cols], (SUBLANES, LANES)) for _ in range(CONV_GROUPS)]
        for m in range(-pad, CONV_GROUPS + pad):
            xm = buf[c, pl.ds(HALO + m, SUBLANES, stride=CONV_GROUPS), :]
            for j in range(CONV_GROUPS):
                kk = m - j + pad
                if 0 <= kk < width:
                    acc[j] = acc[j] + xm * dw_ref[kk, :, cols]
        for j in range(CONV_GROUPS):
            cv[c, pl.ds(j, SUBLANES, stride=CONV_GROUPS), :] = acc[j]
    a = jnp.concatenate([cv[c] for c in range(d // LANES)], axis=-1)
    mu = jnp.mean(a, axis=-1, keepdims=True)
    xc = a - mu
    var = jnp.mean(xc * xc, axis=-1, keepdims=True)
    y = xc * lax.rsqrt(var + EPS) * g_ref[...] + b_ref[...]
    o_ref[0] = (y * _sigmoid(y)).astype(o_ref.dtype)


def dwconv_ln_swish(u, dw, dwb, ln_g, ln_b):
    b, t, d = u.shape
    width = dw.shape[0]
    tt = CONV_TILE
    assert t % tt == 0 and width // 2 < HALO and d % LANES == 0
    nh = tt // HALO
    last = t // HALO - 1
    vec = lambda bi, i: (0, 0)
    dw8 = jnp.broadcast_to(dw[:, None, :], (width, SUBLANES, d))
    return pl.pallas_call(
        functools.partial(_dwconv_kernel, width=width),
        grid=(b, t // tt),
        in_specs=[pl.BlockSpec((1, HALO, d), lambda bi, i: (bi, jnp.maximum(i * nh - 1, 0), 0)),
                  pl.BlockSpec((1, tt, d), lambda bi, i: (bi, i, 0)),
                  pl.BlockSpec((1, HALO, d), lambda bi, i: (bi, jnp.minimum((i + 1) * nh, last), 0)),
                  pl.BlockSpec((width, SUBLANES, d), lambda bi, i: (0, 0, 0)),
                  pl.BlockSpec((1, d), vec), pl.BlockSpec((1, d), vec), pl.BlockSpec((1, d), vec)],
        out_specs=pl.BlockSpec((1, tt, d), lambda bi, i: (bi, i, 0)),
        out_shape=jax.ShapeDtypeStruct((b, t, d), BF16),
        scratch_shapes=[pltpu.VMEM((d // LANES, tt + 2 * HALO, LANES), F32),
                        pltpu.VMEM((d // LANES, tt, LANES), F32)],
        compiler_params=_params("arbitrary", "arbitrary"),
        name="dwconv_ln_swish",
    )(u, u, u, dw8, dwb.reshape(1, d), ln_g.reshape(1, d), ln_b.reshape(1, d))


def _rope_tables(seq, ctx_len):
    n_axis = QK_ROPE // 4
    t = jnp.arange(seq, dtype=jnp.int32)
    row = (t // GRID_W).astype(F32)
    col = (t % GRID_W).astype(F32)
    freqs = ROPE_THETA ** (-jnp.arange(n_axis, dtype=F32) / n_axis)
    ang = jnp.concatenate([row[:, None] * freqs, col[:, None] * freqs], axis=-1)
    ang = jnp.concatenate([ang, jnp.zeros((ctx_len, QK_ROPE // 2), F32)], axis=0)
    cos, sin = jnp.cos(ang), jnp.sin(ang)
    zeros = jnp.zeros((seq + ctx_len, LANES - QK_ROPE), F32)
    cos_t = jnp.concatenate([cos, cos, zeros], axis=-1)
    sin_t = jnp.concatenate([-sin, sin, zeros], axis=-1)
    return cos_t, sin_t


def _rope_cols(w):
    even, odd = w[..., 0::2], w[..., 1::2]
    z = jnp.zeros(w.shape[:-1] + (LANES - QK_ROPE,), w.dtype)
    return jnp.concatenate([even, odd, z, odd, even, z], axis=-1)


def _mod_tiles(m, layout, k, d):
    b, n_lat, n_ctx = layout
    mk = m[:, k * d:(k + 1) * d]
    parts = [jnp.broadcast_to(mk[:b, None, :], (b, n_lat, d))]
    if n_ctx:
        parts.append(jnp.broadcast_to(mk[b][None, None, :], (b, n_ctx, d)))
    return jnp.concatenate(parts, axis=1).reshape(b * (n_lat + n_ctx), 1, d)


def kernel(x, c, ctx, c_ctx, ada_w, ada_b, norm_g, ffn_w1, ffn_w3, ffn_w2, mla_wdq, mla_gq, mla_wuq, mla_wdkv, mla_gkv, mla_wukv, mla_wo, conv_w1, conv_b1, conv_dw, conv_dwb, conv_ln_g, conv_ln_b, conv_w2, conv_b2, final_g):
    b, s, d = x.shape
    lc = ctx.shape[1]
    depth = ada_w.shape[0]
    d_ff = ffn_w1.shape[-1]
    nh = mla_wo.shape[1] // V_DIM
    kv_lora = mla_gkv.shape[1]
    assert s % MOD_TILE == 0 and lc % MOD_TILE == 0 and b < COND_ROWS

    cond = jnp.concatenate([c, c_ctx[None, :], jnp.zeros((COND_ROWS - b - 1, d), F32)], axis=0)
    mods = ada_mods(cond, ada_w, ada_b)

    t_all = s + lc
    rows_all = (b, s // MOD_TILE, lc // MOD_TILE)
    rows_lat = (b, s // MOD_TILE, 0)

    h = jnp.concatenate([x, ctx], axis=1).reshape(b * t_all, d)
    combined = True

    def ffn(h, m, rows, i, k, f):
        xn = norm_mod(h, norm_g[i, k], _mod_tiles(m, rows, 3 * k, d), _mod_tiles(m, rows, 3 * k + 1, d))
        g = gated_up(xn, ffn_w1, (i, f), 0, ffn_w3, (i, f), 0, d_ff, "swiglu")
        return proj_residual(g, ffn_w2, (i, f), h, _mod_tiles(m, rows, 3 * k + 2, d), FFN_RES_WEIGHT)

    for i in range(depth):
        last = i == depth - 1
        mixer = i % 2
        j = i // 2
        ctx_into_mixer = (not last) or mixer == 0
        ctx_out = not last
        m = mods[i]
        if combined and not ctx_into_mixer:
            h = h.reshape(b, t_all, d)[:, :s].reshape(b * s, d)
            combined = False
        rows = rows_all if combined else rows_lat
        t_cur = t_all if combined else s

        h = ffn(h, m, rows, i, 0, 0)

        xn = norm_mod(h, norm_g[i, 1], _mod_tiles(m, rows, 3, d), _mod_tiles(m, rows, 4, d))
        gate = _mod_tiles(m, rows, 5, d)
        if mixer == 0:
            assert combined
            cos_t, sin_t = _rope_tables(s, lc)
            wkv = jnp.concatenate([mla_wdkv[j][:, :kv_lora], _rope_cols(mla_wdkv[j][:, kv_lora:])], axis=1)
            cq, ckv, kpe = mla_down(xn, mla_wdq[j], wkv, mla_gq[j], mla_gkv[j],
                                    jnp.tile(cos_t, (b, 1)), jnp.tile(sin_t, (b, 1)))
            wq3 = mla_wuq[j].reshape(-1, nh, QK_NOPE + QK_ROPE)
            wq_h = jnp.concatenate([wq3[..., :QK_NOPE], _rope_cols(wq3[..., QK_NOPE:])], axis=-1).transpose(1, 0, 2)
            wkv_h = mla_wukv[j]
            q_scale = (QK_NOPE + QK_ROPE) ** -0.5 * math.log2(math.e)
            q, k, v = mla_up(cq.reshape(b, t_all, -1), ckv.reshape(b, t_all, -1), kpe.reshape(b, t_all, -1),
                             cos_t, sin_t, wq_h, wkv_h, q_scale)
            o = attention(q, k, v, s).reshape(b * t_all, nh * V_DIM)
            h = proj_residual(o, mla_wo, (j,), h, gate, 1.0)
        else:
            assert not combined
            u = gated_up(xn, conv_w1, (j,), 0, conv_w1, (j,), 1, d, "glu",
                         bias=(conv_b1[j, :d], conv_b1[j, d:]), out_dtype=F32)
            vv = dwconv_ln_swish(u.reshape(-1, t_cur, d), conv_dw[j], conv_dwb[j], conv_ln_g[j], conv_ln_b[j])
            h = proj_residual(vv.reshape(-1, d), conv_w2, (j,), h, gate, 1.0, bias=conv_b2[j])

        if combined and not ctx_out:
            h = h.reshape(b, t_all, d)[:, :s].reshape(b * s, d)
            combined = False
            rows = rows_lat
        h = ffn(h, m, rows, i, 2, 1)

    if combined:
        h = h.reshape(b, t_all, d)[:, :s].reshape(b * s, d)
    return norm_mod(h, final_g, out_dtype=F32).reshape(b, s, d)
```

```python
import functools
import math
from typing import NamedTuple

import numpy as np
import jax
import jax.numpy as jnp
from jax import lax
from jax.experimental import pallas as pl
from jax.experimental.pallas import tpu as pltpu

F32 = jnp.float32
BF16 = jnp.bfloat16

GRID_W = 64
QK_NOPE = 128
QK_ROPE = 64
V_DIM = 128
ROPE_THETA = 10000.0
FFN_RES_WEIGHT = 0.5
N_MOD = 9
EPS = 1e-6

LANES = 128
SUBLANES = 8
VMEM_LIMIT_BYTES = 56 * 1024 * 1024

MOD_TILE = 256
COND_ROWS = 8
HALO = 16
CAST_ROWS = 256


def _params(*sem):
    return pltpu.CompilerParams(dimension_semantics=sem, vmem_limit_bytes=VMEM_LIMIT_BYTES)


def _tile(n, pref, unit):
    if n <= pref:
        return n
    t = (pref // unit) * unit
    while t > unit and n % t:
        t -= unit
    assert n % t == 0, (n, pref, unit)
    return t


def _cast_weight(src_ref, dst_ref):
    k = src_ref.shape[0]
    rows = _tile(k, CAST_ROWS, SUBLANES)

    def body(c, carry):
        s = pl.multiple_of(c * rows, rows)
        dst_ref[pl.ds(s, rows), :] = src_ref[pl.ds(s, rows), :].astype(BF16)
        return carry

    lax.fori_loop(0, k // rows, body, 0)


def _sigmoid(x):
    return 1.0 / (1.0 + jnp.exp(-x))


def _ada_kernel(c_ref, w_ref, b_ref, o_ref):
    c = c_ref[...]
    s = (c * _sigmoid(c)).astype(BF16)
    o_ref[0] = jnp.dot(s, w_ref[0].astype(BF16), preferred_element_type=F32) + b_ref[0]


def ada_mods(cond, ada_w, ada_b):
    depth, d, n = ada_w.shape
    tn = _tile(n, 1024, LANES)
    return pl.pallas_call(
        _ada_kernel,
        grid=(depth, n // tn),
        in_specs=[
            pl.BlockSpec((COND_ROWS, d), lambda l, j: (0, 0)),
            pl.BlockSpec((1, d, tn), lambda l, j: (l, 0, j)),
            pl.BlockSpec((1, 1, tn), lambda l, j: (l, 0, j)),
        ],
        out_specs=pl.BlockSpec((1, COND_ROWS, tn), lambda l, j: (l, 0, j)),
        out_shape=jax.ShapeDtypeStruct((depth, COND_ROWS, n), F32),
        compiler_params=_params("arbitrary", "arbitrary"),
        name="ada_mods",
    )(cond, ada_w, ada_b.reshape(depth, 1, n))


class Hidden(NamedTuple):
    arrays: tuple
    t_use: int

    @property
    def split(self):
        if len(self.arrays) == 1:
            return None
        return (self.arrays[0].shape[1] // MOD_TILE, self.arrays[1].shape[1] // MOD_TILE)

    @property
    def shape(self):
        return (self.arrays[0].shape[0] * self.t_use, self.arrays[0].shape[2])


def _flat(h):
    return Hidden((h.reshape(1, *h.shape),), h.shape[0])


def _hidden_tile(src, pref):
    return _tile(src.shape[0] if src.split else src.t_use, pref, MOD_TILE)


def _hidden_specs(src, tm, cols, tile_col):
    if src.split is None:
        nt = src.t_use // tm

        def hmap(*g):
            i, j = tile_col(*g)
            return (i // nt, i % nt, j)

        return [pl.BlockSpec((None, tm, cols), hmap)]
    n_lat, n_ctx = src.split
    n_sub = tm // MOD_TILE
    specs = []
    for r in range(n_sub):
        def xmap(*g, r=r):
            i, j = tile_col(*g)
            t = i * n_sub + r
            return (t // (n_lat + n_ctx), jnp.minimum(t % (n_lat + n_ctx), n_lat - 1), j)

        def cmap(*g, r=r):
            i, j = tile_col(*g)
            t = i * n_sub + r
            return (t // (n_lat + n_ctx), jnp.maximum(t % (n_lat + n_ctx) - n_lat, 0), j)

        specs += [pl.BlockSpec((None, MOD_TILE, cols), xmap), pl.BlockSpec((None, MOD_TILE, cols), cmap)]
    return specs


def _hidden_args(src, specs):
    return list(src.arrays) * (len(specs) // len(src.arrays))


def _hidden_rows(h_refs, r, row_tile, n_sub, split):
    if split is None:
        return h_refs[0][r * MOD_TILE:(r + 1) * MOD_TILE, :]
    n_lat, n_ctx = split
    is_ctx = (row_tile * n_sub + r) % (n_lat + n_ctx) >= n_lat
    return jnp.where(is_ctx, h_refs[2 * r + 1][...], h_refs[2 * r][...])


def _rms(x, g):
    return x * lax.rsqrt(jnp.mean(x * x, axis=-1, keepdims=True) + EPS) * g


def _norm_kernel(*refs, modulate, n_sub, n_h, split):
    h_refs, refs = refs[:n_h], refs[n_h:]
    if modulate:
        g_ref, sh_ref, sc_ref, o_ref = refs
    else:
        g_ref, o_ref = refs
    for r in range(n_sub):
        y = _rms(_hidden_rows(h_refs, r, pl.program_id(0), n_sub, split), g_ref[...])
        if modulate:
            y = y * (1.0 + sc_ref[r]) + sh_ref[r]
        o_ref[r * MOD_TILE:(r + 1) * MOD_TILE, :] = y.astype(o_ref.dtype)


def norm_mod(src, g, shift=None, scale=None, out_dtype=BF16):
    m, d = src.shape
    tm = _hidden_tile(src, 2 * MOD_TILE)
    n_sub = tm // MOD_TILE
    modulate = shift is not None
    h_specs = _hidden_specs(src, tm, d, lambda i: (i, 0))
    in_specs = h_specs + [pl.BlockSpec((1, d), lambda i: (0, 0))]
    args = _hidden_args(src, h_specs) + [g.reshape(1, d)]
    if modulate:
        in_specs += [pl.BlockSpec((n_sub, 1, d), lambda i: (i, 0, 0))] * 2
        args += [shift, scale]
    return pl.pallas_call(
        functools.partial(_norm_kernel, modulate=modulate, n_sub=n_sub, n_h=len(h_specs), split=src.split),
        grid=(m // tm,),
        in_specs=in_specs,
        out_specs=pl.BlockSpec((tm, d), lambda i: (i, 0)),
        out_shape=jax.ShapeDtypeStruct((m, d), out_dtype),
        compiler_params=_params("arbitrary"),
        name="norm_mod",
    )(*args)


def _up_kernel(*refs, act, has_bias):
    if has_bias:
        x_ref, wa_ref, wb_ref, ba_ref, bb_ref, o_ref, wa_s, wb_s = refs
    else:
        x_ref, wa_ref, wb_ref, o_ref, wa_s, wb_s = refs

    @pl.when(pl.program_id(1) == 0)
    def _():
        _cast_weight(wa_ref, wa_s)
        _cast_weight(wb_ref, wb_s)

    x = x_ref[...]
    a = jnp.dot(x, wa_s[...], preferred_element_type=F32)
    b = jnp.dot(x, wb_s[...], preferred_element_type=F32)
    if has_bias:
        a = a + ba_ref[...]
        b = b + bb_ref[...]
    if act == "swiglu":
        o = (a * _sigmoid(a)) * b
    else:
        o = a * _sigmoid(b)
    o_ref[...] = o.astype(o_ref.dtype)


def gated_up(x, wa, wa_idx, wa_col, wb, wb_idx, wb_col, n, act, bias=None, out_dtype=BF16):
    m, k = x.shape
    tm = _tile(m, 1024, MOD_TILE)
    tn = _tile(n, 512, LANES)
    nj = n // tn

    def wspec(idx, col):
        lead = (None,) * len(idx)
        return pl.BlockSpec(lead + (k, tn), lambda j, i: idx + (0, col * nj + j))

    in_specs = [pl.BlockSpec((tm, k), lambda j, i: (i, 0)), wspec(wa_idx, wa_col), wspec(wb_idx, wb_col)]
    args = [x, wa, wb]
    if bias is not None:
        ba, bb = bias
        in_specs += [pl.BlockSpec((1, tn), lambda j, i: (0, j))] * 2
        args += [ba.reshape(1, n), bb.reshape(1, n)]
    return pl.pallas_call(
        functools.partial(_up_kernel, act=act, has_bias=bias is not None),
        grid=(nj, m // tm),
        in_specs=in_specs,
        out_specs=pl.BlockSpec((tm, tn), lambda j, i: (i, j)),
        out_shape=jax.ShapeDtypeStruct((m, n), out_dtype),
        scratch_shapes=[pltpu.VMEM((k, tn), BF16), pltpu.VMEM((k, tn), BF16)],
        compiler_params=_params("arbitrary", "arbitrary"),
        name="gated_up_" + act,
    )(*args)


def _res_kernel(*refs, coef, n_sub, has_bias, n_h, split):
    x_ref, w_ref = refs[:2]
    refs = refs[2:]
    if has_bias:
        b_ref, refs = refs[0], refs[1:]
    h_refs, (gate_ref, o_ref, w_s) = refs[:n_h], refs[n_h:]

    @pl.when(pl.program_id(1) == 0)
    def _():
        _cast_weight(w_ref, w_s)

    y = jnp.dot(x_ref[...], w_s[...], preferred_element_type=F32)
    if has_bias:
        y = y + b_ref[...]
    for r in range(n_sub):
        rows = slice(r * MOD_TILE, (r + 1) * MOD_TILE)
        h = _hidden_rows(h_refs, r, pl.program_id(1), n_sub, split)
        o_ref[rows, :] = h + (coef * gate_ref[r]) * y[rows, :]


def proj_residual(x, w, w_idx, src, gate, coef, bias=None):
    m, k = x.shape
    n = src.shape[1]
    assert src.shape[0] == m
    pref = 1024 if k * 1024 * 14 <= VMEM_LIMIT_BYTES // 2 else 512
    tm = _hidden_tile(src, pref)
    tn = _tile(n, pref, LANES)
    n_sub = tm // MOD_TILE
    lead = (None,) * len(w_idx)
    in_specs = [pl.BlockSpec((tm, k), lambda j, i: (i, 0)),
                pl.BlockSpec(lead + (k, tn), lambda j, i: w_idx + (0, j))]
    args = [x, w]
    if bias is not None:
        in_specs.append(pl.BlockSpec((1, tn), lambda j, i: (0, j)))
        args.append(bias.reshape(1, n))
    h_specs = _hidden_specs(src, tm, tn, lambda j, i: (i, j))
    in_specs += h_specs + [pl.BlockSpec((n_sub, 1, tn), lambda j, i: (i, 0, j))]
    args += _hidden_args(src, h_specs) + [gate]
    return pl.pallas_call(
        functools.partial(_res_kernel, coef=coef, n_sub=n_sub, has_bias=bias is not None,
                          n_h=len(h_specs), split=src.split),
        grid=(n // tn, m // tm),
        in_specs=in_specs,
        out_specs=pl.BlockSpec((tm, tn), lambda j, i: (i, j)),
        out_shape=jax.ShapeDtypeStruct((m, n), F32),
        scratch_shapes=[pltpu.VMEM((k, tn), BF16)],
        compiler_params=_params("arbitrary", "arbitrary"),
        name="proj_residual",
    )(*args)


def _mla_down_kernel(*refs, lora, n_sub, n_h, split):
    h_refs, refs = refs[:n_h], refs[n_h:]
    (g_ref, sh_ref, sc_ref, wq_ref, wkv_ref, sel_ref, gq_ref, gkv_ref, cos_ref, sin_ref,
     cq_ref, ckv_ref, kpe_ref, wq_s, wkv_s, xn_s) = refs

    @pl.when(pl.program_id(0) == 0)
    def _():
        _cast_weight(wq_ref, wq_s)
        rows = _tile(wkv_ref.shape[0], CAST_ROWS, SUBLANES)

        def body(c, carry):
            r = pl.ds(pl.multiple_of(c * rows, rows), rows)
            w = wkv_ref[r, :].astype(BF16)
            wkv_s[r, :lora] = w[:, :lora]
            wkv_s[r, lora:] = jnp.dot(w[:, lora:], sel_ref[...], preferred_element_type=F32).astype(BF16)
            return carry

        lax.fori_loop(0, wkv_ref.shape[0] // rows, body, 0)

    for r in range(n_sub):
        h = _hidden_rows(h_refs, r, pl.program_id(0), n_sub, split)
        y = _rms(h, g_ref[...]) * (1.0 + sc_ref[r]) + sh_ref[r]
        xn_s[r * MOD_TILE:(r + 1) * MOD_TILE, :] = y.astype(BF16)
    x = xn_s[...]
    cq = jnp.dot(x, wq_s[...], preferred_element_type=F32)
    cq_ref[...] = _rms(cq, gq_ref[...]).astype(BF16)
    kv = jnp.dot(x, wkv_s[...], preferred_element_type=F32)
    ckv_ref[...] = _rms(kv[:, :lora], gkv_ref[...]).astype(BF16)
    kpe = kv[:, lora:lora + LANES] * cos_ref[...] + kv[:, lora + LANES:] * sin_ref[...]
    kpe_ref[...] = kpe.astype(BF16)


def mla_down(src, g, shift, scale, wdq, wdkv, gq, gkv, cos, sin):
    m, d = src.shape
    q_lora = wdq.shape[1]
    kv_lora = gkv.shape[0]
    sel = jnp.asarray(_rope_select(), BF16)
    nkv = kv_lora + sel.shape[1]
    tm = _hidden_tile(src, 2 * MOD_TILE)
    n_sub = tm // MOD_TILE
    const = lambda i: (0, 0)
    row = lambda i: (i, 0)
    mod = pl.BlockSpec((n_sub, 1, d), lambda i: (i, 0, 0))
    h_specs = _hidden_specs(src, tm, d, lambda i: (i, 0))
    return pl.pallas_call(
        functools.partial(_mla_down_kernel, lora=kv_lora, n_sub=n_sub, n_h=len(h_specs), split=src.split),
        grid=(m // tm,),
        in_specs=h_specs + [pl.BlockSpec((1, d), const), mod, mod,
                            pl.BlockSpec((d, q_lora), const),
                            pl.BlockSpec(wdkv.shape, const),
                            pl.BlockSpec(sel.shape, const),
                            pl.BlockSpec((1, q_lora), const),
                            pl.BlockSpec((1, kv_lora), const),
                            pl.BlockSpec((tm, LANES), row),
                            pl.BlockSpec((tm, LANES), row)],
        out_specs=[pl.BlockSpec((tm, q_lora), row),
                   pl.BlockSpec((tm, kv_lora), row),
                   pl.BlockSpec((tm, LANES), row)],
        out_shape=[jax.ShapeDtypeStruct((m, q_lora), BF16),
                   jax.ShapeDtypeStruct((m, kv_lora), BF16),
                   jax.ShapeDtypeStruct((m, LANES), BF16)],
        scratch_shapes=[pltpu.VMEM((d, q_lora), BF16), pltpu.VMEM((d, nkv), BF16), pltpu.VMEM((tm, d), BF16)],
        compiler_params=_params("arbitrary"),
        name="mla_down",
    )(*_hidden_args(src, h_specs), g.reshape(1, d), shift, scale,
      wdq, wdkv, sel, gq.reshape(1, -1), gkv.reshape(1, -1), cos, sin)


def _wq_layout_kernel(w_ref, sel_ref, o_ref):
    o_ref[...] = jnp.dot(w_ref[...].astype(BF16), sel_ref[...], preferred_element_type=F32).astype(BF16)


def mla_wq_layout(wuq, nh):
    q_lora = wuq.shape[0]
    dh = QK_NOPE + QK_ROPE
    rope = _rope_select()
    one = np.zeros((dh, QK_NOPE + rope.shape[1]), np.float32)
    one[:QK_NOPE, :QK_NOPE] = np.eye(QK_NOPE)
    one[QK_NOPE:, QK_NOPE:] = rope
    pair = np.zeros((2 * one.shape[0], 2 * one.shape[1]), np.float32)
    pair[:one.shape[0], :one.shape[1]] = one
    pair[one.shape[0]:, one.shape[1]:] = one
    assert nh % 2 == 0
    return pl.pallas_call(
        _wq_layout_kernel,
        grid=(nh // 2,),
        in_specs=[pl.BlockSpec((q_lora, pair.shape[0]), lambda h: (0, h)),
                  pl.BlockSpec(pair.shape, lambda h: (0, 0))],
        out_specs=pl.BlockSpec((q_lora, pair.shape[1]), lambda h: (0, h)),
        out_shape=jax.ShapeDtypeStruct((q_lora, nh * one.shape[1]), BF16),
        compiler_params=_params("arbitrary"),
        name="mla_wq_layout",
    )(wuq, jnp.asarray(pair, BF16))


def _mla_up_kernel(cq_ref, ckv_ref, kpe_ref, cos_ref, sin_ref, wq_ref, wkv_ref,
                   q_ref, k_ref, v_ref, *, q_scale):
    qf = jnp.dot(cq_ref[0], wq_ref[...], preferred_element_type=F32)
    q_pe = qf[:, QK_NOPE:QK_NOPE + LANES] * cos_ref[...] + qf[:, QK_NOPE + LANES:] * sin_ref[...]
    q_ref[0, 0, :, :QK_NOPE] = (qf[:, :QK_NOPE] * q_scale).astype(BF16)
    q_ref[0, 0, :, QK_NOPE:] = (q_pe * q_scale).astype(BF16)
    kvf = jnp.dot(ckv_ref[0], wkv_ref[...].astype(BF16), preferred_element_type=F32)
    k_ref[0, 0, :, :QK_NOPE] = kvf[:, :QK_NOPE].astype(BF16)
    k_ref[0, 0, :, QK_NOPE:] = kpe_ref[0]
    v_ref[0, 0] = kvf[:, QK_NOPE:].astype(BF16)


def mla_up(cq, ckv, kpe, cos, sin, wq_h, wkv_h, q_scale):
    b, t, q_lora = cq.shape
    kv_lora = ckv.shape[2]
    nh = wq_h.shape[1] // (QK_NOPE + 2 * LANES)
    dqk = QK_NOPE + LANES
    tok = lambda bi, hi: (bi, 0, 0)
    out = lambda bi, hi: (bi, hi, 0, 0)
    return pl.pallas_call(
        functools.partial(_mla_up_kernel, q_scale=q_scale),
        grid=(b, nh),
        in_specs=[pl.BlockSpec((1, t, q_lora), tok),
                  pl.BlockSpec((1, t, kv_lora), tok),
                  pl.BlockSpec((1, t, LANES), tok),
                  pl.BlockSpec((t, LANES), lambda bi, hi: (0, 0)),
                  pl.BlockSpec((t, LANES), lambda bi, hi: (0, 0)),
                  pl.BlockSpec((q_lora, QK_NOPE + 2 * LANES), lambda bi, hi: (0, hi)),
                  pl.BlockSpec((kv_lora, QK_NOPE + V_DIM), lambda bi, hi: (0, hi))],
        out_specs=[pl.BlockSpec((1, 1, t, dqk), out),
                   pl.BlockSpec((1, 1, t, dqk), out),
                   pl.BlockSpec((1, 1, t, V_DIM), out)],
        out_shape=[jax.ShapeDtypeStruct((b, nh, t, dqk), BF16),
                   jax.ShapeDtypeStruct((b, nh, t, dqk), BF16),
                   jax.ShapeDtypeStruct((b, nh, t, V_DIM), BF16)],
        compiler_params=_params("arbitrary", "arbitrary"),
        name="mla_up",
    )(cq, ckv, kpe, cos, sin, wq_h, wkv_h)


ATTN_HEADS = 4
ATTN_TILE = 512
ATTN_ROWS = 256


def _attn_kernel(q_ref, k_ref, v_ref, o_ref, *, n_lat, n_ctx):
    def attend(n_rows, key_rows):
        for hh in range(q_ref.shape[1]):
            k = k_ref[0, hh, key_rows, :]
            v = v_ref[0, hh, key_rows, :]
            for r in range(n_rows // ATTN_ROWS):
                rows = slice(r * ATTN_ROWS, (r + 1) * ATTN_ROWS)
                s = lax.dot_general(q_ref[0, hh, rows, :], k, (((1,), (1,)), ((), ())),
                                    preferred_element_type=F32)
                p = jnp.exp2(s - jnp.max(s, axis=-1, keepdims=True))
                l = jnp.sum(p, axis=-1, keepdims=True)
                o = jnp.dot(p.astype(BF16), v, preferred_element_type=F32)
                o_ref[0, rows, hh * V_DIM:(hh + 1) * V_DIM] = (o / l).astype(BF16)

    is_latent = pl.program_id(2) < n_lat // ATTN_TILE

    @pl.when(is_latent)
    def _():
        attend(ATTN_TILE, slice(0, n_lat + n_ctx))

    @pl.when(jnp.logical_not(is_latent))
    def _():
        attend(n_ctx, slice(n_lat, n_lat + n_ctx))


def attention(q, k, v, n_lat):
    b, nh, t, dqk = q.shape
    n_ctx = t - n_lat
    assert n_lat % ATTN_TILE == 0 and 0 < n_ctx <= ATTN_TILE and n_ctx % ATTN_ROWS == 0
    hb = _tile(nh, ATTN_HEADS, 1)
    return pl.pallas_call(
        functools.partial(_attn_kernel, n_lat=n_lat, n_ctx=n_ctx),
        grid=(b, nh // hb, n_lat // ATTN_TILE + 1),
        in_specs=[pl.BlockSpec((1, hb, ATTN_TILE, dqk), lambda bi, hi, i: (bi, hi, i, 0)),
                  pl.BlockSpec((1, hb, t, dqk), lambda bi, hi, i: (bi, hi, 0, 0)),
                  pl.BlockSpec((1, hb, t, V_DIM), lambda bi, hi, i: (bi, hi, 0, 0))],
        out_specs=pl.BlockSpec((1, ATTN_TILE, hb * V_DIM), lambda bi, hi, i: (bi, i, hi)),
        out_shape=jax.ShapeDtypeStruct((b, t, nh * V_DIM), BF16),
        compiler_params=_params("arbitrary", "arbitrary", "arbitrary"),
        name="attention",
    )(q, k, v)


CONV_TILE = 128
CONV_GROUPS = CONV_TILE // SUBLANES


def _dwconv_kernel(prev_ref, cur_ref, next_ref, dw_ref, dwb_ref, g_ref, b_ref, o_ref, buf, cv, *, width):
    i = pl.program_id(1)
    tt = cur_ref.shape[1]
    d = cur_ref.shape[2]
    pad = width // 2
    first = i == 0
    final = i == pl.num_programs(1) - 1
    for c in range(d // LANES):
        cols = slice(c * LANES, (c + 1) * LANES)
        buf[c, 0:HALO, :] = jnp.where(first, 0.0, prev_ref[0, :, cols])
        buf[c, HALO:HALO + tt, :] = cur_ref[0, :, cols]
        buf[c, HALO + tt:, :] = jnp.where(final, 0.0, next_ref[0, :, cols])
        acc = [jnp.broadcast_to(dwb_ref[:, cols], (SUBLANES, LANES)) for _ in range(CONV_GROUPS)]
        for m in range(-pad, CONV_GROUPS + pad):
            xm = buf[c, pl.ds(HALO + m, SUBLANES, stride=CONV_GROUPS), :]
            for j in range(CONV_GROUPS):
                kk = m - j + pad
                if 0 <= kk < width:
                    acc[j] = acc[j] + xm * dw_ref[kk, :, cols]
        for j in range(CONV_GROUPS):
            cv[c, pl.ds(j, SUBLANES, stride=CONV_GROUPS), :] = acc[j]
    a = jnp.concatenate([cv[c] for c in range(d // LANES)], axis=-1)
    mu = jnp.mean(a, axis=-1, keepdims=True)
    xc = a - mu
    var = jnp.mean(xc * xc, axis=-1, keepdims=True)
    y = xc * lax.rsqrt(var + EPS) * g_ref[...] + b_ref[...]
    o_ref[0] = (y * _sigmoid(y)).astype(o_ref.dtype)


def dwconv_ln_swish(u, dw, dwb, ln_g, ln_b):
    b, t, d = u.shape
    width = dw.shape[0]
    tt = CONV_TILE
    assert t % tt == 0 and width // 2 < HALO and d % LANES == 0
    nh = tt // HALO
    last = t // HALO - 1
    vec = lambda bi, i: (0, 0)
    dw8 = jnp.broadcast_to(dw[:, None, :], (width, SUBLANES, d))
    return pl.pallas_call(
        functools.partial(_dwconv_kernel, width=width),
        grid=(b, t // tt),
        in_specs=[pl.BlockSpec((1, HALO, d), lambda bi, i: (bi, jnp.maximum(i * nh - 1, 0), 0)),
                  pl.BlockSpec((1, tt, d), lambda bi, i: (bi, i, 0)),
                  pl.BlockSpec((1, HALO, d), lambda bi, i: (bi, jnp.minimum((i + 1) * nh, last), 0)),
                  pl.BlockSpec((width, SUBLANES, d), lambda bi, i: (0, 0, 0)),
                  pl.BlockSpec((1, d), vec), pl.BlockSpec((1, d), vec), pl.BlockSpec((1, d), vec)],
        out_specs=pl.BlockSpec((1, tt, d), lambda bi, i: (bi, i, 0)),
        out_shape=jax.ShapeDtypeStruct((b, t, d), BF16),
        scratch_shapes=[pltpu.VMEM((d // LANES, tt + 2 * HALO, LANES), F32),
                        pltpu.VMEM((d // LANES, tt, LANES), F32)],
        compiler_params=_params("arbitrary", "arbitrary"),
        name="dwconv_ln_swish",
    )(u, u, u, dw8, dwb.reshape(1, d), ln_g.reshape(1, d), ln_b.reshape(1, d))


def _rope_tables(seq, ctx_len):
    n_axis = QK_ROPE // 4
    t = jnp.arange(seq, dtype=jnp.int32)
    row = (t // GRID_W).astype(F32)
    col = (t % GRID_W).astype(F32)
    freqs = ROPE_THETA ** (-jnp.arange(n_axis, dtype=F32) / n_axis)
    ang = jnp.concatenate([row[:, None] * freqs, col[:, None] * freqs], axis=-1)
    ang = jnp.concatenate([ang, jnp.zeros((ctx_len, QK_ROPE // 2), F32)], axis=0)
    cos, sin = jnp.cos(ang), jnp.sin(ang)
    zeros = jnp.zeros((seq + ctx_len, LANES - QK_ROPE), F32)
    cos_t = jnp.concatenate([cos, cos, zeros], axis=-1)
    sin_t = jnp.concatenate([-sin, sin, zeros], axis=-1)
    return cos_t, sin_t


def _rope_select():
    half = QK_ROPE // 2
    sel = np.zeros((QK_ROPE, 2 * LANES), np.float32)
    for p in range(half):
        sel[2 * p, p] = sel[2 * p + 1, half + p] = 1.0
        sel[2 * p + 1, LANES + p] = sel[2 * p, LANES + half + p] = 1.0
    return sel


def _mod_tiles(m, layout, k, d):
    b, n_lat, n_ctx = layout
    mk = m[:, k * d:(k + 1) * d]
    parts = [jnp.broadcast_to(mk[:b, None, :], (b, n_lat, d))]
    if n_ctx:
        parts.append(jnp.broadcast_to(mk[b][None, None, :], (b, n_ctx, d)))
    return jnp.concatenate(parts, axis=1).reshape(b * (n_lat + n_ctx), 1, d)


def kernel(x, c, ctx, c_ctx, ada_w, ada_b, norm_g, ffn_w1, ffn_w3, ffn_w2, mla_wdq, mla_gq, mla_wuq, mla_wdkv, mla_gkv, mla_wukv, mla_wo, conv_w1, conv_b1, conv_dw, conv_dwb, conv_ln_g, conv_ln_b, conv_w2, conv_b2, final_g):
    b, s, d = x.shape
    lc = ctx.shape[1]
    depth = ada_w.shape[0]
    d_ff = ffn_w1.shape[-1]
    nh = mla_wo.shape[1] // V_DIM
    assert s % MOD_TILE == 0 and lc % MOD_TILE == 0 and b < COND_ROWS

    cond = jnp.concatenate([c, c_ctx[None, :], jnp.zeros((COND_ROWS - b - 1, d), F32)], axis=0)
    mods = ada_mods(cond, ada_w, ada_b)

    t_all = s + lc
    rows_all = (b, s // MOD_TILE, lc // MOD_TILE)
    rows_lat = (b, s // MOD_TILE, 0)

    src = Hidden((x, ctx), t_all)
    combined = True

    def ffn(src, m, rows, i, k, f):
        xn = norm_mod(src, norm_g[i, k], _mod_tiles(m, rows, 3 * k, d), _mod_tiles(m, rows, 3 * k + 1, d))
        g = gated_up(xn, ffn_w1, (i, f), 0, ffn_w3, (i, f), 0, d_ff, "swiglu")
        return _flat(proj_residual(g, ffn_w2, (i, f), src, _mod_tiles(m, rows, 3 * k + 2, d), FFN_RES_WEIGHT))

    def latent_only(src):
        if src.split:
            return Hidden((src.arrays[0],), s)
        return Hidden((src.arrays[0].reshape(b, t_all, d),), s)

    for i in range(depth):
        last = i == depth - 1
        mixer = i % 2
        j = i // 2
        ctx_into_mixer = (not last) or mixer == 0
        ctx_out = not last
        m = mods[i]
        if combined and not ctx_into_mixer:
            src, combined = latent_only(src), False
        rows = rows_all if combined else rows_lat

        src = ffn(src, m, rows, i, 0, 0)

        shift, scale, gate = (_mod_tiles(m, rows, k, d) for k in (3, 4, 5))
        if mixer == 0:
            assert combined
            cos_t, sin_t = _rope_tables(s, lc)
            cq, ckv, kpe = mla_down(src, norm_g[i, 1], shift, scale, mla_wdq[j], mla_wdkv[j], mla_gq[j], mla_gkv[j],
                                    jnp.tile(cos_t, (b, 1)), jnp.tile(sin_t, (b, 1)))
            wq_h = mla_wq_layout(mla_wuq[j], nh)
            wkv_h = mla_wukv[j]
            q_scale = (QK_NOPE + QK_ROPE) ** -0.5 * math.log2(math.e)
            q, k, v = mla_up(cq.reshape(b, t_all, -1), ckv.reshape(b, t_all, -1), kpe.reshape(b, t_all, -1),
                             cos_t, sin_t, wq_h, wkv_h, q_scale)
            o = attention(q, k, v, s).reshape(b * t_all, nh * V_DIM)
            src = _flat(proj_residual(o, mla_wo, (j,), src, gate, 1.0))
        else:
            assert not combined
            xn = norm_mod(src, norm_g[i, 1], shift, scale)
            u = gated_up(xn, conv_w1, (j,), 0, conv_w1, (j,), 1, d, "glu",
                         bias=(conv_b1[j, :d], conv_b1[j, d:]), out_dtype=F32)
            vv = dwconv_ln_swish(u.reshape(b, s, d), conv_dw[j], conv_dwb[j], conv_ln_g[j], conv_ln_b[j])
            src = _flat(proj_residual(vv.reshape(-1, d), conv_w2, (j,), src, gate, 1.0, bias=conv_b2[j]))

        if combined and not ctx_out:
            src, combined = latent_only(src), False
            rows = rows_lat
        src = ffn(src, m, rows, i, 2, 1)

    if combined:
        src = latent_only(src)
    return norm_mod(src, final_g, out_dtype=F32).reshape(b, s, d)
```

```python
import functools
import math
from typing import NamedTuple

import numpy as np
import jax
import jax.numpy as jnp
from jax import lax
from jax.experimental import pallas as pl
from jax.experimental.pallas import tpu as pltpu

F32 = jnp.float32
BF16 = jnp.bfloat16

GRID_W = 64
QK_NOPE = 128
QK_ROPE = 64
V_DIM = 128
ROPE_THETA = 10000.0
FFN_RES_WEIGHT = 0.5
N_MOD = 9
EPS = 1e-6

LANES = 128
SUBLANES = 8
VMEM_LIMIT_BYTES = 58 * 1024 * 1024

MOD_TILE = 256
COND_ROWS = 8
HALO = 16
CAST_ROWS = 256


def _params(*sem):
    return pltpu.CompilerParams(dimension_semantics=sem, vmem_limit_bytes=VMEM_LIMIT_BYTES)


def _tile(n, pref, unit):
    if n <= pref:
        return n
    t = (pref // unit) * unit
    while t > unit and n % t:
        t -= unit
    assert n % t == 0, (n, pref, unit)
    return t


def _cast_weight(src_ref, dst_ref):
    k = src_ref.shape[0]
    rows = _tile(k, CAST_ROWS, SUBLANES)

    def body(c, carry):
        s = pl.multiple_of(c * rows, rows)
        dst_ref[pl.ds(s, rows), :] = src_ref[pl.ds(s, rows), :].astype(BF16)
        return carry

    lax.fori_loop(0, k // rows, body, 0)


def _sigmoid(x):
    return 1.0 / (1.0 + jnp.exp(-x))


def _ada_kernel(c_ref, w_ref, b_ref, o_ref):
    c = c_ref[...]
    s = (c * _sigmoid(c)).astype(BF16)
    o_ref[0] = jnp.dot(s, w_ref[0].astype(BF16), preferred_element_type=F32) + b_ref[0]


def ada_mods(cond, ada_w, ada_b):
    depth, d, n = ada_w.shape
    tn = _tile(n, 1024, LANES)
    return pl.pallas_call(
        _ada_kernel,
        grid=(depth, n // tn),
        in_specs=[
            pl.BlockSpec((COND_ROWS, d), lambda l, j: (0, 0)),
            pl.BlockSpec((1, d, tn), lambda l, j: (l, 0, j)),
            pl.BlockSpec((1, 1, tn), lambda l, j: (l, 0, j)),
        ],
        out_specs=pl.BlockSpec((1, COND_ROWS, tn), lambda l, j: (l, 0, j)),
        out_shape=jax.ShapeDtypeStruct((depth, COND_ROWS, n), F32),
        compiler_params=_params("arbitrary", "arbitrary"),
        name="ada_mods",
    )(cond, ada_w, ada_b.reshape(depth, 1, n))


class Hidden(NamedTuple):
    arrays: tuple
    t_use: int

    @property
    def split(self):
        if len(self.arrays) == 1:
            return None
        return (self.arrays[0].shape[1] // MOD_TILE, self.arrays[1].shape[1] // MOD_TILE)

    @property
    def shape(self):
        return (self.arrays[0].shape[0] * self.t_use, self.arrays[0].shape[2])


def _flat(h):
    return Hidden((h.reshape(1, *h.shape),), h.shape[0])


def _hidden_tile(src, pref):
    return _tile(src.shape[0] if src.split else src.t_use, pref, MOD_TILE)


def _hidden_specs(src, tm, cols, tile_col):
    if src.split is None:
        nt = src.t_use // tm

        def hmap(*g):
            i, j = tile_col(*g)
            return (i // nt, i % nt, j)

        return [pl.BlockSpec((None, tm, cols), hmap)]
    n_lat, n_ctx = src.split
    n_sub = tm // MOD_TILE
    specs = []
    for r in range(n_sub):
        def xmap(*g, r=r):
            i, j = tile_col(*g)
            t = i * n_sub + r
            return (t // (n_lat + n_ctx), jnp.minimum(t % (n_lat + n_ctx), n_lat - 1), j)

        def cmap(*g, r=r):
            i, j = tile_col(*g)
            t = i * n_sub + r
            return (t // (n_lat + n_ctx), jnp.maximum(t % (n_lat + n_ctx) - n_lat, 0), j)

        specs += [pl.BlockSpec((None, MOD_TILE, cols), xmap), pl.BlockSpec((None, MOD_TILE, cols), cmap)]
    return specs


def _hidden_args(src, specs):
    return list(src.arrays) * (len(specs) // len(src.arrays))


def _hidden_rows(h_refs, r, row_tile, n_sub, split):
    if split is None:
        return h_refs[0][r * MOD_TILE:(r + 1) * MOD_TILE, :]
    n_lat, n_ctx = split
    is_ctx = (row_tile * n_sub + r) % (n_lat + n_ctx) >= n_lat
    return jnp.where(is_ctx, h_refs[2 * r + 1][...], h_refs[2 * r][...])


def _rms(x, g):
    return x * lax.rsqrt(jnp.mean(x * x, axis=-1, keepdims=True) + EPS) * g


def _norm_kernel(*refs, modulate, n_sub, n_h, split):
    h_refs, refs = refs[:n_h], refs[n_h:]
    if modulate:
        g_ref, sh_ref, sc_ref, o_ref = refs
    else:
        g_ref, o_ref = refs
    for r in range(n_sub):
        y = _rms(_hidden_rows(h_refs, r, pl.program_id(0), n_sub, split), g_ref[...])
        if modulate:
            y = y * (1.0 + sc_ref[r]) + sh_ref[r]
        o_ref[r * MOD_TILE:(r + 1) * MOD_TILE, :] = y.astype(o_ref.dtype)


def norm_mod(src, g, shift=None, scale=None, out_dtype=BF16):
    m, d = src.shape
    tm = _hidden_tile(src, 2 * MOD_TILE)
    n_sub = tm // MOD_TILE
    modulate = shift is not None
    h_specs = _hidden_specs(src, tm, d, lambda i: (i, 0))
    in_specs = h_specs + [pl.BlockSpec((1, d), lambda i: (0, 0))]
    args = _hidden_args(src, h_specs) + [g.reshape(1, d)]
    if modulate:
        in_specs += [pl.BlockSpec((n_sub, 1, d), lambda i: (i, 0, 0))] * 2
        args += [shift, scale]
    return pl.pallas_call(
        functools.partial(_norm_kernel, modulate=modulate, n_sub=n_sub, n_h=len(h_specs), split=src.split),
        grid=(m // tm,),
        in_specs=in_specs,
        out_specs=pl.BlockSpec((tm, d), lambda i: (i, 0)),
        out_shape=jax.ShapeDtypeStruct((m, d), out_dtype),
        compiler_params=_params("arbitrary"),
        name="norm_mod",
    )(*args)


def _up_kernel(*refs, act, has_bias):
    if has_bias:
        x_ref, wa_ref, wb_ref, ba_ref, bb_ref, o_ref, wa_s, wb_s = refs
    else:
        x_ref, wa_ref, wb_ref, o_ref, wa_s, wb_s = refs

    @pl.when(pl.program_id(1) == 0)
    def _():
        _cast_weight(wa_ref, wa_s)
        _cast_weight(wb_ref, wb_s)

    x = x_ref[...]
    a = jnp.dot(x, wa_s[...], preferred_element_type=F32)
    b = jnp.dot(x, wb_s[...], preferred_element_type=F32)
    if has_bias:
        a = a + ba_ref[...]
        b = b + bb_ref[...]
    if act == "swiglu":
        o = (a * _sigmoid(a)) * b
    else:
        o = a * _sigmoid(b)
    o_ref[...] = o.astype(o_ref.dtype)


def gated_up(x, wa, wa_idx, wa_col, wb, wb_idx, wb_col, n, act, bias=None, out_dtype=BF16):
    m, k = x.shape
    tm = _tile(m, 1024, MOD_TILE)
    tn = _tile(n, 512, LANES)
    nj = n // tn

    def wspec(idx, col):
        lead = (None,) * len(idx)
        return pl.BlockSpec(lead + (k, tn), lambda j, i: idx + (0, col * nj + j))

    in_specs = [pl.BlockSpec((tm, k), lambda j, i: (i, 0)), wspec(wa_idx, wa_col), wspec(wb_idx, wb_col)]
    args = [x, wa, wb]
    if bias is not None:
        ba, bb = bias
        in_specs += [pl.BlockSpec((1, tn), lambda j, i: (0, j))] * 2
        args += [ba.reshape(1, n), bb.reshape(1, n)]
    return pl.pallas_call(
        functools.partial(_up_kernel, act=act, has_bias=bias is not None),
        grid=(nj, m // tm),
        in_specs=in_specs,
        out_specs=pl.BlockSpec((tm, tn), lambda j, i: (i, j)),
        out_shape=jax.ShapeDtypeStruct((m, n), out_dtype),
        scratch_shapes=[pltpu.VMEM((k, tn), BF16), pltpu.VMEM((k, tn), BF16)],
        compiler_params=_params("arbitrary", "arbitrary"),
        name="gated_up_" + act,
    )(*args)


def _res_kernel(*refs, coef, n_sub, has_bias, n_h, split):
    x_ref, w_ref = refs[:2]
    refs = refs[2:]
    if has_bias:
        b_ref, refs = refs[0], refs[1:]
    h_refs, (gate_ref, o_ref, w_s) = refs[:n_h], refs[n_h:]

    @pl.when(pl.program_id(1) == 0)
    def _():
        _cast_weight(w_ref, w_s)

    y = jnp.dot(x_ref[...], w_s[...], preferred_element_type=F32)
    if has_bias:
        y = y + b_ref[...]
    for r in range(n_sub):
        rows = slice(r * MOD_TILE, (r + 1) * MOD_TILE)
        h = _hidden_rows(h_refs, r, pl.program_id(1), n_sub, split)
        o_ref[rows, :] = h + (coef * gate_ref[r]) * y[rows, :]


def proj_residual(x, w, w_idx, src, gate, coef, bias=None):
    m, k = x.shape
    n = src.shape[1]
    assert src.shape[0] == m
    tn = _tile(n, 1024, LANES)
    if k * tn * 14 <= VMEM_LIMIT_BYTES // 2:
        tm, w_mode = _hidden_tile(src, 1024), {}
    else:
        tm, w_mode = _hidden_tile(src, 512), {"pipeline_mode": pl.Buffered(1)}
    n_sub = tm // MOD_TILE
    lead = (None,) * len(w_idx)
    in_specs = [pl.BlockSpec((tm, k), lambda j, i: (i, 0)),
                pl.BlockSpec(lead + (k, tn), lambda j, i: w_idx + (0, j), **w_mode)]
    args = [x, w]
    if bias is not None:
        in_specs.append(pl.BlockSpec((1, tn), lambda j, i: (0, j)))
        args.append(bias.reshape(1, n))
    h_specs = _hidden_specs(src, tm, tn, lambda j, i: (i, j))
    in_specs += h_specs + [pl.BlockSpec((n_sub, 1, tn), lambda j, i: (i, 0, j))]
    args += _hidden_args(src, h_specs) + [gate]
    return pl.pallas_call(
        functools.partial(_res_kernel, coef=coef, n_sub=n_sub, has_bias=bias is not None,
                          n_h=len(h_specs), split=src.split),
        grid=(n // tn, m // tm),
        in_specs=in_specs,
        out_specs=pl.BlockSpec((tm, tn), lambda j, i: (i, j)),
        out_shape=jax.ShapeDtypeStruct((m, n), F32),
        scratch_shapes=[pltpu.VMEM((k, tn), BF16)],
        compiler_params=_params("arbitrary", "arbitrary"),
        name="proj_residual",
    )(*args)


def _mla_down_kernel(*refs, lora, n_sub, n_h, split):
    h_refs, refs = refs[:n_h], refs[n_h:]
    (g_ref, sh_ref, sc_ref, wq_ref, wkv_ref, sel_ref, gq_ref, gkv_ref, cos_ref, sin_ref,
     cq_ref, ckv_ref, kpe_ref, wq_s, wkv_s, xn_s) = refs

    @pl.when(pl.program_id(0) == 0)
    def _():
        _cast_weight(wq_ref, wq_s)
        rows = _tile(wkv_ref.shape[0], CAST_ROWS, SUBLANES)

        def body(c, carry):
            r = pl.ds(pl.multiple_of(c * rows, rows), rows)
            w = wkv_ref[r, :].astype(BF16)
            wkv_s[r, :lora] = w[:, :lora]
            wkv_s[r, lora:] = jnp.dot(w[:, lora:], sel_ref[...], preferred_element_type=F32).astype(BF16)
            return carry

        lax.fori_loop(0, wkv_ref.shape[0] // rows, body, 0)

    for r in range(n_sub):
        h = _hidden_rows(h_refs, r, pl.program_id(0), n_sub, split)
        y = _rms(h, g_ref[...]) * (1.0 + sc_ref[r]) + sh_ref[r]
        xn_s[r * MOD_TILE:(r + 1) * MOD_TILE, :] = y.astype(BF16)
    x = xn_s[...]
    cq = jnp.dot(x, wq_s[...], preferred_element_type=F32)
    cq_ref[...] = _rms(cq, gq_ref[...]).astype(BF16)
    kv = jnp.dot(x, wkv_s[...], preferred_element_type=F32)
    ckv_ref[...] = _rms(kv[:, :lora], gkv_ref[...]).astype(BF16)
    kpe_ref[...] = _rope(kv[:, lora:], cos_ref[...], sin_ref[...]).astype(BF16)


def mla_down(src, g, shift, scale, wdq, wdkv, gq, gkv, cos, sin):
    m, d = src.shape
    q_lora = wdq.shape[1]
    kv_lora = gkv.shape[0]
    sel = jnp.asarray(_rope_select(), BF16)
    nkv = kv_lora + sel.shape[1]
    tm = _hidden_tile(src, 2 * MOD_TILE)
    n_sub = tm // MOD_TILE
    const = lambda i: (0, 0)
    row = lambda i: (i, 0)
    mod = pl.BlockSpec((n_sub, 1, d), lambda i: (i, 0, 0))
    h_specs = _hidden_specs(src, tm, d, lambda i: (i, 0))
    return pl.pallas_call(
        functools.partial(_mla_down_kernel, lora=kv_lora, n_sub=n_sub, n_h=len(h_specs), split=src.split),
        grid=(m // tm,),
        in_specs=h_specs + [pl.BlockSpec((1, d), const), mod, mod,
                            pl.BlockSpec((d, q_lora), const),
                            pl.BlockSpec(wdkv.shape, const),
                            pl.BlockSpec(sel.shape, const),
                            pl.BlockSpec((1, q_lora), const),
                            pl.BlockSpec((1, kv_lora), const),
                            pl.BlockSpec((tm, LANES), row),
                            pl.BlockSpec((tm, LANES), row)],
        out_specs=[pl.BlockSpec((tm, q_lora), row),
                   pl.BlockSpec((tm, kv_lora), row),
                   pl.BlockSpec((tm, LANES), row)],
        out_shape=[jax.ShapeDtypeStruct((m, q_lora), BF16),
                   jax.ShapeDtypeStruct((m, kv_lora), BF16),
                   jax.ShapeDtypeStruct((m, LANES), BF16)],
        scratch_shapes=[pltpu.VMEM((d, q_lora), BF16), pltpu.VMEM((d, nkv), BF16), pltpu.VMEM((tm, d), BF16)],
        compiler_params=_params("arbitrary"),
        name="mla_down",
    )(*_hidden_args(src, h_specs), g.reshape(1, d), shift, scale,
      wdq, wdkv, sel, gq.reshape(1, -1), gkv.reshape(1, -1), cos, sin)


def _wq_layout_kernel(w_ref, sel_ref, o_ref):
    o_ref[...] = jnp.dot(w_ref[...].astype(BF16), sel_ref[...], preferred_element_type=F32).astype(BF16)


def mla_wq_layout(wuq, nh):
    q_lora = wuq.shape[0]
    dh = QK_NOPE + QK_ROPE
    rope = _rope_select()
    one = np.zeros((dh, QK_NOPE + rope.shape[1]), np.float32)
    one[:QK_NOPE, :QK_NOPE] = np.eye(QK_NOPE)
    one[QK_NOPE:, QK_NOPE:] = rope
    pair = np.zeros((2 * one.shape[0], 2 * one.shape[1]), np.float32)
    pair[:one.shape[0], :one.shape[1]] = one
    pair[one.shape[0]:, one.shape[1]:] = one
    assert nh % 2 == 0
    return pl.pallas_call(
        _wq_layout_kernel,
        grid=(nh // 2,),
        in_specs=[pl.BlockSpec((q_lora, pair.shape[0]), lambda h: (0, h)),
                  pl.BlockSpec(pair.shape, lambda h: (0, 0))],
        out_specs=pl.BlockSpec((q_lora, pair.shape[1]), lambda h: (0, h)),
        out_shape=jax.ShapeDtypeStruct((q_lora, nh * one.shape[1]), BF16),
        compiler_params=_params("arbitrary"),
        name="mla_wq_layout",
    )(wuq, jnp.asarray(pair, BF16))


MLA_UP_HEADS = 2


def _mla_up_kernel(cq_ref, ckv_ref, kpe_ref, cos_ref, sin_ref, wq_ref, wkv_ref,
                   q_ref, k_ref, v_ref, *, q_scale):
    dq, dkv = QK_NOPE + LANES, QK_NOPE + V_DIM
    qf = jnp.dot(cq_ref[0], wq_ref[...], preferred_element_type=F32)
    kvf = jnp.dot(ckv_ref[0], wkv_ref[...].astype(BF16), preferred_element_type=F32)
    for hh in range(q_ref.shape[1]):
        q_pe = _rope(qf[:, hh * dq + QK_NOPE:(hh + 1) * dq], cos_ref[...], sin_ref[...])
        q_ref[0, hh, :, :QK_NOPE] = (qf[:, hh * dq:hh * dq + QK_NOPE] * q_scale).astype(BF16)
        q_ref[0, hh, :, QK_NOPE:] = (q_pe * q_scale).astype(BF16)
        k_ref[0, hh, :, :QK_NOPE] = kvf[:, hh * dkv:hh * dkv + QK_NOPE].astype(BF16)
        k_ref[0, hh, :, QK_NOPE:] = kpe_ref[0]
        v_ref[0, hh] = kvf[:, hh * dkv + QK_NOPE:(hh + 1) * dkv].astype(BF16)


def mla_up(cq, ckv, kpe, cos, sin, wq_h, wkv_h, q_scale):
    b, t, q_lora = cq.shape
    kv_lora = ckv.shape[2]
    dqk = QK_NOPE + LANES
    nh = wq_h.shape[1] // dqk
    hb = _tile(nh, MLA_UP_HEADS, 1)
    tok = lambda bi, hi: (bi, 0, 0)
    out = lambda bi, hi: (bi, hi, 0, 0)
    return pl.pallas_call(
        functools.partial(_mla_up_kernel, q_scale=q_scale),
        grid=(b, nh // hb),
        in_specs=[pl.BlockSpec((1, t, q_lora), tok),
                  pl.BlockSpec((1, t, kv_lora), tok),
                  pl.BlockSpec((1, t, LANES), tok),
                  pl.BlockSpec((t, LANES), lambda bi, hi: (0, 0)),
                  pl.BlockSpec((t, LANES), lambda bi, hi: (0, 0)),
                  pl.BlockSpec((q_lora, hb * dqk), lambda bi, hi: (0, hi)),
                  pl.BlockSpec((kv_lora, hb * (QK_NOPE + V_DIM)), lambda bi, hi: (0, hi))],
        out_specs=[pl.BlockSpec((1, hb, t, dqk), out),
                   pl.BlockSpec((1, hb, t, dqk), out),
                   pl.BlockSpec((1, hb, t, V_DIM), out)],
        out_shape=[jax.ShapeDtypeStruct((b, nh, t, dqk), BF16),
                   jax.ShapeDtypeStruct((b, nh, t, dqk), BF16),
                   jax.ShapeDtypeStruct((b, nh, t, V_DIM), BF16)],
        compiler_params=_params("arbitrary", "arbitrary"),
        name="mla_up",
    )(cq, ckv, kpe, cos, sin, wq_h, wkv_h)


ATTN_HEADS = 4
ATTN_TILE = 512
ATTN_ROWS = 256


def _attn_kernel(q_ref, k_ref, v_ref, o_ref, *, n_lat, n_ctx):
    def attend(n_rows, key_rows):
        for hh in range(q_ref.shape[1]):
            k = k_ref[0, hh, key_rows, :]
            v = v_ref[0, hh, key_rows, :]
            for r in range(n_rows // ATTN_ROWS):
                rows = slice(r * ATTN_ROWS, (r + 1) * ATTN_ROWS)
                s = lax.dot_general(q_ref[0, hh, rows, :], k, (((1,), (1,)), ((), ())),
                                    preferred_element_type=F32)
                p = jnp.exp2(s - jnp.max(s, axis=-1, keepdims=True))
                l = jnp.sum(p, axis=-1, keepdims=True)
                o = jnp.dot(p.astype(BF16), v, preferred_element_type=F32)
                o_ref[0, rows, hh * V_DIM:(hh + 1) * V_DIM] = (o / l).astype(BF16)

    @pl.when(pl.program_id(2) == 0)
    def _():
        attend(n_ctx, slice(n_lat, n_lat + n_ctx))

    @pl.when(pl.program_id(2) > 0)
    def _():
        attend(ATTN_TILE, slice(0, n_lat + n_ctx))


def attention(q, k, v, n_lat):
    b, nh, t, dqk = q.shape
    n_ctx = t - n_lat
    assert n_lat % ATTN_TILE == 0 and 0 < n_ctx <= ATTN_TILE and n_ctx % ATTN_ROWS == 0
    hb = _tile(nh, ATTN_HEADS, 1)
    n_tiles = n_lat // ATTN_TILE + 1
    tile = lambda i: (i + n_tiles - 1) % n_tiles
    return pl.pallas_call(
        functools.partial(_attn_kernel, n_lat=n_lat, n_ctx=n_ctx),
        grid=(b, nh // hb, n_tiles),
        in_specs=[pl.BlockSpec((1, hb, ATTN_TILE, dqk), lambda bi, hi, i: (bi, hi, tile(i), 0)),
                  pl.BlockSpec((1, hb, t, dqk), lambda bi, hi, i: (bi, hi, 0, 0)),
                  pl.BlockSpec((1, hb, t, V_DIM), lambda bi, hi, i: (bi, hi, 0, 0))],
        out_specs=pl.BlockSpec((1, ATTN_TILE, hb * V_DIM), lambda bi, hi, i: (bi, tile(i), hi)),
        out_shape=jax.ShapeDtypeStruct((b, t, nh * V_DIM), BF16),
        compiler_params=_params("arbitrary", "arbitrary", "arbitrary"),
        name="attention",
    )(q, k, v)


CONV_TILE = 128
CONV_GROUPS = CONV_TILE // SUBLANES


def _dwconv_kernel(prev_ref, cur_ref, next_ref, dw_ref, dwb_ref, g_ref, b_ref, o_ref, buf, cv, *, width):
    i = pl.program_id(1)
    tt = cur_ref.shape[1]
    d = cur_ref.shape[2]
    pad = width // 2
    first = i == 0
    final = i == pl.num_programs(1) - 1
    for c in range(d // LANES):
        cols = slice(c * LANES, (c + 1) * LANES)
        buf[c, 0:HALO, :] = jnp.where(first, 0.0, prev_ref[0, :, cols])
        buf[c, HALO:HALO + tt, :] = cur_ref[0, :, cols]
        buf[c, HALO + tt:, :] = jnp.where(final, 0.0, next_ref[0, :, cols])
        acc = [jnp.broadcast_to(dwb_ref[:, cols], (SUBLANES, LANES)) for _ in range(CONV_GROUPS)]
        for m in range(-pad, CONV_GROUPS + pad):
            xm = buf[c, pl.ds(HALO + m, SUBLANES, stride=CONV_GROUPS), :]
            for j in range(CONV_GROUPS):
                kk = m - j + pad
                if 0 <= kk < width:
                    acc[j] = acc[j] + xm * dw_ref[kk, :, cols]
        for j in range(CONV_GROUPS):
            cv[c, pl.ds(j, SUBLANES, stride=CONV_GROUPS), :] = acc[j]
    a = jnp.concatenate([cv[c] for c in range(d // LANES)], axis=-1)
    mu = jnp.mean(a, axis=-1, keepdims=True)
    xc = a - mu
    var = jnp.mean(xc * xc, axis=-1, keepdims=True)
    y = xc * lax.rsqrt(var + EPS) * g_ref[...] + b_ref[...]
    o_ref[0] = (y * _sigmoid(y)).astype(o_ref.dtype)


def dwconv_ln_swish(u, dw, dwb, ln_g, ln_b):
    b, t, d = u.shape
    width = dw.shape[0]
    tt = CONV_TILE
    assert t % tt == 0 and width // 2 < HALO and d % LANES == 0
    nh = tt // HALO
    last = t // HALO - 1
    vec = lambda bi, i: (0, 0)
    dw8 = jnp.broadcast_to(dw[:, None, :], (width, SUBLANES, d))
    return pl.pallas_call(
        functools.partial(_dwconv_kernel, width=width),
        grid=(b, t // tt),
        in_specs=[pl.BlockSpec((1, HALO, d), lambda bi, i: (bi, jnp.maximum(i * nh - 1, 0), 0)),
                  pl.BlockSpec((1, tt, d), lambda bi, i: (bi, i, 0)),
                  pl.BlockSpec((1, HALO, d), lambda bi, i: (bi, jnp.minimum((i + 1) * nh, last), 0)),
                  pl.BlockSpec((width, SUBLANES, d), lambda bi, i: (0, 0, 0)),
                  pl.BlockSpec((1, d), vec), pl.BlockSpec((1, d), vec), pl.BlockSpec((1, d), vec)],
        out_specs=pl.BlockSpec((1, tt, d), lambda bi, i: (bi, i, 0)),
        out_shape=jax.ShapeDtypeStruct((b, t, d), BF16),
        scratch_shapes=[pltpu.VMEM((d // LANES, tt + 2 * HALO, LANES), F32),
                        pltpu.VMEM((d // LANES, tt, LANES), F32)],
        compiler_params=_params("arbitrary", "arbitrary"),
        name="dwconv_ln_swish",
    )(u, u, u, dw8, dwb.reshape(1, d), ln_g.reshape(1, d), ln_b.reshape(1, d))


def _rope_tables(seq, ctx_len):
    n_axis = QK_ROPE // 4
    t = jnp.arange(seq, dtype=jnp.int32)
    row = (t // GRID_W).astype(F32)
    col = (t % GRID_W).astype(F32)
    freqs = ROPE_THETA ** (-jnp.arange(n_axis, dtype=F32) / n_axis)
    ang = jnp.concatenate([row[:, None] * freqs, col[:, None] * freqs], axis=-1)
    ang = jnp.concatenate([ang, jnp.zeros((ctx_len, QK_ROPE // 2), F32)], axis=0)
    cos, sin = jnp.cos(ang), jnp.sin(ang)
    zeros = jnp.zeros((seq + ctx_len, LANES - QK_ROPE), F32)
    cos_t = jnp.concatenate([cos, cos, zeros], axis=-1)
    sin_t = jnp.concatenate([-sin, sin, zeros], axis=-1)
    return cos_t, sin_t


def _rope_select():
    half = QK_ROPE // 2
    sel = np.zeros((QK_ROPE, LANES), np.float32)
    for p in range(half):
        sel[2 * p, p] = sel[2 * p, QK_ROPE + p] = 1.0
        sel[2 * p + 1, half + p] = sel[2 * p + 1, QK_ROPE + half + p] = 1.0
    return sel


def _rope(g, cos, sin):
    return g * cos + pltpu.roll(g, QK_ROPE // 2, axis=1) * sin


def _mod_tiles(m, layout, k, d):
    b, n_lat, n_ctx = layout
    mk = m[:, k * d:(k + 1) * d]
    parts = [jnp.broadcast_to(mk[:b, None, :], (b, n_lat, d))]
    if n_ctx:
        parts.append(jnp.broadcast_to(mk[b][None, None, :], (b, n_ctx, d)))
    return jnp.concatenate(parts, axis=1).reshape(b * (n_lat + n_ctx), 1, d)


def kernel(x, c, ctx, c_ctx, ada_w, ada_b, norm_g, ffn_w1, ffn_w3, ffn_w2, mla_wdq, mla_gq, mla_wuq, mla_wdkv, mla_gkv, mla_wukv, mla_wo, conv_w1, conv_b1, conv_dw, conv_dwb, conv_ln_g, conv_ln_b, conv_w2, conv_b2, final_g):
    b, s, d = x.shape
    lc = ctx.shape[1]
    depth = ada_w.shape[0]
    d_ff = ffn_w1.shape[-1]
    nh = mla_wo.shape[1] // V_DIM
    assert s % MOD_TILE == 0 and lc % MOD_TILE == 0 and b < COND_ROWS

    cond = jnp.concatenate([c, c_ctx[None, :], jnp.zeros((COND_ROWS - b - 1, d), F32)], axis=0)
    mods = ada_mods(cond, ada_w, ada_b)

    t_all = s + lc
    rows_all = (b, s // MOD_TILE, lc // MOD_TILE)
    rows_lat = (b, s // MOD_TILE, 0)

    src = Hidden((x, ctx), t_all)
    combined = True

    def ffn(src, m, rows, i, k, f):
        xn = norm_mod(src, norm_g[i, k], _mod_tiles(m, rows, 3 * k, d), _mod_tiles(m, rows, 3 * k + 1, d))
        g = gated_up(xn, ffn_w1, (i, f), 0, ffn_w3, (i, f), 0, d_ff, "swiglu")
        return _flat(proj_residual(g, ffn_w2, (i, f), src, _mod_tiles(m, rows, 3 * k + 2, d), FFN_RES_WEIGHT))

    def latent_only(src):
        if src.split:
            return Hidden((src.arrays[0],), s)
        return Hidden((src.arrays[0].reshape(b, t_all, d),), s)

    for i in range(depth):
        last = i == depth - 1
        mixer = i % 2
        j = i // 2
        ctx_into_mixer = (not last) or mixer == 0
        ctx_out = not last
        m = mods[i]
        if combined and not ctx_into_mixer:
            src, combined = latent_only(src), False
        rows = rows_all if combined else rows_lat

        src = ffn(src, m, rows, i, 0, 0)

        shift, scale, gate = (_mod_tiles(m, rows, k, d) for k in (3, 4, 5))
        if mixer == 0:
            assert combined
            cos_t, sin_t = _rope_tables(s, lc)
            cq, ckv, kpe = mla_down(src, norm_g[i, 1], shift, scale, mla_wdq[j], mla_wdkv[j], mla_gq[j], mla_gkv[j],
                                    jnp.tile(cos_t, (b, 1)), jnp.tile(sin_t, (b, 1)))
            wq_h = mla_wq_layout(mla_wuq[j], nh)
            wkv_h = mla_wukv[j]
            q_scale = (QK_NOPE + QK_ROPE) ** -0.5 * math.log2(math.e)
            q, k, v = mla_up(cq.reshape(b, t_all, -1), ckv.reshape(b, t_all, -1), kpe.reshape(b, t_all, -1),
                             cos_t, sin_t, wq_h, wkv_h, q_scale)
            o = attention(q, k, v, s).reshape(b * t_all, nh * V_DIM)
            src = _flat(proj_residual(o, mla_wo, (j,), src, gate, 1.0))
        else:
            assert not combined
            xn = norm_mod(src, norm_g[i, 1], shift, scale)
            u = gated_up(xn, conv_w1, (j,), 0, conv_w1, (j,), 1, d, "glu",
                         bias=(conv_b1[j, :d], conv_b1[j, d:]), out_dtype=F32)
            vv = dwconv_ln_swish(u.reshape(b, s, d), conv_dw[j], conv_dwb[j], conv_ln_g[j], conv_ln_b[j])
            src = _flat(proj_residual(vv.reshape(-1, d), conv_w2, (j,), src, gate, 1.0, bias=conv_b2[j]))

        if combined and not ctx_out:
            src, combined = latent_only(src), False
            rows = rows_lat
        src = ffn(src, m, rows, i, 2, 1)

    if combined:
        src = latent_only(src)
    return norm_mod(src, final_g, out_dtype=F32).reshape(b, s, d)
```

```python
import functools
import math
from typing import NamedTuple

import numpy as np
import jax
import jax.numpy as jnp
from jax import lax
from jax.experimental import pallas as pl
from jax.experimental.pallas import tpu as pltpu

F32 = jnp.float32
BF16 = jnp.bfloat16

GRID_W = 64
QK_NOPE = 128
QK_ROPE = 64
V_DIM = 128
ROPE_THETA = 10000.0
FFN_RES_WEIGHT = 0.5
N_MOD = 9
EPS = 1e-6

LANES = 128
SUBLANES = 8
VMEM_LIMIT_BYTES = 58 * 1024 * 1024

MOD_TILE = 256
COND_ROWS = 8
HALO = 16
CAST_ROWS = 256


def _params(*sem):
    return pltpu.CompilerParams(dimension_semantics=sem, vmem_limit_bytes=VMEM_LIMIT_BYTES)


def _tile(n, pref, unit):
    if n <= pref:
        return n
    t = (pref // unit) * unit
    while t > unit and n % t:
        t -= unit
    assert n % t == 0, (n, pref, unit)
    return t


def _cast_weight(src_ref, dst_ref):
    k = src_ref.shape[0]
    rows = _tile(k, CAST_ROWS, SUBLANES)

    def body(c, carry):
        s = pl.multiple_of(c * rows, rows)
        dst_ref[pl.ds(s, rows), :] = src_ref[pl.ds(s, rows), :].astype(BF16)
        return carry

    lax.fori_loop(0, k // rows, body, 0)


def _sigmoid(x):
    return 1.0 / (1.0 + jnp.exp(-x))


def _ada_kernel(c_ref, w_ref, b_ref, o_ref):
    c = c_ref[...]
    s = (c * _sigmoid(c)).astype(BF16)
    o_ref[0] = jnp.dot(s, w_ref[0].astype(BF16), preferred_element_type=F32) + b_ref[0]


def ada_mods(cond, ada_w, ada_b):
    depth, d, n = ada_w.shape
    tn = _tile(n, 1024, LANES)
    return pl.pallas_call(
        _ada_kernel,
        grid=(depth, n // tn),
        in_specs=[
            pl.BlockSpec((COND_ROWS, d), lambda l, j: (0, 0)),
            pl.BlockSpec((1, d, tn), lambda l, j: (l, 0, j)),
            pl.BlockSpec((1, 1, tn), lambda l, j: (l, 0, j)),
        ],
        out_specs=pl.BlockSpec((1, COND_ROWS, tn), lambda l, j: (l, 0, j)),
        out_shape=jax.ShapeDtypeStruct((depth, COND_ROWS, n), F32),
        compiler_params=_params("arbitrary", "arbitrary"),
        name="ada_mods",
    )(cond, ada_w, ada_b.reshape(depth, 1, n))


class Hidden(NamedTuple):
    arrays: tuple
    t_use: int

    @property
    def split(self):
        if len(self.arrays) == 1:
            return None
        return (self.arrays[0].shape[1] // MOD_TILE, self.arrays[1].shape[1] // MOD_TILE)

    @property
    def shape(self):
        return (self.arrays[0].shape[0] * self.t_use, self.arrays[0].shape[2])


def _flat(h):
    return Hidden((h.reshape(1, *h.shape),), h.shape[0])


def _hidden_tile(src, pref):
    return _tile(src.shape[0] if src.split else src.t_use, pref, MOD_TILE)


def _hidden_specs(src, tm, cols, tile_col):
    if src.split is None:
        nt = src.t_use // tm

        def hmap(*g):
            i, j = tile_col(*g)
            return (i // nt, i % nt, j)

        return [pl.BlockSpec((None, tm, cols), hmap)]
    n_lat, n_ctx = src.split
    n_sub = tm // MOD_TILE
    specs = []
    for r in range(n_sub):
        def xmap(*g, r=r):
            i, j = tile_col(*g)
            t = i * n_sub + r
            return (t // (n_lat + n_ctx), jnp.minimum(t % (n_lat + n_ctx), n_lat - 1), j)

        def cmap(*g, r=r):
            i, j = tile_col(*g)
            t = i * n_sub + r
            return (t // (n_lat + n_ctx), jnp.maximum(t % (n_lat + n_ctx) - n_lat, 0), j)

        specs += [pl.BlockSpec((None, MOD_TILE, cols), xmap), pl.BlockSpec((None, MOD_TILE, cols), cmap)]
    return specs


def _hidden_args(src, specs):
    return list(src.arrays) * (len(specs) // len(src.arrays))


def _hidden_rows(h_refs, r, row_tile, n_sub, split):
    if split is None:
        return h_refs[0][r * MOD_TILE:(r + 1) * MOD_TILE, :]
    n_lat, n_ctx = split
    is_ctx = (row_tile * n_sub + r) % (n_lat + n_ctx) >= n_lat
    return jnp.where(is_ctx, h_refs[2 * r + 1][...], h_refs[2 * r][...])


def _rms(x, g):
    return x * lax.rsqrt(jnp.mean(x * x, axis=-1, keepdims=True) + EPS) * g


def _norm_kernel(*refs, modulate, n_sub, n_h, split):
    h_refs, refs = refs[:n_h], refs[n_h:]
    if modulate:
        g_ref, sh_ref, sc_ref, o_ref = refs
    else:
        g_ref, o_ref = refs
    for r in range(n_sub):
        y = _rms(_hidden_rows(h_refs, r, pl.program_id(0), n_sub, split), g_ref[...])
        if modulate:
            y = y * (1.0 + sc_ref[r]) + sh_ref[r]
        o_ref[r * MOD_TILE:(r + 1) * MOD_TILE, :] = y.astype(o_ref.dtype)


def norm_mod(src, g, shift=None, scale=None, out_dtype=BF16):
    m, d = src.shape
    tm = _hidden_tile(src, 4 * MOD_TILE)
    n_sub = tm // MOD_TILE
    modulate = shift is not None
    h_specs = _hidden_specs(src, tm, d, lambda i: (i, 0))
    in_specs = h_specs + [pl.BlockSpec((1, d), lambda i: (0, 0))]
    args = _hidden_args(src, h_specs) + [g.reshape(1, d)]
    if modulate:
        in_specs += [pl.BlockSpec((n_sub, 1, d), lambda i: (i, 0, 0))] * 2
        args += [shift, scale]
    return pl.pallas_call(
        functools.partial(_norm_kernel, modulate=modulate, n_sub=n_sub, n_h=len(h_specs), split=src.split),
        grid=(m // tm,),
        in_specs=in_specs,
        out_specs=pl.BlockSpec((tm, d), lambda i: (i, 0)),
        out_shape=jax.ShapeDtypeStruct((m, d), out_dtype),
        compiler_params=_params("arbitrary"),
        name="norm_mod",
    )(*args)


def _up_kernel(*refs, act, has_bias):
    if has_bias:
        x_ref, wa_ref, wb_ref, ba_ref, bb_ref, o_ref, wa_s, wb_s = refs
    else:
        x_ref, wa_ref, wb_ref, o_ref, wa_s, wb_s = refs

    @pl.when(pl.program_id(1) == 0)
    def _():
        _cast_weight(wa_ref, wa_s)
        _cast_weight(wb_ref, wb_s)

    x = x_ref[...]
    a = jnp.dot(x, wa_s[...], preferred_element_type=F32)
    b = jnp.dot(x, wb_s[...], preferred_element_type=F32)
    if has_bias:
        a = a + ba_ref[...]
        b = b + bb_ref[...]
    if act == "swiglu":
        o = (a * _sigmoid(a)) * b
    else:
        o = a * _sigmoid(b)
    o_ref[...] = o.astype(o_ref.dtype)


def gated_up(x, wa, wa_idx, wa_col, wb, wb_idx, wb_col, n, act, bias=None, out_dtype=BF16):
    m, k = x.shape
    tm = _tile(m, 2048, MOD_TILE)
    tn = _tile(n, 512, LANES)
    nj = n // tn

    def wspec(idx, col):
        lead = (None,) * len(idx)
        return pl.BlockSpec(lead + (k, tn), lambda j, i: idx + (0, col * nj + j))

    in_specs = [pl.BlockSpec((tm, k), lambda j, i: (i, 0)), wspec(wa_idx, wa_col), wspec(wb_idx, wb_col)]
    args = [x, wa, wb]
    if bias is not None:
        ba, bb = bias
        in_specs += [pl.BlockSpec((1, tn), lambda j, i: (0, j))] * 2
        args += [ba.reshape(1, n), bb.reshape(1, n)]
    return pl.pallas_call(
        functools.partial(_up_kernel, act=act, has_bias=bias is not None),
        grid=(nj, m // tm),
        in_specs=in_specs,
        out_specs=pl.BlockSpec((tm, tn), lambda j, i: (i, j)),
        out_shape=jax.ShapeDtypeStruct((m, n), out_dtype),
        scratch_shapes=[pltpu.VMEM((k, tn), BF16), pltpu.VMEM((k, tn), BF16)],
        compiler_params=_params("arbitrary", "arbitrary"),
        name="gated_up_" + act,
    )(*args)


def _res_kernel(*refs, coef, n_sub, has_bias, n_h, split):
    x_ref, w_ref = refs[:2]
    refs = refs[2:]
    if has_bias:
        b_ref, refs = refs[0], refs[1:]
    h_refs, (gate_ref, o_ref, w_s) = refs[:n_h], refs[n_h:]

    @pl.when(pl.program_id(1) == 0)
    def _():
        _cast_weight(w_ref, w_s)

    y = jnp.dot(x_ref[...], w_s[...], preferred_element_type=F32)
    if has_bias:
        y = y + b_ref[...]
    for r in range(n_sub):
        rows = slice(r * MOD_TILE, (r + 1) * MOD_TILE)
        h = _hidden_rows(h_refs, r, pl.program_id(1), n_sub, split)
        o_ref[rows, :] = h + (coef * gate_ref[r]) * y[rows, :]


def proj_residual(x, w, w_idx, src, gate, coef, bias=None):
    m, k = x.shape
    n = src.shape[1]
    assert src.shape[0] == m
    tn = _tile(n, 1024, LANES)
    if k * tn * 14 <= VMEM_LIMIT_BYTES // 2:
        tm, w_mode = _hidden_tile(src, 1024), {}
    else:
        tm, w_mode = _hidden_tile(src, 512), {"pipeline_mode": pl.Buffered(1)}
    n_sub = tm // MOD_TILE
    lead = (None,) * len(w_idx)
    in_specs = [pl.BlockSpec((tm, k), lambda j, i: (i, 0)),
                pl.BlockSpec(lead + (k, tn), lambda j, i: w_idx + (0, j), **w_mode)]
    args = [x, w]
    if bias is not None:
        in_specs.append(pl.BlockSpec((1, tn), lambda j, i: (0, j)))
        args.append(bias.reshape(1, n))
    h_specs = _hidden_specs(src, tm, tn, lambda j, i: (i, j))
    in_specs += h_specs + [pl.BlockSpec((n_sub, 1, tn), lambda j, i: (i, 0, j))]
    args += _hidden_args(src, h_specs) + [gate]
    return pl.pallas_call(
        functools.partial(_res_kernel, coef=coef, n_sub=n_sub, has_bias=bias is not None,
                          n_h=len(h_specs), split=src.split),
        grid=(n // tn, m // tm),
        in_specs=in_specs,
        out_specs=pl.BlockSpec((tm, tn), lambda j, i: (i, j)),
        out_shape=jax.ShapeDtypeStruct((m, n), F32),
        scratch_shapes=[pltpu.VMEM((k, tn), BF16)],
        compiler_params=_params("arbitrary", "arbitrary"),
        name="proj_residual",
    )(*args)


def _mla_down_kernel(*refs, lora, n_sub, n_h, split):
    h_refs, refs = refs[:n_h], refs[n_h:]
    (g_ref, sh_ref, sc_ref, wq_ref, wkv_ref, sel_ref, gq_ref, gkv_ref, cos_ref, sin_ref,
     cq_ref, ckv_ref, kpe_ref, wq_s, wkv_s, xn_s) = refs

    @pl.when(pl.program_id(0) == 0)
    def _():
        _cast_weight(wq_ref, wq_s)
        rows = _tile(wkv_ref.shape[0], CAST_ROWS, SUBLANES)

        def body(c, carry):
            r = pl.ds(pl.multiple_of(c * rows, rows), rows)
            w = wkv_ref[r, :].astype(BF16)
            wkv_s[r, :lora] = w[:, :lora]
            wkv_s[r, lora:] = jnp.dot(w[:, lora:], sel_ref[...], preferred_element_type=F32).astype(BF16)
            return carry

        lax.fori_loop(0, wkv_ref.shape[0] // rows, body, 0)

    for r in range(n_sub):
        h = _hidden_rows(h_refs, r, pl.program_id(0), n_sub, split)
        y = _rms(h, g_ref[...]) * (1.0 + sc_ref[r]) + sh_ref[r]
        xn_s[r * MOD_TILE:(r + 1) * MOD_TILE, :] = y.astype(BF16)
    x = xn_s[...]
    cq = jnp.dot(x, wq_s[...], preferred_element_type=F32)
    cq_ref[...] = _rms(cq, gq_ref[...]).astype(BF16)
    kv = jnp.dot(x, wkv_s[...], preferred_element_type=F32)
    ckv_ref[...] = _rms(kv[:, :lora], gkv_ref[...]).astype(BF16)
    kpe = kv[:, lora:lora + LANES] * cos_ref[...] + kv[:, lora + LANES:] * sin_ref[...]
    kpe_ref[...] = kpe.astype(BF16)


def mla_down(src, g, shift, scale, wdq, wdkv, gq, gkv, cos, sin):
    m, d = src.shape
    q_lora = wdq.shape[1]
    kv_lora = gkv.shape[0]
    sel = jnp.asarray(_rope_select_pair(), BF16)
    nkv = kv_lora + sel.shape[1]
    tm = _hidden_tile(src, 2 * MOD_TILE)
    n_sub = tm // MOD_TILE
    const = lambda i: (0, 0)
    row = lambda i: (i, 0)
    mod = pl.BlockSpec((n_sub, 1, d), lambda i: (i, 0, 0))
    h_specs = _hidden_specs(src, tm, d, lambda i: (i, 0))
    return pl.pallas_call(
        functools.partial(_mla_down_kernel, lora=kv_lora, n_sub=n_sub, n_h=len(h_specs), split=src.split),
        grid=(m // tm,),
        in_specs=h_specs + [pl.BlockSpec((1, d), const), mod, mod,
                            pl.BlockSpec((d, q_lora), const),
                            pl.BlockSpec(wdkv.shape, const),
                            pl.BlockSpec(sel.shape, const),
                            pl.BlockSpec((1, q_lora), const),
                            pl.BlockSpec((1, kv_lora), const),
                            pl.BlockSpec((tm, LANES), row),
                            pl.BlockSpec((tm, LANES), row)],
        out_specs=[pl.BlockSpec((tm, q_lora), row),
                   pl.BlockSpec((tm, kv_lora), row),
                   pl.BlockSpec((tm, LANES), row)],
        out_shape=[jax.ShapeDtypeStruct((m, q_lora), BF16),
                   jax.ShapeDtypeStruct((m, kv_lora), BF16),
                   jax.ShapeDtypeStruct((m, LANES), BF16)],
        scratch_shapes=[pltpu.VMEM((d, q_lora), BF16), pltpu.VMEM((d, nkv), BF16), pltpu.VMEM((tm, d), BF16)],
        compiler_params=_params("arbitrary"),
        name="mla_down",
    )(*_hidden_args(src, h_specs), g.reshape(1, d), shift, scale,
      wdq, wdkv, sel, gq.reshape(1, -1), gkv.reshape(1, -1), cos, sin)


def _wq_layout_kernel(w_ref, sel_ref, o_ref):
    o_ref[...] = jnp.dot(w_ref[...].astype(BF16), sel_ref[...], preferred_element_type=F32).astype(BF16)


def mla_wq_layout(wuq, nh):
    q_lora = wuq.shape[0]
    dh = QK_NOPE + QK_ROPE
    rope = _rope_select()
    one = np.zeros((dh, QK_NOPE + rope.shape[1]), np.float32)
    one[:QK_NOPE, :QK_NOPE] = np.eye(QK_NOPE)
    one[QK_NOPE:, QK_NOPE:] = rope
    pair = np.zeros((2 * one.shape[0], 2 * one.shape[1]), np.float32)
    pair[:one.shape[0], :one.shape[1]] = one
    pair[one.shape[0]:, one.shape[1]:] = one
    assert nh % 2 == 0
    return pl.pallas_call(
        _wq_layout_kernel,
        grid=(nh // 2,),
        in_specs=[pl.BlockSpec((q_lora, pair.shape[0]), lambda h: (0, h)),
                  pl.BlockSpec(pair.shape, lambda h: (0, 0))],
        out_specs=pl.BlockSpec((q_lora, pair.shape[1]), lambda h: (0, h)),
        out_shape=jax.ShapeDtypeStruct((q_lora, nh * one.shape[1]), BF16),
        compiler_params=_params("arbitrary"),
        name="mla_wq_layout",
    )(wuq, jnp.asarray(pair, BF16))


MLA_UP_HEADS = 2


def _mla_up_kernel(cq_ref, ckv_ref, kpe_ref, cos_ref, sin_ref, wq_ref, wkv_ref,
                   q_ref, k_ref, v_ref, *, q_scale):
    dq, dkv = QK_NOPE + LANES, QK_NOPE + V_DIM
    qf = jnp.dot(cq_ref[0], wq_ref[...], preferred_element_type=F32)
    kvf = jnp.dot(ckv_ref[0], wkv_ref[...].astype(BF16), preferred_element_type=F32)
    for hh in range(q_ref.shape[1]):
        q_pe = _rope(qf[:, hh * dq + QK_NOPE:(hh + 1) * dq], cos_ref[...], sin_ref[...])
        q_ref[0, hh, :, :QK_NOPE] = (qf[:, hh * dq:hh * dq + QK_NOPE] * q_scale).astype(BF16)
        q_ref[0, hh, :, QK_NOPE:] = (q_pe * q_scale).astype(BF16)
        k_ref[0, hh, :, :QK_NOPE] = kvf[:, hh * dkv:hh * dkv + QK_NOPE].astype(BF16)
        k_ref[0, hh, :, QK_NOPE:] = kpe_ref[0]
        v_ref[0, hh] = kvf[:, hh * dkv + QK_NOPE:(hh + 1) * dkv].astype(BF16)


def mla_up(cq, ckv, kpe, cos, sin, wq_h, wkv_h, q_scale):
    b, t, q_lora = cq.shape
    kv_lora = ckv.shape[2]
    dqk = QK_NOPE + LANES
    nh = wq_h.shape[1] // dqk
    hb = _tile(nh, MLA_UP_HEADS, 1)
    tok = lambda bi, hi: (bi, 0, 0)
    out = lambda bi, hi: (bi, hi, 0, 0)
    return pl.pallas_call(
        functools.partial(_mla_up_kernel, q_scale=q_scale),
        grid=(b, nh // hb),
        in_specs=[pl.BlockSpec((1, t, q_lora), tok),
                  pl.BlockSpec((1, t, kv_lora), tok),
                  pl.BlockSpec((1, t, LANES), tok),
                  pl.BlockSpec((t, LANES), lambda bi, hi: (0, 0)),
                  pl.BlockSpec((t, LANES), lambda bi, hi: (0, 0)),
                  pl.BlockSpec((q_lora, hb * dqk), lambda bi, hi: (0, hi)),
                  pl.BlockSpec((kv_lora, hb * (QK_NOPE + V_DIM)), lambda bi, hi: (0, hi))],
        out_specs=[pl.BlockSpec((1, hb, t, dqk), out),
                   pl.BlockSpec((1, hb, t, dqk), out),
                   pl.BlockSpec((1, hb, t, V_DIM), out)],
        out_shape=[jax.ShapeDtypeStruct((b, nh, t, dqk), BF16),
                   jax.ShapeDtypeStruct((b, nh, t, dqk), BF16),
                   jax.ShapeDtypeStruct((b, nh, t, V_DIM), BF16)],
        compiler_params=_params("arbitrary", "arbitrary"),
        name="mla_up",
    )(cq, ckv, kpe, cos, sin, wq_h, wkv_h)


ATTN_HEADS = 4
ATTN_TILE = 512
ATTN_ROWS = 256


def _attn_kernel(q_ref, k_ref, v_ref, o_ref, *, n_lat, n_ctx):
    def attend(n_rows, key_rows):
        for hh in range(q_ref.shape[1]):
            k = k_ref[0, hh, key_rows, :]
            v = v_ref[0, hh, key_rows, :]
            for r in range(n_rows // ATTN_ROWS):
                rows = slice(r * ATTN_ROWS, (r + 1) * ATTN_ROWS)
                s = lax.dot_general(q_ref[0, hh, rows, :], k, (((1,), (1,)), ((), ())),
                                    preferred_element_type=F32)
                p = jnp.exp2(s - jnp.max(s, axis=-1, keepdims=True))
                l = jnp.sum(p, axis=-1, keepdims=True)
                o = jnp.dot(p.astype(BF16), v, preferred_element_type=F32)
                o_ref[0, rows, hh * V_DIM:(hh + 1) * V_DIM] = (o / l).astype(BF16)

    @pl.when(pl.program_id(2) == 0)
    def _():
        attend(n_ctx, slice(n_lat, n_lat + n_ctx))

    @pl.when(pl.program_id(2) > 0)
    def _():
        attend(ATTN_TILE, slice(0, n_lat + n_ctx))


def attention(q, k, v, n_lat):
    b, nh, t, dqk = q.shape
    n_ctx = t - n_lat
    assert n_lat % ATTN_TILE == 0 and 0 < n_ctx <= ATTN_TILE and n_ctx % ATTN_ROWS == 0
    hb = _tile(nh, ATTN_HEADS, 1)
    n_tiles = n_lat // ATTN_TILE + 1
    tile = lambda i: (i + n_tiles - 1) % n_tiles
    return pl.pallas_call(
        functools.partial(_attn_kernel, n_lat=n_lat, n_ctx=n_ctx),
        grid=(b, nh // hb, n_tiles),
        in_specs=[pl.BlockSpec((1, hb, ATTN_TILE, dqk), lambda bi, hi, i: (bi, hi, tile(i), 0)),
                  pl.BlockSpec((1, hb, t, dqk), lambda bi, hi, i: (bi, hi, 0, 0)),
                  pl.BlockSpec((1, hb, t, V_DIM), lambda bi, hi, i: (bi, hi, 0, 0))],
        out_specs=pl.BlockSpec((1, ATTN_TILE, hb * V_DIM), lambda bi, hi, i: (bi, tile(i), hi)),
        out_shape=jax.ShapeDtypeStruct((b, t, nh * V_DIM), BF16),
        compiler_params=_params("arbitrary", "arbitrary", "arbitrary"),
        name="attention",
    )(q, k, v)


CONV_TILE = 128
CONV_GROUPS = CONV_TILE // SUBLANES


def _dwconv_kernel(prev_ref, cur_ref, next_ref, dw_ref, dwb_ref, g_ref, b_ref, o_ref, buf, cv, *, width):
    i = pl.program_id(1)
    tt = cur_ref.shape[1]
    d = cur_ref.shape[2]
    pad = width // 2
    first = i == 0
    final = i == pl.num_programs(1) - 1
    for c in range(d // LANES):
        cols = slice(c * LANES, (c + 1) * LANES)
        buf[c, 0:HALO, :] = jnp.where(first, 0.0, prev_ref[0, :, cols])
        buf[c, HALO:HALO + tt, :] = cur_ref[0, :, cols]
        buf[c, HALO + tt:, :] = jnp.where(final, 0.0, next_ref[0, :, cols])
        acc = [jnp.broadcast_to(dwb_ref[:, cols], (SUBLANES, LANES)) for _ in range(CONV_GROUPS)]
        for m in range(-pad, CONV_GROUPS + pad):
            xm = buf[c, pl.ds(HALO + m, SUBLANES, stride=CONV_GROUPS), :]
            for j in range(CONV_GROUPS):
                kk = m - j + pad
                if 0 <= kk < width:
                    acc[j] = acc[j] + xm * dw_ref[kk, :, cols]
        for j in range(CONV_GROUPS):
            cv[c, pl.ds(j, SUBLANES, stride=CONV_GROUPS), :] = acc[j]
    a = jnp.concatenate([cv[c] for c in range(d // LANES)], axis=-1)
    mu = jnp.mean(a, axis=-1, keepdims=True)
    xc = a - mu
    var = jnp.mean(xc * xc, axis=-1, keepdims=True)
    y = xc * lax.rsqrt(var + EPS) * g_ref[...] + b_ref[...]
    o_ref[0] = (y * _sigmoid(y)).astype(o_ref.dtype)


def dwconv_ln_swish(u, dw, dwb, ln_g, ln_b):
    b, t, d = u.shape
    width = dw.shape[0]
    tt = CONV_TILE
    assert t % tt == 0 and width // 2 < HALO and d % LANES == 0
    nh = tt // HALO
    last = t // HALO - 1
    vec = lambda bi, i: (0, 0)
    dw8 = jnp.broadcast_to(dw[:, None, :], (width, SUBLANES, d))
    return pl.pallas_call(
        functools.partial(_dwconv_kernel, width=width),
        grid=(b, t // tt),
        in_specs=[pl.BlockSpec((1, HALO, d), lambda bi, i: (bi, jnp.maximum(i * nh - 1, 0), 0)),
                  pl.BlockSpec((1, tt, d), lambda bi, i: (bi, i, 0)),
                  pl.BlockSpec((1, HALO, d), lambda bi, i: (bi, jnp.minimum((i + 1) * nh, last), 0)),
                  pl.BlockSpec((width, SUBLANES, d), lambda bi, i: (0, 0, 0)),
                  pl.BlockSpec((1, d), vec), pl.BlockSpec((1, d), vec), pl.BlockSpec((1, d), vec)],
        out_specs=pl.BlockSpec((1, tt, d), lambda bi, i: (bi, i, 0)),
        out_shape=jax.ShapeDtypeStruct((b, t, d), BF16),
        scratch_shapes=[pltpu.VMEM((d // LANES, tt + 2 * HALO, LANES), F32),
                        pltpu.VMEM((d // LANES, tt, LANES), F32)],
        compiler_params=_params("arbitrary", "arbitrary"),
        name="dwconv_ln_swish",
    )(u, u, u, dw8, dwb.reshape(1, d), ln_g.reshape(1, d), ln_b.reshape(1, d))


def _rope_tables(seq, ctx_len):
    n_axis = QK_ROPE // 4
    t = jnp.arange(seq, dtype=jnp.int32)
    row = (t // GRID_W).astype(F32)
    col = (t % GRID_W).astype(F32)
    freqs = ROPE_THETA ** (-jnp.arange(n_axis, dtype=F32) / n_axis)
    ang = jnp.concatenate([row[:, None] * freqs, col[:, None] * freqs], axis=-1)
    ang = jnp.concatenate([ang, jnp.zeros((ctx_len, QK_ROPE // 2), F32)], axis=0)
    cos, sin = jnp.cos(ang), jnp.sin(ang)
    zeros = jnp.zeros((seq + ctx_len, LANES - QK_ROPE), F32)
    cos_t = jnp.concatenate([cos, cos, zeros], axis=-1)
    sin_t = jnp.concatenate([-sin, sin, zeros], axis=-1)
    return cos_t, sin_t


def _rope_select():
    half = QK_ROPE // 2
    sel = np.zeros((QK_ROPE, LANES), np.float32)
    for p in range(half):
        sel[2 * p, p] = sel[2 * p, QK_ROPE + p] = 1.0
        sel[2 * p + 1, half + p] = sel[2 * p + 1, QK_ROPE + half + p] = 1.0
    return sel


def _rope_select_pair():
    sel = _rope_select()
    out = np.zeros((QK_ROPE, 2 * LANES), np.float32)
    out[:, :QK_ROPE] = sel[:, :QK_ROPE]
    out[:, LANES:LANES + QK_ROPE] = sel[:, QK_ROPE // 2:QK_ROPE // 2 + QK_ROPE]
    return out


def _rope(g, cos, sin):
    return g * cos + pltpu.roll(g, QK_ROPE // 2, axis=1) * sin


def _mod_tiles(m, layout, k, d):
    b, n_lat, n_ctx = layout
    mk = m[:, k * d:(k + 1) * d]
    parts = [jnp.broadcast_to(mk[:b, None, :], (b, n_lat, d))]
    if n_ctx:
        parts.append(jnp.broadcast_to(mk[b][None, None, :], (b, n_ctx, d)))
    return jnp.concatenate(parts, axis=1).reshape(b * (n_lat + n_ctx), 1, d)


def kernel(x, c, ctx, c_ctx, ada_w, ada_b, norm_g, ffn_w1, ffn_w3, ffn_w2, mla_wdq, mla_gq, mla_wuq, mla_wdkv, mla_gkv, mla_wukv, mla_wo, conv_w1, conv_b1, conv_dw, conv_dwb, conv_ln_g, conv_ln_b, conv_w2, conv_b2, final_g):
    b, s, d = x.shape
    lc = ctx.shape[1]
    depth = ada_w.shape[0]
    d_ff = ffn_w1.shape[-1]
    nh = mla_wo.shape[1] // V_DIM
    assert s % MOD_TILE == 0 and lc % MOD_TILE == 0 and b < COND_ROWS

    cond = jnp.concatenate([c, c_ctx[None, :], jnp.zeros((COND_ROWS - b - 1, d), F32)], axis=0)
    mods = ada_mods(cond, ada_w, ada_b)

    t_all = s + lc
    rows_all = (b, s // MOD_TILE, lc // MOD_TILE)
    rows_lat = (b, s // MOD_TILE, 0)

    src = Hidden((x, ctx), t_all)
    combined = True

    def ffn(src, m, rows, i, k, f):
        xn = norm_mod(src, norm_g[i, k], _mod_tiles(m, rows, 3 * k, d), _mod_tiles(m, rows, 3 * k + 1, d))
        g = gated_up(xn, ffn_w1, (i, f), 0, ffn_w3, (i, f), 0, d_ff, "swiglu")
        return _flat(proj_residual(g, ffn_w2, (i, f), src, _mod_tiles(m, rows, 3 * k + 2, d), FFN_RES_WEIGHT))

    def latent_only(src):
        if src.split:
            return Hidden((src.arrays[0],), s)
        return Hidden((src.arrays[0].reshape(b, t_all, d),), s)

    for i in range(depth):
        last = i == depth - 1
        mixer = i % 2
        j = i // 2
        ctx_into_mixer = (not last) or mixer == 0
        ctx_out = not last
        m = mods[i]
        if combined and not ctx_into_mixer:
            src, combined = latent_only(src), False
        rows = rows_all if combined else rows_lat

        src = ffn(src, m, rows, i, 0, 0)

        shift, scale, gate = (_mod_tiles(m, rows, k, d) for k in (3, 4, 5))
        if mixer == 0:
            assert combined
            cos_t, sin_t = _rope_tables(s, lc)
            cq, ckv, kpe = mla_down(src, norm_g[i, 1], shift, scale, mla_wdq[j], mla_wdkv[j], mla_gq[j], mla_gkv[j],
                                    jnp.tile(cos_t, (b, 1)), jnp.tile(sin_t, (b, 1)))
            wq_h = mla_wq_layout(mla_wuq[j], nh)
            wkv_h = mla_wukv[j]
            q_scale = (QK_NOPE + QK_ROPE) ** -0.5 * math.log2(math.e)
            q, k, v = mla_up(cq.reshape(b, t_all, -1), ckv.reshape(b, t_all, -1), kpe.reshape(b, t_all, -1),
                             cos_t, sin_t, wq_h, wkv_h, q_scale)
            o = attention(q, k, v, s).reshape(b * t_all, nh * V_DIM)
            src = _flat(proj_residual(o, mla_wo, (j,), src, gate, 1.0))
        else:
            assert not combined
            xn = norm_mod(src, norm_g[i, 1], shift, scale)
            u = gated_up(xn, conv_w1, (j,), 0, conv_w1, (j,), 1, d, "glu",
                         bias=(conv_b1[j, :d], conv_b1[j, d:]), out_dtype=F32)
            vv = dwconv_ln_swish(u.reshape(b, s, d), conv_dw[j], conv_dwb[j], conv_ln_g[j], conv_ln_b[j])
            src = _flat(proj_residual(vv.reshape(-1, d), conv_w2, (j,), src, gate, 1.0, bias=conv_b2[j]))

        if combined and not ctx_out:
            src, combined = latent_only(src), False
            rows = rows_lat
        src = ffn(src, m, rows, i, 2, 1)

    if combined:
        src = latent_only(src)
    return norm_mod(src, final_g, out_dtype=F32).reshape(b, s, d)
```

```python
import functools
import math
from typing import NamedTuple

import numpy as np
import jax
import jax.numpy as jnp
from jax import lax
from jax.experimental import pallas as pl
from jax.experimental.pallas import tpu as pltpu

F32 = jnp.float32
BF16 = jnp.bfloat16

GRID_W = 64
QK_NOPE = 128
QK_ROPE = 64
V_DIM = 128
ROPE_THETA = 10000.0
FFN_RES_WEIGHT = 0.5
N_MOD = 9
EPS = 1e-6

LANES = 128
SUBLANES = 8
VMEM_LIMIT_BYTES = 58 * 1024 * 1024

MOD_TILE = 256
COND_ROWS = 8
HALO = 16
CAST_ROWS = 256


def _params(*sem):
    return pltpu.CompilerParams(dimension_semantics=sem, vmem_limit_bytes=VMEM_LIMIT_BYTES)


def _tile(n, pref, unit):
    if n <= pref:
        return n
    t = (pref // unit) * unit
    while t > unit and n % t:
        t -= unit
    assert n % t == 0, (n, pref, unit)
    return t


def _cast_weight(src_ref, dst_ref):
    k = src_ref.shape[0]
    rows = _tile(k, CAST_ROWS, SUBLANES)

    def body(c, carry):
        s = pl.multiple_of(c * rows, rows)
        dst_ref[pl.ds(s, rows), :] = src_ref[pl.ds(s, rows), :].astype(BF16)
        return carry

    lax.fori_loop(0, k // rows, body, 0)


def _sigmoid(x):
    return 1.0 / (1.0 + jnp.exp(-x))


def _ada_kernel(c_ref, w_ref, b_ref, o_ref):
    c = c_ref[...]
    s = (c * _sigmoid(c)).astype(BF16)
    o_ref[0] = jnp.dot(s, w_ref[0].astype(BF16), preferred_element_type=F32) + b_ref[0]


def ada_mods(cond, ada_w, ada_b):
    depth, d, n = ada_w.shape
    tn = _tile(n, 1024, LANES)
    return pl.pallas_call(
        _ada_kernel,
        grid=(depth, n // tn),
        in_specs=[
            pl.BlockSpec((COND_ROWS, d), lambda l, j: (0, 0)),
            pl.BlockSpec((1, d, tn), lambda l, j: (l, 0, j)),
            pl.BlockSpec((1, 1, tn), lambda l, j: (l, 0, j)),
        ],
        out_specs=pl.BlockSpec((1, COND_ROWS, tn), lambda l, j: (l, 0, j)),
        out_shape=jax.ShapeDtypeStruct((depth, COND_ROWS, n), F32),
        compiler_params=_params("arbitrary", "arbitrary"),
        name="ada_mods",
    )(cond, ada_w, ada_b.reshape(depth, 1, n))


class Hidden(NamedTuple):
    arrays: tuple
    t_use: int

    @property
    def split(self):
        if len(self.arrays) == 1:
            return None
        return (self.arrays[0].shape[1] // MOD_TILE, self.arrays[1].shape[1] // MOD_TILE)

    @property
    def shape(self):
        return (self.arrays[0].shape[0] * self.t_use, self.arrays[0].shape[2])


def _flat(h):
    return Hidden((h.reshape(1, *h.shape),), h.shape[0])


def _hidden_tile(src, pref):
    return _tile(src.shape[0] if src.split else src.t_use, pref, MOD_TILE)


def _hidden_specs(src, tm, cols, tile_col):
    if src.split is None:
        nt = src.t_use // tm

        def hmap(*g):
            i, j = tile_col(*g)
            return (i // nt, i % nt, j)

        return [pl.BlockSpec((None, tm, cols), hmap)]
    n_lat, n_ctx = src.split
    n_sub = tm // MOD_TILE
    specs = []
    for r in range(n_sub):
        def xmap(*g, r=r):
            i, j = tile_col(*g)
            t = i * n_sub + r
            return (t // (n_lat + n_ctx), jnp.minimum(t % (n_lat + n_ctx), n_lat - 1), j)

        def cmap(*g, r=r):
            i, j = tile_col(*g)
            t = i * n_sub + r
            return (t // (n_lat + n_ctx), jnp.maximum(t % (n_lat + n_ctx) - n_lat, 0), j)

        specs += [pl.BlockSpec((None, MOD_TILE, cols), xmap), pl.BlockSpec((None, MOD_TILE, cols), cmap)]
    return specs


def _hidden_args(src, specs):
    return list(src.arrays) * (len(specs) // len(src.arrays))


def _hidden_rows(h_refs, r, row_tile, n_sub, split):
    if split is None:
        return h_refs[0][r * MOD_TILE:(r + 1) * MOD_TILE, :]
    n_lat, n_ctx = split
    is_ctx = (row_tile * n_sub + r) % (n_lat + n_ctx) >= n_lat
    return jnp.where(is_ctx, h_refs[2 * r + 1][...], h_refs[2 * r][...])


def _rms(x, g):
    return x * lax.rsqrt(jnp.mean(x * x, axis=-1, keepdims=True) + EPS) * g


def _norm_kernel(*refs, modulate, n_sub, n_h, split):
    h_refs, refs = refs[:n_h], refs[n_h:]
    if modulate:
        g_ref, sh_ref, sc_ref, o_ref = refs
    else:
        g_ref, o_ref = refs
    for r in range(n_sub):
        y = _rms(_hidden_rows(h_refs, r, pl.program_id(0), n_sub, split), g_ref[...])
        if modulate:
            y = y * (1.0 + sc_ref[r]) + sh_ref[r]
        o_ref[r * MOD_TILE:(r + 1) * MOD_TILE, :] = y.astype(o_ref.dtype)


def norm_mod(src, g, shift=None, scale=None, out_dtype=BF16):
    m, d = src.shape
    tm = _hidden_tile(src, 4 * MOD_TILE)
    n_sub = tm // MOD_TILE
    modulate = shift is not None
    h_specs = _hidden_specs(src, tm, d, lambda i: (i, 0))
    in_specs = h_specs + [pl.BlockSpec((1, d), lambda i: (0, 0))]
    args = _hidden_args(src, h_specs) + [g.reshape(1, d)]
    if modulate:
        in_specs += [pl.BlockSpec((n_sub, 1, d), lambda i: (i, 0, 0))] * 2
        args += [shift, scale]
    return pl.pallas_call(
        functools.partial(_norm_kernel, modulate=modulate, n_sub=n_sub, n_h=len(h_specs), split=src.split),
        grid=(m // tm,),
        in_specs=in_specs,
        out_specs=pl.BlockSpec((tm, d), lambda i: (i, 0)),
        out_shape=jax.ShapeDtypeStruct((m, d), out_dtype),
        compiler_params=_params("arbitrary"),
        name="norm_mod",
    )(*args)


def _up_kernel(*refs, act, has_bias):
    if has_bias:
        x_ref, wa_ref, wb_ref, ba_ref, bb_ref, o_ref, wa_s, wb_s = refs
    else:
        x_ref, wa_ref, wb_ref, o_ref, wa_s, wb_s = refs

    @pl.when(pl.program_id(1) == 0)
    def _():
        _cast_weight(wa_ref, wa_s)
        _cast_weight(wb_ref, wb_s)

    x = x_ref[...]
    a = jnp.dot(x, wa_s[...], preferred_element_type=F32)
    b = jnp.dot(x, wb_s[...], preferred_element_type=F32)
    if has_bias:
        a = a + ba_ref[...]
        b = b + bb_ref[...]
    if act == "swiglu":
        o = (a * _sigmoid(a)) * b
    else:
        o = a * _sigmoid(b)
    o_ref[...] = o.astype(o_ref.dtype)


def gated_up(x, wa, wa_idx, wa_col, wb, wb_idx, wb_col, n, act, bias=None, out_dtype=BF16):
    m, k = x.shape
    tm = _tile(m, 1024, MOD_TILE)
    tn = _tile(n, 512, LANES)
    nj = n // tn

    def wspec(idx, col):
        lead = (None,) * len(idx)
        return pl.BlockSpec(lead + (k, tn), lambda j, i: idx + (0, col * nj + j))

    in_specs = [pl.BlockSpec((tm, k), lambda j, i: (i, 0)), wspec(wa_idx, wa_col), wspec(wb_idx, wb_col)]
    args = [x, wa, wb]
    if bias is not None:
        ba, bb = bias
        in_specs += [pl.BlockSpec((1, tn), lambda j, i: (0, j))] * 2
        args += [ba.reshape(1, n), bb.reshape(1, n)]
    return pl.pallas_call(
        functools.partial(_up_kernel, act=act, has_bias=bias is not None),
        grid=(nj, m // tm),
        in_specs=in_specs,
        out_specs=pl.BlockSpec((tm, tn), lambda j, i: (i, j)),
        out_shape=jax.ShapeDtypeStruct((m, n), out_dtype),
        scratch_shapes=[pltpu.VMEM((k, tn), BF16), pltpu.VMEM((k, tn), BF16)],
        compiler_params=_params("arbitrary", "arbitrary"),
        name="gated_up_" + act,
    )(*args)


def _res_kernel(*refs, coef, n_sub, has_bias, n_h, split):
    x_ref, w_ref = refs[:2]
    refs = refs[2:]
    if has_bias:
        b_ref, refs = refs[0], refs[1:]
    h_refs, (gate_ref, o_ref, w_s) = refs[:n_h], refs[n_h:]

    @pl.when(pl.program_id(1) == 0)
    def _():
        _cast_weight(w_ref, w_s)

    y = jnp.dot(x_ref[...], w_s[...], preferred_element_type=F32)
    if has_bias:
        y = y + b_ref[...]
    for r in range(n_sub):
        rows = slice(r * MOD_TILE, (r + 1) * MOD_TILE)
        h = _hidden_rows(h_refs, r, pl.program_id(1), n_sub, split)
        o_ref[rows, :] = h + (coef * gate_ref[r]) * y[rows, :]


def proj_residual(x, w, w_idx, src, gate, coef, bias=None):
    m, k = x.shape
    n = src.shape[1]
    assert src.shape[0] == m
    tn = _tile(n, 1024, LANES)
    if k * tn * 14 <= VMEM_LIMIT_BYTES // 2:
        tm, w_mode = _hidden_tile(src, 1024), {}
    else:
        tm, w_mode = _hidden_tile(src, 512), {"pipeline_mode": pl.Buffered(1)}
    n_sub = tm // MOD_TILE
    lead = (None,) * len(w_idx)
    in_specs = [pl.BlockSpec((tm, k), lambda j, i: (i, 0)),
                pl.BlockSpec(lead + (k, tn), lambda j, i: w_idx + (0, j), **w_mode)]
    args = [x, w]
    if bias is not None:
        in_specs.append(pl.BlockSpec((1, tn), lambda j, i: (0, j)))
        args.append(bias.reshape(1, n))
    h_specs = _hidden_specs(src, tm, tn, lambda j, i: (i, j))
    in_specs += h_specs + [pl.BlockSpec((n_sub, 1, tn), lambda j, i: (i, 0, j))]
    args += _hidden_args(src, h_specs) + [gate]
    return pl.pallas_call(
        functools.partial(_res_kernel, coef=coef, n_sub=n_sub, has_bias=bias is not None,
                          n_h=len(h_specs), split=src.split),
        grid=(n // tn, m // tm),
        in_specs=in_specs,
        out_specs=pl.BlockSpec((tm, tn), lambda j, i: (i, j)),
        out_shape=jax.ShapeDtypeStruct((m, n), F32),
        scratch_shapes=[pltpu.VMEM((k, tn), BF16)],
        compiler_params=_params("arbitrary", "arbitrary"),
        name="proj_residual",
    )(*args)


def _mla_down_kernel(*refs, lora, n_sub, n_h, split):
    h_refs, refs = refs[:n_h], refs[n_h:]
    (g_ref, sh_ref, sc_ref, wq_ref, wkv_ref, sel_ref, gq_ref, gkv_ref, cos_ref, sin_ref,
     cq_ref, ckv_ref, kpe_ref, wq_s, wkv_s, xn_s) = refs

    @pl.when(pl.program_id(0) == 0)
    def _():
        _cast_weight(wq_ref, wq_s)
        rows = _tile(wkv_ref.shape[0], CAST_ROWS, SUBLANES)

        def body(c, carry):
            r = pl.ds(pl.multiple_of(c * rows, rows), rows)
            w = wkv_ref[r, :].astype(BF16)
            wkv_s[r, :lora] = w[:, :lora]
            wkv_s[r, lora:] = jnp.dot(w[:, lora:], sel_ref[...], preferred_element_type=F32).astype(BF16)
            return carry

        lax.fori_loop(0, wkv_ref.shape[0] // rows, body, 0)

    for r in range(n_sub):
        h = _hidden_rows(h_refs, r, pl.program_id(0), n_sub, split)
        y = _rms(h, g_ref[...]) * (1.0 + sc_ref[r]) + sh_ref[r]
        xn_s[r * MOD_TILE:(r + 1) * MOD_TILE, :] = y.astype(BF16)
    x = xn_s[...]
    cq = jnp.dot(x, wq_s[...], preferred_element_type=F32)
    cq_ref[...] = _rms(cq, gq_ref[...]).astype(BF16)
    kv = jnp.dot(x, wkv_s[...], preferred_element_type=F32)
    ckv_ref[...] = _rms(kv[:, :lora], gkv_ref[...]).astype(BF16)
    kpe = kv[:, lora:lora + LANES] * cos_ref[...] + kv[:, lora + LANES:] * sin_ref[...]
    kpe_ref[...] = kpe.astype(BF16)


def mla_down(src, g, shift, scale, wdq, wdkv, gq, gkv, cos, sin):
    m, d = src.shape
    q_lora = wdq.shape[1]
    kv_lora = gkv.shape[0]
    sel = jnp.asarray(_rope_select_pair(), BF16)
    nkv = kv_lora + sel.shape[1]
    tm = _hidden_tile(src, 2 * MOD_TILE)
    n_sub = tm // MOD_TILE
    const = lambda i: (0, 0)
    row = lambda i: (i, 0)
    mod = pl.BlockSpec((n_sub, 1, d), lambda i: (i, 0, 0))
    h_specs = _hidden_specs(src, tm, d, lambda i: (i, 0))
    return pl.pallas_call(
        functools.partial(_mla_down_kernel, lora=kv_lora, n_sub=n_sub, n_h=len(h_specs), split=src.split),
        grid=(m // tm,),
        in_specs=h_specs + [pl.BlockSpec((1, d), const), mod, mod,
                            pl.BlockSpec((d, q_lora), const),
                            pl.BlockSpec(wdkv.shape, const),
                            pl.BlockSpec(sel.shape, const),
                            pl.BlockSpec((1, q_lora), const),
                            pl.BlockSpec((1, kv_lora), const),
                            pl.BlockSpec((tm, LANES), row),
                            pl.BlockSpec((tm, LANES), row)],
        out_specs=[pl.BlockSpec((tm, q_lora), row),
                   pl.BlockSpec((tm, kv_lora), row),
                   pl.BlockSpec((tm, LANES), row)],
        out_shape=[jax.ShapeDtypeStruct((m, q_lora), BF16),
                   jax.ShapeDtypeStruct((m, kv_lora), BF16),
                   jax.ShapeDtypeStruct((m, LANES), BF16)],
        scratch_shapes=[pltpu.VMEM((d, q_lora), BF16), pltpu.VMEM((d, nkv), BF16), pltpu.VMEM((tm, d), BF16)],
        compiler_params=_params("arbitrary"),
        name="mla_down",
    )(*_hidden_args(src, h_specs), g.reshape(1, d), shift, scale,
      wdq, wdkv, sel, gq.reshape(1, -1), gkv.reshape(1, -1), cos, sin)


def _wq_layout_kernel(w_ref, sel_ref, o_ref):
    o_ref[...] = jnp.dot(w_ref[...].astype(BF16), sel_ref[...], preferred_element_type=F32).astype(BF16)


def mla_wq_layout(wuq, nh):
    q_lora = wuq.shape[0]
    dh = QK_NOPE + QK_ROPE
    rope = _rope_select()
    one = np.zeros((dh, QK_NOPE + rope.shape[1]), np.float32)
    one[:QK_NOPE, :QK_NOPE] = np.eye(QK_NOPE)
    one[QK_NOPE:, QK_NOPE:] = rope
    pair = np.zeros((2 * one.shape[0], 2 * one.shape[1]), np.float32)
    pair[:one.shape[0], :one.shape[1]] = one
    pair[one.shape[0]:, one.shape[1]:] = one
    assert nh % 2 == 0
    return pl.pallas_call(
        _wq_layout_kernel,
        grid=(nh // 2,),
        in_specs=[pl.BlockSpec((q_lora, pair.shape[0]), lambda h: (0, h)),
                  pl.BlockSpec(pair.shape, lambda h: (0, 0))],
        out_specs=pl.BlockSpec((q_lora, pair.shape[1]), lambda h: (0, h)),
        out_shape=jax.ShapeDtypeStruct((q_lora, nh * one.shape[1]), BF16),
        compiler_params=_params("arbitrary"),
        name="mla_wq_layout",
    )(wuq, jnp.asarray(pair, BF16))


MLA_UP_HEADS = 2


def _mla_up_kernel(cq_ref, ckv_ref, kpe_ref, cos_ref, sin_ref, wq_ref, wkv_ref,
                   q_ref, k_ref, v_ref, *, q_scale):
    dq, dkv = QK_NOPE + LANES, QK_NOPE + V_DIM
    qf = jnp.dot(cq_ref[0], wq_ref[...], preferred_element_type=F32)
    kvf = jnp.dot(ckv_ref[0], wkv_ref[...].astype(BF16), preferred_element_type=F32)
    for hh in range(q_ref.shape[1]):
        q_pe = _rope(qf[:, hh * dq + QK_NOPE:(hh + 1) * dq], cos_ref[...], sin_ref[...])
        q_ref[0, hh, :, :QK_NOPE] = (qf[:, hh * dq:hh * dq + QK_NOPE] * q_scale).astype(BF16)
        q_ref[0, hh, :, QK_NOPE:] = (q_pe * q_scale).astype(BF16)
        k_ref[0, hh, :, :QK_NOPE] = kvf[:, hh * dkv:hh * dkv + QK_NOPE].astype(BF16)
        k_ref[0, hh, :, QK_NOPE:] = kpe_ref[0]
        v_ref[0, hh] = kvf[:, hh * dkv + QK_NOPE:(hh + 1) * dkv].astype(BF16)


def mla_up(cq, ckv, kpe, cos, sin, wq_h, wkv_h, q_scale):
    b, t, q_lora = cq.shape
    kv_lora = ckv.shape[2]
    dqk = QK_NOPE + LANES
    nh = wq_h.shape[1] // dqk
    hb = _tile(nh, MLA_UP_HEADS, 1)
    tok = lambda bi, hi: (bi, 0, 0)
    out = lambda bi, hi: (bi, hi, 0, 0)
    return pl.pallas_call(
        functools.partial(_mla_up_kernel, q_scale=q_scale),
        grid=(b, nh // hb),
        in_specs=[pl.BlockSpec((1, t, q_lora), tok),
                  pl.BlockSpec((1, t, kv_lora), tok),
                  pl.BlockSpec((1, t, LANES), tok),
                  pl.BlockSpec((t, LANES), lambda bi, hi: (0, 0)),
                  pl.BlockSpec((t, LANES), lambda bi, hi: (0, 0)),
                  pl.BlockSpec((q_lora, hb * dqk), lambda bi, hi: (0, hi)),
                  pl.BlockSpec((kv_lora, hb * (QK_NOPE + V_DIM)), lambda bi, hi: (0, hi))],
        out_specs=[pl.BlockSpec((1, hb, t, dqk), out),
                   pl.BlockSpec((1, hb, t, dqk), out),
                   pl.BlockSpec((1, hb, t, V_DIM), out)],
        out_shape=[jax.ShapeDtypeStruct((b, nh, t, dqk), BF16),
                   jax.ShapeDtypeStruct((b, nh, t, dqk), BF16),
                   jax.ShapeDtypeStruct((b, nh, t, V_DIM), BF16)],
        compiler_params=_params("arbitrary", "arbitrary"),
        name="mla_up",
    )(cq, ckv, kpe, cos, sin, wq_h, wkv_h)


ATTN_HEADS = 8
ATTN_TILE = 512
ATTN_ROWS = 256


def _attn_kernel(q_ref, k_ref, v_ref, o_ref, *, n_lat, n_ctx):
    def attend(n_rows, key_rows):
        for hh in range(q_ref.shape[1]):
            k = k_ref[0, hh, key_rows, :]
            v = v_ref[0, hh, key_rows, :]
            for r in range(n_rows // ATTN_ROWS):
                rows = slice(r * ATTN_ROWS, (r + 1) * ATTN_ROWS)
                s = lax.dot_general(q_ref[0, hh, rows, :], k, (((1,), (1,)), ((), ())),
                                    preferred_element_type=F32)
                p = jnp.exp2(s - jnp.max(s, axis=-1, keepdims=True))
                l = jnp.sum(p, axis=-1, keepdims=True)
                o = jnp.dot(p.astype(BF16), v, preferred_element_type=F32)
                o_ref[0, rows, hh * V_DIM:(hh + 1) * V_DIM] = (o / l).astype(BF16)

    @pl.when(pl.program_id(2) == 0)
    def _():
        attend(n_ctx, slice(n_lat, n_lat + n_ctx))

    @pl.when(pl.program_id(2) > 0)
    def _():
        attend(ATTN_TILE, slice(0, n_lat + n_ctx))


def attention(q, k, v, n_lat):
    b, nh, t, dqk = q.shape
    n_ctx = t - n_lat
    assert n_lat % ATTN_TILE == 0 and 0 < n_ctx <= ATTN_TILE and n_ctx % ATTN_ROWS == 0
    hb = _tile(nh, ATTN_HEADS, 1)
    n_tiles = n_lat // ATTN_TILE + 1
    tile = lambda i: (i + n_tiles - 1) % n_tiles
    return pl.pallas_call(
        functools.partial(_attn_kernel, n_lat=n_lat, n_ctx=n_ctx),
        grid=(b, nh // hb, n_tiles),
        in_specs=[pl.BlockSpec((1, hb, ATTN_TILE, dqk), lambda bi, hi, i: (bi, hi, tile(i), 0)),
                  pl.BlockSpec((1, hb, t, dqk), lambda bi, hi, i: (bi, hi, 0, 0)),
                  pl.BlockSpec((1, hb, t, V_DIM), lambda bi, hi, i: (bi, hi, 0, 0))],
        out_specs=pl.BlockSpec((1, ATTN_TILE, hb * V_DIM), lambda bi, hi, i: (bi, tile(i), hi)),
        out_shape=jax.ShapeDtypeStruct((b, t, nh * V_DIM), BF16),
        compiler_params=_params("arbitrary", "arbitrary", "arbitrary"),
        name="attention",
    )(q, k, v)


CONV_TILE = 128
CONV_GROUPS = CONV_TILE // SUBLANES


def _dwconv_kernel(prev_ref, cur_ref, next_ref, dw_ref, dwb_ref, g_ref, b_ref, o_ref, buf, cv, *, width):
    i = pl.program_id(1)
    tt = cur_ref.shape[1]
    d = cur_ref.shape[2]
    pad = width // 2
    first = i == 0
    final = i == pl.num_programs(1) - 1
    for c in range(d // LANES):
        cols = slice(c * LANES, (c + 1) * LANES)
        buf[c, 0:HALO, :] = jnp.where(first, 0.0, prev_ref[0, :, cols])
        buf[c, HALO:HALO + tt, :] = cur_ref[0, :, cols]
        buf[c, HALO + tt:, :] = jnp.where(final, 0.0, next_ref[0, :, cols])
        acc = [jnp.broadcast_to(dwb_ref[:, cols], (SUBLANES, LANES)) for _ in range(CONV_GROUPS)]
        for m in range(-pad, CONV_GROUPS + pad):
            xm = buf[c, pl.ds(HALO + m, SUBLANES, stride=CONV_GROUPS), :]
            for j in range(CONV_GROUPS):
                kk = m - j + pad
                if 0 <= kk < width:
                    acc[j] = acc[j] + xm * dw_ref[kk, :, cols]
        for j in range(CONV_GROUPS):
            cv[c, pl.ds(j, SUBLANES, stride=CONV_GROUPS), :] = acc[j]
    a = jnp.concatenate([cv[c] for c in range(d // LANES)], axis=-1)
    mu = jnp.mean(a, axis=-1, keepdims=True)
    xc = a - mu
    var = jnp.mean(xc * xc, axis=-1, keepdims=True)
    y = xc * lax.rsqrt(var + EPS) * g_ref[...] + b_ref[...]
    o_ref[0] = (y * _sigmoid(y)).astype(o_ref.dtype)


def dwconv_ln_swish(u, dw, dwb, ln_g, ln_b):
    b, t, d = u.shape
    width = dw.shape[0]
    tt = CONV_TILE
    assert t % tt == 0 and width // 2 < HALO and d % LANES == 0
    nh = tt // HALO
    last = t // HALO - 1
    vec = lambda bi, i: (0, 0)
    dw8 = jnp.broadcast_to(dw[:, None, :], (width, SUBLANES, d))
    return pl.pallas_call(
        functools.partial(_dwconv_kernel, width=width),
        grid=(b, t // tt),
        in_specs=[pl.BlockSpec((1, HALO, d), lambda bi, i: (bi, jnp.maximum(i * nh - 1, 0), 0)),
                  pl.BlockSpec((1, tt, d), lambda bi, i: (bi, i, 0)),
                  pl.BlockSpec((1, HALO, d), lambda bi, i: (bi, jnp.minimum((i + 1) * nh, last), 0)),
                  pl.BlockSpec((width, SUBLANES, d), lambda bi, i: (0, 0, 0)),
                  pl.BlockSpec((1, d), vec), pl.BlockSpec((1, d), vec), pl.BlockSpec((1, d), vec)],
        out_specs=pl.BlockSpec((1, tt, d), lambda bi, i: (bi, i, 0)),
        out_shape=jax.ShapeDtypeStruct((b, t, d), BF16),
        scratch_shapes=[pltpu.VMEM((d // LANES, tt + 2 * HALO, LANES), F32),
                        pltpu.VMEM((d // LANES, tt, LANES), F32)],
        compiler_params=_params("arbitrary", "arbitrary"),
        name="dwconv_ln_swish",
    )(u, u, u, dw8, dwb.reshape(1, d), ln_g.reshape(1, d), ln_b.reshape(1, d))


def _rope_tables(seq, ctx_len):
    n_axis = QK_ROPE // 4
    t = jnp.arange(seq, dtype=jnp.int32)
    row = (t // GRID_W).astype(F32)
    col = (t % GRID_W).astype(F32)
    freqs = ROPE_THETA ** (-jnp.arange(n_axis, dtype=F32) / n_axis)
    ang = jnp.concatenate([row[:, None] * freqs, col[:, None] * freqs], axis=-1)
    ang = jnp.concatenate([ang, jnp.zeros((ctx_len, QK_ROPE // 2), F32)], axis=0)
    cos, sin = jnp.cos(ang), jnp.sin(ang)
    zeros = jnp.zeros((seq + ctx_len, LANES - QK_ROPE), F32)
    cos_t = jnp.concatenate([cos, cos, zeros], axis=-1)
    sin_t = jnp.concatenate([-sin, sin, zeros], axis=-1)
    return cos_t, sin_t


def _rope_select():
    half = QK_ROPE // 2
    sel = np.zeros((QK_ROPE, LANES), np.float32)
    for p in range(half):
        sel[2 * p, p] = sel[2 * p, QK_ROPE + p] = 1.0
        sel[2 * p + 1, half + p] = sel[2 * p + 1, QK_ROPE + half + p] = 1.0
    return sel


def _rope_select_pair():
    sel = _rope_select()
    out = np.zeros((QK_ROPE, 2 * LANES), np.float32)
    out[:, :QK_ROPE] = sel[:, :QK_ROPE]
    out[:, LANES:LANES + QK_ROPE] = sel[:, QK_ROPE // 2:QK_ROPE // 2 + QK_ROPE]
    return out


def _rope(g, cos, sin):
    return g * cos + pltpu.roll(g, QK_ROPE // 2, axis=1) * sin


def _mod_tiles(m, layout, k, d):
    b, n_lat, n_ctx = layout
    mk = m[:, k * d:(k + 1) * d]
    parts = [jnp.broadcast_to(mk[:b, None, :], (b, n_lat, d))]
    if n_ctx:
        parts.append(jnp.broadcast_to(mk[b][None, None, :], (b, n_ctx, d)))
    return jnp.concatenate(parts, axis=1).reshape(b * (n_lat + n_ctx), 1, d)


def kernel(x, c, ctx, c_ctx, ada_w, ada_b, norm_g, ffn_w1, ffn_w3, ffn_w2, mla_wdq, mla_gq, mla_wuq, mla_wdkv, mla_gkv, mla_wukv, mla_wo, conv_w1, conv_b1, conv_dw, conv_dwb, conv_ln_g, conv_ln_b, conv_w2, conv_b2, final_g):
    b, s, d = x.shape
    lc = ctx.shape[1]
    depth = ada_w.shape[0]
    d_ff = ffn_w1.shape[-1]
    nh = mla_wo.shape[1] // V_DIM
    assert s % MOD_TILE == 0 and lc % MOD_TILE == 0 and b < COND_ROWS

    cond = jnp.concatenate([c, c_ctx[None, :], jnp.zeros((COND_ROWS - b - 1, d), F32)], axis=0)
    mods = ada_mods(cond, ada_w, ada_b)

    t_all = s + lc
    rows_all = (b, s // MOD_TILE, lc // MOD_TILE)
    rows_lat = (b, s // MOD_TILE, 0)

    src = Hidden((x, ctx), t_all)
    combined = True

    def ffn(src, m, rows, i, k, f):
        xn = norm_mod(src, norm_g[i, k], _mod_tiles(m, rows, 3 * k, d), _mod_tiles(m, rows, 3 * k + 1, d))
        g = gated_up(xn, ffn_w1, (i, f), 0, ffn_w3, (i, f), 0, d_ff, "swiglu")
        return _flat(proj_residual(g, ffn_w2, (i, f), src, _mod_tiles(m, rows, 3 * k + 2, d), FFN_RES_WEIGHT))

    def latent_only(src):
        if src.split:
            return Hidden((src.arrays[0],), s)
        return Hidden((src.arrays[0].reshape(b, t_all, d),), s)

    for i in range(depth):
        last = i == depth - 1
        mixer = i % 2
        j = i // 2
        ctx_into_mixer = (not last) or mixer == 0
        ctx_out = not last
        m = mods[i]
        if combined and not ctx_into_mixer:
            src, combined = latent_only(src), False
        rows = rows_all if combined else rows_lat

        src = ffn(src, m, rows, i, 0, 0)

        shift, scale, gate = (_mod_tiles(m, rows, k, d) for k in (3, 4, 5))
        if mixer == 0:
            assert combined
            cos_t, sin_t = _rope_tables(s, lc)
            cq, ckv, kpe = mla_down(src, norm_g[i, 1], shift, scale, mla_wdq[j], mla_wdkv[j], mla_gq[j], mla_gkv[j],
                                    jnp.tile(cos_t, (b, 1)), jnp.tile(sin_t, (b, 1)))
            wq_h = mla_wq_layout(mla_wuq[j], nh)
            wkv_h = mla_wukv[j]
            q_scale = (QK_NOPE + QK_ROPE) ** -0.5 * math.log2(math.e)
            q, k, v = mla_up(cq.reshape(b, t_all, -1), ckv.reshape(b, t_all, -1), kpe.reshape(b, t_all, -1),
                             cos_t, sin_t, wq_h, wkv_h, q_scale)
            o = attention(q, k, v, s).reshape(b * t_all, nh * V_DIM)
            src = _flat(proj_residual(o, mla_wo, (j,), src, gate, 1.0))
        else:
            assert not combined
            xn = norm_mod(src, norm_g[i, 1], shift, scale)
            u = gated_up(xn, conv_w1, (j,), 0, conv_w1, (j,), 1, d, "glu",
                         bias=(conv_b1[j, :d], conv_b1[j, d:]), out_dtype=F32)
            vv = dwconv_ln_swish(u.reshape(b, s, d), conv_dw[j], conv_dwb[j], conv_ln_g[j], conv_ln_b[j])
            src = _flat(proj_residual(vv.reshape(-1, d), conv_w2, (j,), src, gate, 1.0, bias=conv_b2[j]))

        if combined and not ctx_out:
            src, combined = latent_only(src), False
            rows = rows_lat
        src = ffn(src, m, rows, i, 2, 1)

    if combined:
        src = latent_only(src)
    return norm_mod(src, final_g, out_dtype=F32).reshape(b, s, d)
```

```python
import functools
import math
from typing import NamedTuple

import numpy as np
import jax
import jax.numpy as jnp
from jax import lax
from jax.experimental import pallas as pl
from jax.experimental.pallas import tpu as pltpu

F32 = jnp.float32
BF16 = jnp.bfloat16

GRID_W = 64
QK_NOPE = 128
QK_ROPE = 64
V_DIM = 128
ROPE_THETA = 10000.0
FFN_RES_WEIGHT = 0.5
N_MOD = 9
EPS = 1e-6

LANES = 128
SUBLANES = 8
VMEM_LIMIT_BYTES = 58 * 1024 * 1024

MOD_TILE = 256
COND_ROWS = 8
HALO = 16
CAST_ROWS = 256


def _params(*sem):
    return pltpu.CompilerParams(dimension_semantics=sem, vmem_limit_bytes=VMEM_LIMIT_BYTES)


def _tile(n, pref, unit):
    if n <= pref:
        return n
    t = (pref // unit) * unit
    while t > unit and n % t:
        t -= unit
    assert n % t == 0, (n, pref, unit)
    return t


def _cast_weight(src_ref, dst_ref):
    k = src_ref.shape[0]
    rows = _tile(k, CAST_ROWS, SUBLANES)

    def body(c, carry):
        s = pl.multiple_of(c * rows, rows)
        dst_ref[pl.ds(s, rows), :] = src_ref[pl.ds(s, rows), :].astype(BF16)
        return carry

    lax.fori_loop(0, k // rows, body, 0)


def _sigmoid(x):
    return 1.0 / (1.0 + jnp.exp(-x))


def _ada_kernel(c_ref, w_ref, b_ref, o_ref):
    c = c_ref[...]
    s = (c * _sigmoid(c)).astype(BF16)
    o_ref[0] = jnp.dot(s, w_ref[0].astype(BF16), preferred_element_type=F32) + b_ref[0]


def ada_mods(cond, ada_w, ada_b):
    depth, d, n = ada_w.shape
    tn = _tile(n, 1024, LANES)
    return pl.pallas_call(
        _ada_kernel,
        grid=(depth, n // tn),
        in_specs=[
            pl.BlockSpec((COND_ROWS, d), lambda l, j: (0, 0)),
            pl.BlockSpec((1, d, tn), lambda l, j: (l, 0, j)),
            pl.BlockSpec((1, 1, tn), lambda l, j: (l, 0, j)),
        ],
        out_specs=pl.BlockSpec((1, COND_ROWS, tn), lambda l, j: (l, 0, j)),
        out_shape=jax.ShapeDtypeStruct((depth, COND_ROWS, n), F32),
        compiler_params=_params("arbitrary", "arbitrary"),
        name="ada_mods",
    )(cond, ada_w, ada_b.reshape(depth, 1, n))


class Hidden(NamedTuple):
    arrays: tuple
    t_use: int

    @property
    def split(self):
        if len(self.arrays) == 1:
            return None
        return (self.arrays[0].shape[1] // MOD_TILE, self.arrays[1].shape[1] // MOD_TILE)

    @property
    def shape(self):
        return (self.arrays[0].shape[0] * self.t_use, self.arrays[0].shape[2])


def _flat(h):
    return Hidden((h.reshape(1, *h.shape),), h.shape[0])


def _hidden_tile(src, pref):
    return _tile(src.shape[0] if src.split else src.t_use, pref, MOD_TILE)


def _hidden_specs(src, tm, cols, tile_col):
    if src.split is None:
        nt = src.t_use // tm

        def hmap(*g):
            i, j = tile_col(*g)
            return (i // nt, i % nt, j)

        return [pl.BlockSpec((None, tm, cols), hmap)]
    n_lat, n_ctx = src.split
    n_sub = tm // MOD_TILE
    specs = []
    for r in range(n_sub):
        def xmap(*g, r=r):
            i, j = tile_col(*g)
            t = i * n_sub + r
            return (t // (n_lat + n_ctx), jnp.minimum(t % (n_lat + n_ctx), n_lat - 1), j)

        def cmap(*g, r=r):
            i, j = tile_col(*g)
            t = i * n_sub + r
            return (t // (n_lat + n_ctx), jnp.maximum(t % (n_lat + n_ctx) - n_lat, 0), j)

        specs += [pl.BlockSpec((None, MOD_TILE, cols), xmap), pl.BlockSpec((None, MOD_TILE, cols), cmap)]
    return specs


def _hidden_args(src, specs):
    return list(src.arrays) * (len(specs) // len(src.arrays))


def _hidden_rows(h_refs, r, row_tile, n_sub, split):
    if split is None:
        return h_refs[0][r * MOD_TILE:(r + 1) * MOD_TILE, :]
    n_lat, n_ctx = split
    is_ctx = (row_tile * n_sub + r) % (n_lat + n_ctx) >= n_lat
    return jnp.where(is_ctx, h_refs[2 * r + 1][...], h_refs[2 * r][...])


def _rms(x, g):
    return x * lax.rsqrt(jnp.mean(x * x, axis=-1, keepdims=True) + EPS) * g


def _norm_kernel(*refs, modulate, n_sub, n_h, split):
    h_refs, refs = refs[:n_h], refs[n_h:]
    if modulate:
        g_ref, sh_ref, sc_ref, o_ref = refs
    else:
        g_ref, o_ref = refs
    for r in range(n_sub):
        y = _rms(_hidden_rows(h_refs, r, pl.program_id(0), n_sub, split), g_ref[...])
        if modulate:
            y = y * (1.0 + sc_ref[r]) + sh_ref[r]
        o_ref[r * MOD_TILE:(r + 1) * MOD_TILE, :] = y.astype(o_ref.dtype)


def norm_mod(src, g, shift=None, scale=None, out_dtype=BF16):
    m, d = src.shape
    tm = _hidden_tile(src, 4 * MOD_TILE)
    n_sub = tm // MOD_TILE
    modulate = shift is not None
    h_specs = _hidden_specs(src, tm, d, lambda i: (i, 0))
    in_specs = h_specs + [pl.BlockSpec((1, d), lambda i: (0, 0))]
    args = _hidden_args(src, h_specs) + [g.reshape(1, d)]
    if modulate:
        in_specs += [pl.BlockSpec((n_sub, 1, d), lambda i: (i, 0, 0))] * 2
        args += [shift, scale]
    return pl.pallas_call(
        functools.partial(_norm_kernel, modulate=modulate, n_sub=n_sub, n_h=len(h_specs), split=src.split),
        grid=(m // tm,),
        in_specs=in_specs,
        out_specs=pl.BlockSpec((tm, d), lambda i: (i, 0)),
        out_shape=jax.ShapeDtypeStruct((m, d), out_dtype),
        compiler_params=_params("arbitrary"),
        name="norm_mod",
    )(*args)


def _up_kernel(*refs, act, has_bias):
    if has_bias:
        x_ref, wa_ref, wb_ref, ba_ref, bb_ref, o_ref, wa_s, wb_s = refs
    else:
        x_ref, wa_ref, wb_ref, o_ref, wa_s, wb_s = refs

    @pl.when(pl.program_id(1) == 0)
    def _():
        _cast_weight(wa_ref, wa_s)
        _cast_weight(wb_ref, wb_s)

    x = x_ref[...]
    a = jnp.dot(x, wa_s[...], preferred_element_type=F32)
    b = jnp.dot(x, wb_s[...], preferred_element_type=F32)
    if has_bias:
        a = a + ba_ref[...]
        b = b + bb_ref[...]
    if act == "swiglu":
        o = (a * _sigmoid(a)) * b
    else:
        o = a * _sigmoid(b)
    o_ref[...] = o.astype(o_ref.dtype)


def gated_up(x, wa, wa_idx, wa_col, wb, wb_idx, wb_col, n, act, bias=None, out_dtype=BF16):
    m, k = x.shape
    tm = _tile(m, 1024, MOD_TILE)
    tn = _tile(n, 512, LANES)
    nj = n // tn

    def wspec(idx, col):
        lead = (None,) * len(idx)
        return pl.BlockSpec(lead + (k, tn), lambda j, i: idx + (0, col * nj + j))

    in_specs = [pl.BlockSpec((tm, k), lambda j, i: (i, 0)), wspec(wa_idx, wa_col), wspec(wb_idx, wb_col)]
    args = [x, wa, wb]
    if bias is not None:
        ba, bb = bias
        in_specs += [pl.BlockSpec((1, tn), lambda j, i: (0, j))] * 2
        args += [ba.reshape(1, n), bb.reshape(1, n)]
    return pl.pallas_call(
        functools.partial(_up_kernel, act=act, has_bias=bias is not None),
        grid=(nj, m // tm),
        in_specs=in_specs,
        out_specs=pl.BlockSpec((tm, tn), lambda j, i: (i, j)),
        out_shape=jax.ShapeDtypeStruct((m, n), out_dtype),
        scratch_shapes=[pltpu.VMEM((k, tn), BF16), pltpu.VMEM((k, tn), BF16)],
        compiler_params=_params("arbitrary", "arbitrary"),
        name="gated_up_" + act,
    )(*args)


def _res_kernel(*refs, coef, n_sub, has_bias, n_h, split):
    x_ref, w_ref = refs[:2]
    refs = refs[2:]
    if has_bias:
        b_ref, refs = refs[0], refs[1:]
    h_refs, (gate_ref, o_ref, w_s) = refs[:n_h], refs[n_h:]

    @pl.when(pl.program_id(1) == 0)
    def _():
        _cast_weight(w_ref, w_s)

    y = jnp.dot(x_ref[...], w_s[...], preferred_element_type=F32)
    if has_bias:
        y = y + b_ref[...]
    for r in range(n_sub):
        rows = slice(r * MOD_TILE, (r + 1) * MOD_TILE)
        h = _hidden_rows(h_refs, r, pl.program_id(1), n_sub, split)
        o_ref[rows, :] = h + (coef * gate_ref[r]) * y[rows, :]


def proj_residual(x, w, w_idx, src, gate, coef, bias=None):
    m, k = x.shape
    n = src.shape[1]
    assert src.shape[0] == m
    tn = _tile(n, 1024, LANES)
    if k * tn * 14 <= VMEM_LIMIT_BYTES // 2:
        tm, w_mode = _hidden_tile(src, 1024), {}
    else:
        tm, w_mode = _hidden_tile(src, 512), {"pipeline_mode": pl.Buffered(1)}
    n_sub = tm // MOD_TILE
    lead = (None,) * len(w_idx)
    in_specs = [pl.BlockSpec((tm, k), lambda j, i: (i, 0)),
                pl.BlockSpec(lead + (k, tn), lambda j, i: w_idx + (0, j), **w_mode)]
    args = [x, w]
    if bias is not None:
        in_specs.append(pl.BlockSpec((1, tn), lambda j, i: (0, j)))
        args.append(bias.reshape(1, n))
    h_specs = _hidden_specs(src, tm, tn, lambda j, i: (i, j))
    in_specs += h_specs + [pl.BlockSpec((n_sub, 1, tn), lambda j, i: (i, 0, j))]
    args += _hidden_args(src, h_specs) + [gate]
    return pl.pallas_call(
        functools.partial(_res_kernel, coef=coef, n_sub=n_sub, has_bias=bias is not None,
                          n_h=len(h_specs), split=src.split),
        grid=(n // tn, m // tm),
        in_specs=in_specs,
        out_specs=pl.BlockSpec((tm, tn), lambda j, i: (i, j)),
        out_shape=jax.ShapeDtypeStruct((m, n), F32),
        scratch_shapes=[pltpu.VMEM((k, tn), BF16)],
        compiler_params=_params("arbitrary", "arbitrary"),
        name="proj_residual",
    )(*args)


def _mla_down_kernel(*refs, lora, n_sub, n_h, split):
    h_refs, refs = refs[:n_h], refs[n_h:]
    (g_ref, sh_ref, sc_ref, wq_ref, wkv_ref, sel_ref, gq_ref, gkv_ref, cos_ref, sin_ref,
     cq_ref, ckv_ref, kpe_ref, wq_s, wkv_s, xn_s) = refs

    @pl.when(pl.program_id(0) == 0)
    def _():
        _cast_weight(wq_ref, wq_s)
        rows = _tile(wkv_ref.shape[0], CAST_ROWS, SUBLANES)

        def body(c, carry):
            r = pl.ds(pl.multiple_of(c * rows, rows), rows)
            w = wkv_ref[r, :].astype(BF16)
            wkv_s[r, :lora] = w[:, :lora]
            wkv_s[r, lora:] = jnp.dot(w[:, lora:], sel_ref[...], preferred_element_type=F32).astype(BF16)
            return carry

        lax.fori_loop(0, wkv_ref.shape[0] // rows, body, 0)

    for r in range(n_sub):
        h = _hidden_rows(h_refs, r, pl.program_id(0), n_sub, split)
        y = _rms(h, g_ref[...]) * (1.0 + sc_ref[r]) + sh_ref[r]
        xn_s[r * MOD_TILE:(r + 1) * MOD_TILE, :] = y.astype(BF16)
    x = xn_s[...]
    cq = jnp.dot(x, wq_s[...], preferred_element_type=F32)
    cq_ref[...] = _rms(cq, gq_ref[...]).astype(BF16)
    kv = jnp.dot(x, wkv_s[...], preferred_element_type=F32)
    ckv_ref[...] = _rms(kv[:, :lora], gkv_ref[...]).astype(BF16)
    kpe = kv[:, lora:lora + LANES] * cos_ref[...] + kv[:, lora + LANES:] * sin_ref[...]
    kpe_ref[...] = kpe.astype(BF16)


def mla_down(src, g, shift, scale, wdq, wdkv, gq, gkv, cos, sin):
    m, d = src.shape
    q_lora = wdq.shape[1]
    kv_lora = gkv.shape[0]
    sel = jnp.asarray(_rope_select_pair(), BF16)
    nkv = kv_lora + sel.shape[1]
    tm = _hidden_tile(src, 2 * MOD_TILE)
    n_sub = tm // MOD_TILE
    const = lambda i: (0, 0)
    row = lambda i: (i, 0)
    mod = pl.BlockSpec((n_sub, 1, d), lambda i: (i, 0, 0))
    h_specs = _hidden_specs(src, tm, d, lambda i: (i, 0))
    return pl.pallas_call(
        functools.partial(_mla_down_kernel, lora=kv_lora, n_sub=n_sub, n_h=len(h_specs), split=src.split),
        grid=(m // tm,),
        in_specs=h_specs + [pl.BlockSpec((1, d), const), mod, mod,
                            pl.BlockSpec((d, q_lora), const),
                            pl.BlockSpec(wdkv.shape, const),
                            pl.BlockSpec(sel.shape, const),
                            pl.BlockSpec((1, q_lora), const),
                            pl.BlockSpec((1, kv_lora), const),
                            pl.BlockSpec((tm, LANES), row),
                            pl.BlockSpec((tm, LANES), row)],
        out_specs=[pl.BlockSpec((tm, q_lora), row),
                   pl.BlockSpec((tm, kv_lora), row),
                   pl.BlockSpec((tm, LANES), row)],
        out_shape=[jax.ShapeDtypeStruct((m, q_lora), BF16),
                   jax.ShapeDtypeStruct((m, kv_lora), BF16),
                   jax.ShapeDtypeStruct((m, LANES), BF16)],
        scratch_shapes=[pltpu.VMEM((d, q_lora), BF16), pltpu.VMEM((d, nkv), BF16), pltpu.VMEM((tm, d), BF16)],
        compiler_params=_params("arbitrary"),
        name="mla_down",
    )(*_hidden_args(src, h_specs), g.reshape(1, d), shift, scale,
      wdq, wdkv, sel, gq.reshape(1, -1), gkv.reshape(1, -1), cos, sin)


def _wq_layout_kernel(w_ref, sel_ref, o_ref):
    o_ref[...] = jnp.dot(w_ref[...].astype(BF16), sel_ref[...], preferred_element_type=F32).astype(BF16)


def mla_wq_layout(wuq, nh):
    q_lora = wuq.shape[0]
    dh = QK_NOPE + QK_ROPE
    rope = _rope_select()
    one = np.zeros((dh, QK_NOPE + rope.shape[1]), np.float32)
    one[:QK_NOPE, :QK_NOPE] = np.eye(QK_NOPE)
    one[QK_NOPE:, QK_NOPE:] = rope
    pair = np.zeros((2 * one.shape[0], 2 * one.shape[1]), np.float32)
    pair[:one.shape[0], :one.shape[1]] = one
    pair[one.shape[0]:, one.shape[1]:] = one
    assert nh % 2 == 0
    return pl.pallas_call(
        _wq_layout_kernel,
        grid=(nh // 2,),
        in_specs=[pl.BlockSpec((q_lora, pair.shape[0]), lambda h: (0, h)),
                  pl.BlockSpec(pair.shape, lambda h: (0, 0))],
        out_specs=pl.BlockSpec((q_lora, pair.shape[1]), lambda h: (0, h)),
        out_shape=jax.ShapeDtypeStruct((q_lora, nh * one.shape[1]), BF16),
        compiler_params=_params("arbitrary"),
        name="mla_wq_layout",
    )(wuq, jnp.asarray(pair, BF16))


MLA_UP_HEADS = 2


def _mla_up_kernel(cq_ref, ckv_ref, kpe_ref, cos_ref, sin_ref, wq_ref, wkv_ref,
                   q_ref, k_ref, v_ref, *, q_scale):
    dq, dkv = QK_NOPE + LANES, QK_NOPE + V_DIM
    qf = jnp.dot(cq_ref[0], wq_ref[...], preferred_element_type=F32)
    kvf = jnp.dot(ckv_ref[0], wkv_ref[...].astype(BF16), preferred_element_type=F32)
    for hh in range(q_ref.shape[1]):
        q_pe = _rope(qf[:, hh * dq + QK_NOPE:(hh + 1) * dq], cos_ref[...], sin_ref[...])
        q_ref[0, hh, :, :QK_NOPE] = (qf[:, hh * dq:hh * dq + QK_NOPE] * q_scale).astype(BF16)
        q_ref[0, hh, :, QK_NOPE:] = (q_pe * q_scale).astype(BF16)
        k_ref[0, hh, :, :QK_NOPE] = kvf[:, hh * dkv:hh * dkv + QK_NOPE].astype(BF16)
        k_ref[0, hh, :, QK_NOPE:] = kpe_ref[0]
        v_ref[0, hh] = kvf[:, hh * dkv + QK_NOPE:(hh + 1) * dkv].astype(BF16)


def mla_up(cq, ckv, kpe, cos, sin, wq_h, wkv_h, q_scale):
    b, t, q_lora = cq.shape
    kv_lora = ckv.shape[2]
    dqk = QK_NOPE + LANES
    nh = wq_h.shape[1] // dqk
    hb = _tile(nh, MLA_UP_HEADS, 1)
    tok = lambda bi, hi: (bi, 0, 0)
    out = lambda bi, hi: (bi, hi, 0, 0)
    return pl.pallas_call(
        functools.partial(_mla_up_kernel, q_scale=q_scale),
        grid=(b, nh // hb),
        in_specs=[pl.BlockSpec((1, t, q_lora), tok),
                  pl.BlockSpec((1, t, kv_lora), tok),
                  pl.BlockSpec((1, t, LANES), tok),
                  pl.BlockSpec((t, LANES), lambda bi, hi: (0, 0)),
                  pl.BlockSpec((t, LANES), lambda bi, hi: (0, 0)),
                  pl.BlockSpec((q_lora, hb * dqk), lambda bi, hi: (0, hi)),
                  pl.BlockSpec((kv_lora, hb * (QK_NOPE + V_DIM)), lambda bi, hi: (0, hi))],
        out_specs=[pl.BlockSpec((1, hb, t, dqk), out),
                   pl.BlockSpec((1, hb, t, dqk), out),
                   pl.BlockSpec((1, hb, t, V_DIM), out)],
        out_shape=[jax.ShapeDtypeStruct((b, nh, t, dqk), BF16),
                   jax.ShapeDtypeStruct((b, nh, t, dqk), BF16),
                   jax.ShapeDtypeStruct((b, nh, t, V_DIM), BF16)],
        compiler_params=_params("arbitrary", "arbitrary"),
        name="mla_up",
    )(cq, ckv, kpe, cos, sin, wq_h, wkv_h)


ATTN_HEADS = 8
ATTN_TILE = 512
ATTN_ROWS = 256


def _attn_kernel(q_ref, k_ref, v_ref, o_ref, *, n_lat, n_ctx):
    def attend(n_rows, key_rows):
        for hh in range(q_ref.shape[1]):
            k = k_ref[0, hh, key_rows, :]
            v = v_ref[0, hh, key_rows, :]
            v1 = jnp.concatenate([v, jnp.ones_like(v)], axis=-1)
            for r in range(n_rows // ATTN_ROWS):
                rows = slice(r * ATTN_ROWS, (r + 1) * ATTN_ROWS)
                s = lax.dot_general(q_ref[0, hh, rows, :], k, (((1,), (1,)), ((), ())),
                                    preferred_element_type=F32)
                p = jnp.exp2(s - jnp.max(s, axis=-1, keepdims=True))
                o = jnp.dot(p.astype(BF16), v1, preferred_element_type=F32)
                o_ref[0, rows, hh * V_DIM:(hh + 1) * V_DIM] = (o[:, :V_DIM] / o[:, V_DIM:]).astype(BF16)

    @pl.when(pl.program_id(2) == 0)
    def _():
        attend(n_ctx, slice(n_lat, n_lat + n_ctx))

    @pl.when(pl.program_id(2) > 0)
    def _():
        attend(ATTN_TILE, slice(0, n_lat + n_ctx))


def attention(q, k, v, n_lat):
    b, nh, t, dqk = q.shape
    n_ctx = t - n_lat
    assert n_lat % ATTN_TILE == 0 and 0 < n_ctx <= ATTN_TILE and n_ctx % ATTN_ROWS == 0
    hb = _tile(nh, ATTN_HEADS, 1)
    n_tiles = n_lat // ATTN_TILE + 1
    tile = lambda i: (i + n_tiles - 1) % n_tiles
    return pl.pallas_call(
        functools.partial(_attn_kernel, n_lat=n_lat, n_ctx=n_ctx),
        grid=(b, nh // hb, n_tiles),
        in_specs=[pl.BlockSpec((1, hb, ATTN_TILE, dqk), lambda bi, hi, i: (bi, hi, tile(i), 0)),
                  pl.BlockSpec((1, hb, t, dqk), lambda bi, hi, i: (bi, hi, 0, 0)),
                  pl.BlockSpec((1, hb, t, V_DIM), lambda bi, hi, i: (bi, hi, 0, 0))],
        out_specs=pl.BlockSpec((1, ATTN_TILE, hb * V_DIM), lambda bi, hi, i: (bi, tile(i), hi)),
        out_shape=jax.ShapeDtypeStruct((b, t, nh * V_DIM), BF16),
        compiler_params=_params("arbitrary", "arbitrary", "arbitrary"),
        name="attention",
    )(q, k, v)


CONV_TILE = 128
CONV_GROUPS = CONV_TILE // SUBLANES


def _dwconv_kernel(prev_ref, cur_ref, next_ref, dw_ref, dwb_ref, g_ref, b_ref, o_ref, buf, cv, *, width):
    i = pl.program_id(1)
    tt = cur_ref.shape[1]
    d = cur_ref.shape[2]
    pad = width // 2
    first = i == 0
    final = i == pl.num_programs(1) - 1
    for c in range(d // LANES):
        cols = slice(c * LANES, (c + 1) * LANES)
        buf[c, 0:HALO, :] = jnp.where(first, 0.0, prev_ref[0, :, cols])
        buf[c, HALO:HALO + tt, :] = cur_ref[0, :, cols]
        buf[c, HALO + tt:, :] = jnp.where(final, 0.0, next_ref[0, :, cols])
        acc = [jnp.broadcast_to(dwb_ref[:, cols], (SUBLANES, LANES)) for _ in range(CONV_GROUPS)]
        for m in range(-pad, CONV_GROUPS + pad):
            xm = buf[c, pl.ds(HALO + m, SUBLANES, stride=CONV_GROUPS), :]
            for j in range(CONV_GROUPS):
                kk = m - j + pad
                if 0 <= kk < width:
                    acc[j] = acc[j] + xm * dw_ref[kk, :, cols]
        for j in range(CONV_GROUPS):
            cv[c, pl.ds(j, SUBLANES, stride=CONV_GROUPS), :] = acc[j]
    a = jnp.concatenate([cv[c] for c in range(d // LANES)], axis=-1)
    mu = jnp.mean(a, axis=-1, keepdims=True)
    xc = a - mu
    var = jnp.mean(xc * xc, axis=-1, keepdims=True)
    y = xc * lax.rsqrt(var + EPS) * g_ref[...] + b_ref[...]
    o_ref[0] = (y * _sigmoid(y)).astype(o_ref.dtype)


def dwconv_ln_swish(u, dw, dwb, ln_g, ln_b):
    b, t, d = u.shape
    width = dw.shape[0]
    tt = CONV_TILE
    assert t % tt == 0 and width // 2 < HALO and d % LANES == 0
    nh = tt // HALO
    last = t // HALO - 1
    vec = lambda bi, i: (0, 0)
    dw8 = jnp.broadcast_to(dw[:, None, :], (width, SUBLANES, d))
    return pl.pallas_call(
        functools.partial(_dwconv_kernel, width=width),
        grid=(b, t // tt),
        in_specs=[pl.BlockSpec((1, HALO, d), lambda bi, i: (bi, jnp.maximum(i * nh - 1, 0), 0)),
                  pl.BlockSpec((1, tt, d), lambda bi, i: (bi, i, 0)),
                  pl.BlockSpec((1, HALO, d), lambda bi, i: (bi, jnp.minimum((i + 1) * nh, last), 0)),
                  pl.BlockSpec((width, SUBLANES, d), lambda bi, i: (0, 0, 0)),
                  pl.BlockSpec((1, d), vec), pl.BlockSpec((1, d), vec), pl.BlockSpec((1, d), vec)],
        out_specs=pl.BlockSpec((1, tt, d), lambda bi, i: (bi, i, 0)),
        out_shape=jax.ShapeDtypeStruct((b, t, d), BF16),
        scratch_shapes=[pltpu.VMEM((d // LANES, tt + 2 * HALO, LANES), F32),
                        pltpu.VMEM((d // LANES, tt, LANES), F32)],
        compiler_params=_params("arbitrary", "arbitrary"),
        name="dwconv_ln_swish",
    )(u, u, u, dw8, dwb.reshape(1, d), ln_g.reshape(1, d), ln_b.reshape(1, d))


def _rope_tables(seq, ctx_len):
    n_axis = QK_ROPE // 4
    t = jnp.arange(seq, dtype=jnp.int32)
    row = (t // GRID_W).astype(F32)
    col = (t % GRID_W).astype(F32)
    freqs = ROPE_THETA ** (-jnp.arange(n_axis, dtype=F32) / n_axis)
    ang = jnp.concatenate([row[:, None] * freqs, col[:, None] * freqs], axis=-1)
    ang = jnp.concatenate([ang, jnp.zeros((ctx_len, QK_ROPE // 2), F32)], axis=0)
    cos, sin = jnp.cos(ang), jnp.sin(ang)
    zeros = jnp.zeros((seq + ctx_len, LANES - QK_ROPE), F32)
    cos_t = jnp.concatenate([cos, cos, zeros], axis=-1)
    sin_t = jnp.concatenate([-sin, sin, zeros], axis=-1)
    return cos_t, sin_t


def _rope_select():
    half = QK_ROPE // 2
    sel = np.zeros((QK_ROPE, LANES), np.float32)
    for p in range(half):
        sel[2 * p, p] = sel[2 * p, QK_ROPE + p] = 1.0
        sel[2 * p + 1, half + p] = sel[2 * p + 1, QK_ROPE + half + p] = 1.0
    return sel


def _rope_select_pair():
    sel = _rope_select()
    out = np.zeros((QK_ROPE, 2 * LANES), np.float32)
    out[:, :QK_ROPE] = sel[:, :QK_ROPE]
    out[:, LANES:LANES + QK_ROPE] = sel[:, QK_ROPE // 2:QK_ROPE // 2 + QK_ROPE]
    return out


def _rope(g, cos, sin):
    return g * cos + pltpu.roll(g, QK_ROPE // 2, axis=1) * sin


def _mod_tiles(m, layout, k, d):
    b, n_lat, n_ctx = layout
    mk = m[:, k * d:(k + 1) * d]
    parts = [jnp.broadcast_to(mk[:b, None, :], (b, n_lat, d))]
    if n_ctx:
        parts.append(jnp.broadcast_to(mk[b][None, None, :], (b, n_ctx, d)))
    return jnp.concatenate(parts, axis=1).reshape(b * (n_lat + n_ctx), 1, d)


def kernel(x, c, ctx, c_ctx, ada_w, ada_b, norm_g, ffn_w1, ffn_w3, ffn_w2, mla_wdq, mla_gq, mla_wuq, mla_wdkv, mla_gkv, mla_wukv, mla_wo, conv_w1, conv_b1, conv_dw, conv_dwb, conv_ln_g, conv_ln_b, conv_w2, conv_b2, final_g):
    b, s, d = x.shape
    lc = ctx.shape[1]
    depth = ada_w.shape[0]
    d_ff = ffn_w1.shape[-1]
    nh = mla_wo.shape[1] // V_DIM
    assert s % MOD_TILE == 0 and lc % MOD_TILE == 0 and b < COND_ROWS

    cond = jnp.concatenate([c, c_ctx[None, :], jnp.zeros((COND_ROWS - b - 1, d), F32)], axis=0)
    mods = ada_mods(cond, ada_w, ada_b)

    t_all = s + lc
    rows_all = (b, s // MOD_TILE, lc // MOD_TILE)
    rows_lat = (b, s // MOD_TILE, 0)

    src = Hidden((x, ctx), t_all)
    combined = True

    def ffn(src, m, rows, i, k, f):
        xn = norm_mod(src, norm_g[i, k], _mod_tiles(m, rows, 3 * k, d), _mod_tiles(m, rows, 3 * k + 1, d))
        g = gated_up(xn, ffn_w1, (i, f), 0, ffn_w3, (i, f), 0, d_ff, "swiglu")
        return _flat(proj_residual(g, ffn_w2, (i, f), src, _mod_tiles(m, rows, 3 * k + 2, d), FFN_RES_WEIGHT))

    def latent_only(src):
        if src.split:
            return Hidden((src.arrays[0],), s)
        return Hidden((src.arrays[0].reshape(b, t_all, d),), s)

    for i in range(depth):
        last = i == depth - 1
        mixer = i % 2
        j = i // 2
        ctx_into_mixer = (not last) or mixer == 0
        ctx_out = not last
        m = mods[i]
        if combined and not ctx_into_mixer:
            src, combined = latent_only(src), False
        rows = rows_all if combined else rows_lat

        src = ffn(src, m, rows, i, 0, 0)

        shift, scale, gate = (_mod_tiles(m, rows, k, d) for k in (3, 4, 5))
        if mixer == 0:
            assert combined
            cos_t, sin_t = _rope_tables(s, lc)
            cq, ckv, kpe = mla_down(src, norm_g[i, 1], shift, scale, mla_wdq[j], mla_wdkv[j], mla_gq[j], mla_gkv[j],
                                    jnp.tile(cos_t, (b, 1)), jnp.tile(sin_t, (b, 1)))
            wq_h = mla_wq_layout(mla_wuq[j], nh)
            wkv_h = mla_wukv[j]
            q_scale = (QK_NOPE + QK_ROPE) ** -0.5 * math.log2(math.e)
            q, k, v = mla_up(cq.reshape(b, t_all, -1), ckv.reshape(b, t_all, -1), kpe.reshape(b, t_all, -1),
                             cos_t, sin_t, wq_h, wkv_h, q_scale)
            o = attention(q, k, v, s).reshape(b * t_all, nh * V_DIM)
            src = _flat(proj_residual(o, mla_wo, (j,), src, gate, 1.0))
        else:
            assert not combined
            xn = norm_mod(src, norm_g[i, 1], shift, scale)
            u = gated_up(xn, conv_w1, (j,), 0, conv_w1, (j,), 1, d, "glu",
                         bias=(conv_b1[j, :d], conv_b1[j, d:]), out_dtype=F32)
            vv = dwconv_ln_swish(u.reshape(b, s, d), conv_dw[j], conv_dwb[j], conv_ln_g[j], conv_ln_b[j])
            src = _flat(proj_residual(vv.reshape(-1, d), conv_w2, (j,), src, gate, 1.0, bias=conv_b2[j]))

        if combined and not ctx_out:
            src, combined = latent_only(src), False
            rows = rows_lat
        src = ffn(src, m, rows, i, 2, 1)

    if combined:
        src = latent_only(src)
    return norm_mod(src, final_g, out_dtype=F32).reshape(b, s, d)
```

```python
import functools
import math
from typing import NamedTuple

import numpy as np
import jax
import jax.numpy as jnp
from jax import lax
from jax.experimental import pallas as pl
from jax.experimental.pallas import tpu as pltpu

F32 = jnp.float32
BF16 = jnp.bfloat16

GRID_W = 64
QK_NOPE = 128
QK_ROPE = 64
V_DIM = 128
ROPE_THETA = 10000.0
FFN_RES_WEIGHT = 0.5
N_MOD = 9
EPS = 1e-6

LANES = 128
SUBLANES = 8
VMEM_LIMIT_BYTES = 58 * 1024 * 1024

MOD_TILE = 256
COND_ROWS = 8
HALO = 16
CAST_ROWS = 256

MXU_TILE = 1024
UP_COLS = 512
DEEP_ROWS = 512
NORM_ROWS = 1024
MLA_DOWN_ROWS = 512


def _params(*sem):
    return pltpu.CompilerParams(dimension_semantics=sem, vmem_limit_bytes=VMEM_LIMIT_BYTES)


def _tile(n, pref, unit):
    if n <= pref:
        return n
    t = (pref // unit) * unit
    while t > unit and n % t:
        t -= unit
    assert n % t == 0, (n, pref, unit)
    return t


def _cast_weight(src_ref, dst_ref):
    k = src_ref.shape[0]
    rows = _tile(k, CAST_ROWS, SUBLANES)

    def body(c, carry):
        s = pl.multiple_of(c * rows, rows)
        dst_ref[pl.ds(s, rows), :] = src_ref[pl.ds(s, rows), :].astype(BF16)
        return carry

    lax.fori_loop(0, k // rows, body, 0)


def _sigmoid(x):
    return 1.0 / (1.0 + jnp.exp(-x))


def _ada_kernel(c_ref, w_ref, b_ref, o_ref):
    c = c_ref[...]
    s = (c * _sigmoid(c)).astype(BF16)
    o_ref[0] = jnp.dot(s, w_ref[0].astype(BF16), preferred_element_type=F32) + b_ref[0]


def ada_mods(cond, ada_w, ada_b):
    depth, d, n = ada_w.shape
    tn = _tile(n, MXU_TILE, LANES)
    return pl.pallas_call(
        _ada_kernel,
        grid=(depth, n // tn),
        in_specs=[
            pl.BlockSpec((COND_ROWS, d), lambda l, j: (0, 0)),
            pl.BlockSpec((1, d, tn), lambda l, j: (l, 0, j)),
            pl.BlockSpec((1, 1, tn), lambda l, j: (l, 0, j)),
        ],
        out_specs=pl.BlockSpec((1, COND_ROWS, tn), lambda l, j: (l, 0, j)),
        out_shape=jax.ShapeDtypeStruct((depth, COND_ROWS, n), F32),
        compiler_params=_params("arbitrary", "arbitrary"),
        name="ada_mods",
    )(cond, ada_w, ada_b.reshape(depth, 1, n))


class Hidden(NamedTuple):
    arrays: tuple
    t_use: int

    @property
    def split(self):
        if len(self.arrays) == 1:
            return None
        return (self.arrays[0].shape[1] // MOD_TILE, self.arrays[1].shape[1] // MOD_TILE)

    @property
    def shape(self):
        return (self.arrays[0].shape[0] * self.t_use, self.arrays[0].shape[2])


def _flat(h):
    return Hidden((h.reshape(1, *h.shape),), h.shape[0])


def _hidden_tile(src, pref):
    return _tile(src.shape[0] if src.split else src.t_use, pref, MOD_TILE)


def _hidden_specs(src, tm, cols, tile_col):
    if src.split is None:
        nt = src.t_use // tm

        def hmap(*g):
            i, j = tile_col(*g)
            return (i // nt, i % nt, j)

        return [pl.BlockSpec((None, tm, cols), hmap)]
    n_lat, n_ctx = src.split
    n_sub = tm // MOD_TILE
    specs = []
    for r in range(n_sub):
        def xmap(*g, r=r):
            i, j = tile_col(*g)
            t = i * n_sub + r
            return (t // (n_lat + n_ctx), jnp.minimum(t % (n_lat + n_ctx), n_lat - 1), j)

        def cmap(*g, r=r):
            i, j = tile_col(*g)
            t = i * n_sub + r
            return (t // (n_lat + n_ctx), jnp.maximum(t % (n_lat + n_ctx) - n_lat, 0), j)

        specs += [pl.BlockSpec((None, MOD_TILE, cols), xmap), pl.BlockSpec((None, MOD_TILE, cols), cmap)]
    return specs


def _hidden_args(src, specs):
    return list(src.arrays) * (len(specs) // len(src.arrays))


def _hidden_rows(h_refs, r, row_tile, n_sub, split):
    if split is None:
        return h_refs[0][r * MOD_TILE:(r + 1) * MOD_TILE, :]
    n_lat, n_ctx = split
    is_ctx = (row_tile * n_sub + r) % (n_lat + n_ctx) >= n_lat
    return jnp.where(is_ctx, h_refs[2 * r + 1][...], h_refs[2 * r][...])


def _rms(x, g):
    return x * lax.rsqrt(jnp.mean(x * x, axis=-1, keepdims=True) + EPS) * g


def _norm_kernel(*refs, modulate, n_sub, n_h, split):
    h_refs, refs = refs[:n_h], refs[n_h:]
    if modulate:
        g_ref, sh_ref, sc_ref, o_ref = refs
    else:
        g_ref, o_ref = refs
    for r in range(n_sub):
        y = _rms(_hidden_rows(h_refs, r, pl.program_id(0), n_sub, split), g_ref[...])
        if modulate:
            y = y * (1.0 + sc_ref[r]) + sh_ref[r]
        o_ref[r * MOD_TILE:(r + 1) * MOD_TILE, :] = y.astype(o_ref.dtype)


def norm_mod(src, g, shift=None, scale=None, out_dtype=BF16):
    m, d = src.shape
    tm = _hidden_tile(src, NORM_ROWS)
    n_sub = tm // MOD_TILE
    modulate = shift is not None
    h_specs = _hidden_specs(src, tm, d, lambda i: (i, 0))
    in_specs = h_specs + [pl.BlockSpec((1, d), lambda i: (0, 0))]
    args = _hidden_args(src, h_specs) + [g.reshape(1, d)]
    if modulate:
        in_specs += [pl.BlockSpec((n_sub, 1, d), lambda i: (i, 0, 0))] * 2
        args += [shift, scale]
    return pl.pallas_call(
        functools.partial(_norm_kernel, modulate=modulate, n_sub=n_sub, n_h=len(h_specs), split=src.split),
        grid=(m // tm,),
        in_specs=in_specs,
        out_specs=pl.BlockSpec((tm, d), lambda i: (i, 0)),
        out_shape=jax.ShapeDtypeStruct((m, d), out_dtype),
        compiler_params=_params("arbitrary"),
        name="norm_mod",
    )(*args)


def _up_kernel(*refs, act, has_bias):
    if has_bias:
        x_ref, wa_ref, wb_ref, ba_ref, bb_ref, o_ref, wa_s, wb_s = refs
    else:
        x_ref, wa_ref, wb_ref, o_ref, wa_s, wb_s = refs

    @pl.when(pl.program_id(1) == 0)
    def _():
        _cast_weight(wa_ref, wa_s)
        _cast_weight(wb_ref, wb_s)

    x = x_ref[...]
    a = jnp.dot(x, wa_s[...], preferred_element_type=F32)
    b = jnp.dot(x, wb_s[...], preferred_element_type=F32)
    if has_bias:
        a = a + ba_ref[...]
        b = b + bb_ref[...]
    if act == "swiglu":
        o = (a * _sigmoid(a)) * b
    else:
        o = a * _sigmoid(b)
    o_ref[...] = o.astype(o_ref.dtype)


def gated_up(x, wa, wa_idx, wa_col, wb, wb_idx, wb_col, n, act, bias=None, out_dtype=BF16):
    m, k = x.shape
    tm = _tile(m, MXU_TILE, MOD_TILE)
    tn = _tile(n, UP_COLS, LANES)
    nj = n // tn

    def wspec(idx, col):
        lead = (None,) * len(idx)
        return pl.BlockSpec(lead + (k, tn), lambda j, i: idx + (0, col * nj + j))

    in_specs = [pl.BlockSpec((tm, k), lambda j, i: (i, 0)), wspec(wa_idx, wa_col), wspec(wb_idx, wb_col)]
    args = [x, wa, wb]
    if bias is not None:
        ba, bb = bias
        in_specs += [pl.BlockSpec((1, tn), lambda j, i: (0, j))] * 2
        args += [ba.reshape(1, n), bb.reshape(1, n)]
    return pl.pallas_call(
        functools.partial(_up_kernel, act=act, has_bias=bias is not None),
        grid=(nj, m // tm),
        in_specs=in_specs,
        out_specs=pl.BlockSpec((tm, tn), lambda j, i: (i, j)),
        out_shape=jax.ShapeDtypeStruct((m, n), out_dtype),
        scratch_shapes=[pltpu.VMEM((k, tn), BF16), pltpu.VMEM((k, tn), BF16)],
        compiler_params=_params("arbitrary", "arbitrary"),
        name="gated_up_" + act,
    )(*args)


def _res_kernel(*refs, coef, n_sub, has_bias, n_h, split):
    x_ref, w_ref = refs[:2]
    refs = refs[2:]
    if has_bias:
        b_ref, refs = refs[0], refs[1:]
    h_refs, (gate_ref, o_ref, w_s) = refs[:n_h], refs[n_h:]

    @pl.when(pl.program_id(1) == 0)
    def _():
        _cast_weight(w_ref, w_s)

    y = jnp.dot(x_ref[...], w_s[...], preferred_element_type=F32)
    if has_bias:
        y = y + b_ref[...]
    for r in range(n_sub):
        rows = slice(r * MOD_TILE, (r + 1) * MOD_TILE)
        h = _hidden_rows(h_refs, r, pl.program_id(1), n_sub, split)
        o_ref[rows, :] = h + (coef * gate_ref[r]) * y[rows, :]


def proj_residual(x, w, w_idx, src, gate, coef, bias=None):
    m, k = x.shape
    n = src.shape[1]
    assert src.shape[0] == m
    tn = _tile(n, MXU_TILE, LANES)
    if k * tn * 14 <= VMEM_LIMIT_BYTES // 2:
        tm, w_mode = _hidden_tile(src, MXU_TILE), {}
    else:
        tm, w_mode = _hidden_tile(src, DEEP_ROWS), {"pipeline_mode": pl.Buffered(1)}
    n_sub = tm // MOD_TILE
    lead = (None,) * len(w_idx)
    in_specs = [pl.BlockSpec((tm, k), lambda j, i: (i, 0)),
                pl.BlockSpec(lead + (k, tn), lambda j, i: w_idx + (0, j), **w_mode)]
    args = [x, w]
    if bias is not None:
        in_specs.append(pl.BlockSpec((1, tn), lambda j, i: (0, j)))
        args.append(bias.reshape(1, n))
    h_specs = _hidden_specs(src, tm, tn, lambda j, i: (i, j))
    in_specs += h_specs + [pl.BlockSpec((n_sub, 1, tn), lambda j, i: (i, 0, j))]
    args += _hidden_args(src, h_specs) + [gate]
    return pl.pallas_call(
        functools.partial(_res_kernel, coef=coef, n_sub=n_sub, has_bias=bias is not None,
                          n_h=len(h_specs), split=src.split),
        grid=(n // tn, m // tm),
        in_specs=in_specs,
        out_specs=pl.BlockSpec((tm, tn), lambda j, i: (i, j)),
        out_shape=jax.ShapeDtypeStruct((m, n), F32),
        scratch_shapes=[pltpu.VMEM((k, tn), BF16)],
        compiler_params=_params("arbitrary", "arbitrary"),
        name="proj_residual",
    )(*args)


def _mla_down_kernel(*refs, lora, n_sub, n_h, split):
    h_refs, refs = refs[:n_h], refs[n_h:]
    (g_ref, sh_ref, sc_ref, wq_ref, wkv_ref, sel_ref, gq_ref, gkv_ref, cos_ref, sin_ref,
     cq_ref, ckv_ref, kpe_ref, wq_s, wkv_s, xn_s) = refs

    @pl.when(pl.program_id(0) == 0)
    def _():
        _cast_weight(wq_ref, wq_s)
        rows = _tile(wkv_ref.shape[0], CAST_ROWS, SUBLANES)

        def body(c, carry):
            r = pl.ds(pl.multiple_of(c * rows, rows), rows)
            w = wkv_ref[r, :].astype(BF16)
            wkv_s[r, :lora] = w[:, :lora]
            wkv_s[r, lora:] = jnp.dot(w[:, lora:], sel_ref[...], preferred_element_type=F32).astype(BF16)
            return carry

        lax.fori_loop(0, wkv_ref.shape[0] // rows, body, 0)

    for r in range(n_sub):
        h = _hidden_rows(h_refs, r, pl.program_id(0), n_sub, split)
        y = _rms(h, g_ref[...]) * (1.0 + sc_ref[r]) + sh_ref[r]
        xn_s[r * MOD_TILE:(r + 1) * MOD_TILE, :] = y.astype(BF16)
    x = xn_s[...]
    cq = jnp.dot(x, wq_s[...], preferred_element_type=F32)
    cq_ref[...] = _rms(cq, gq_ref[...]).astype(BF16)
    kv = jnp.dot(x, wkv_s[...], preferred_element_type=F32)
    ckv_ref[...] = _rms(kv[:, :lora], gkv_ref[...]).astype(BF16)
    kpe = kv[:, lora:lora + LANES] * cos_ref[...] + kv[:, lora + LANES:] * sin_ref[...]
    kpe_ref[...] = kpe.astype(BF16)


def mla_down(src, g, shift, scale, wdq, wdkv, gq, gkv, cos, sin):
    m, d = src.shape
    q_lora = wdq.shape[1]
    kv_lora = gkv.shape[0]
    sel = jnp.asarray(_rope_select_pair(), BF16)
    nkv = kv_lora + sel.shape[1]
    tm = _hidden_tile(src, MLA_DOWN_ROWS)
    n_sub = tm // MOD_TILE
    const = lambda i: (0, 0)
    row = lambda i: (i, 0)
    mod = pl.BlockSpec((n_sub, 1, d), lambda i: (i, 0, 0))
    h_specs = _hidden_specs(src, tm, d, lambda i: (i, 0))
    return pl.pallas_call(
        functools.partial(_mla_down_kernel, lora=kv_lora, n_sub=n_sub, n_h=len(h_specs), split=src.split),
        grid=(m // tm,),
        in_specs=h_specs + [pl.BlockSpec((1, d), const), mod, mod,
                            pl.BlockSpec((d, q_lora), const),
                            pl.BlockSpec(wdkv.shape, const),
                            pl.BlockSpec(sel.shape, const),
                            pl.BlockSpec((1, q_lora), const),
                            pl.BlockSpec((1, kv_lora), const),
                            pl.BlockSpec((tm, LANES), row),
                            pl.BlockSpec((tm, LANES), row)],
        out_specs=[pl.BlockSpec((tm, q_lora), row),
                   pl.BlockSpec((tm, kv_lora), row),
                   pl.BlockSpec((tm, LANES), row)],
        out_shape=[jax.ShapeDtypeStruct((m, q_lora), BF16),
                   jax.ShapeDtypeStruct((m, kv_lora), BF16),
                   jax.ShapeDtypeStruct((m, LANES), BF16)],
        scratch_shapes=[pltpu.VMEM((d, q_lora), BF16), pltpu.VMEM((d, nkv), BF16), pltpu.VMEM((tm, d), BF16)],
        compiler_params=_params("arbitrary"),
        name="mla_down",
    )(*_hidden_args(src, h_specs), g.reshape(1, d), shift, scale,
      wdq, wdkv, sel, gq.reshape(1, -1), gkv.reshape(1, -1), cos, sin)


def _wq_layout_kernel(w_ref, sel_ref, o_ref):
    o_ref[...] = jnp.dot(w_ref[...].astype(BF16), sel_ref[...], preferred_element_type=F32).astype(BF16)


def mla_wq_layout(wuq, nh):
    q_lora = wuq.shape[0]
    dh = QK_NOPE + QK_ROPE
    rope = _rope_select()
    one = np.zeros((dh, QK_NOPE + rope.shape[1]), np.float32)
    one[:QK_NOPE, :QK_NOPE] = np.eye(QK_NOPE)
    one[QK_NOPE:, QK_NOPE:] = rope
    pair = np.zeros((2 * one.shape[0], 2 * one.shape[1]), np.float32)
    pair[:one.shape[0], :one.shape[1]] = one
    pair[one.shape[0]:, one.shape[1]:] = one
    assert nh % 2 == 0
    return pl.pallas_call(
        _wq_layout_kernel,
        grid=(nh // 2,),
        in_specs=[pl.BlockSpec((q_lora, pair.shape[0]), lambda h: (0, h)),
                  pl.BlockSpec(pair.shape, lambda h: (0, 0))],
        out_specs=pl.BlockSpec((q_lora, pair.shape[1]), lambda h: (0, h)),
        out_shape=jax.ShapeDtypeStruct((q_lora, nh * one.shape[1]), BF16),
        compiler_params=_params("arbitrary"),
        name="mla_wq_layout",
    )(wuq, jnp.asarray(pair, BF16))


MLA_UP_HEADS = 2


def _mla_up_kernel(cq_ref, ckv_ref, kpe_ref, cos_ref, sin_ref, wq_ref, wkv_ref,
                   q_ref, k_ref, v_ref, *, q_scale):
    dq, dkv = QK_NOPE + LANES, QK_NOPE + V_DIM
    qf = jnp.dot(cq_ref[0], wq_ref[...], preferred_element_type=F32)
    kvf = jnp.dot(ckv_ref[0], wkv_ref[...].astype(BF16), preferred_element_type=F32)
    for hh in range(q_ref.shape[1]):
        q_pe = _rope(qf[:, hh * dq + QK_NOPE:(hh + 1) * dq], cos_ref[...], sin_ref[...])
        q_ref[0, hh, :, :QK_NOPE] = (qf[:, hh * dq:hh * dq + QK_NOPE] * q_scale).astype(BF16)
        q_ref[0, hh, :, QK_NOPE:] = (q_pe * q_scale).astype(BF16)
        k_ref[0, hh, :, :QK_NOPE] = kvf[:, hh * dkv:hh * dkv + QK_NOPE].astype(BF16)
        k_ref[0, hh, :, QK_NOPE:] = kpe_ref[0]
        v_ref[0, hh] = kvf[:, hh * dkv + QK_NOPE:(hh + 1) * dkv].astype(BF16)


def mla_up(cq, ckv, kpe, cos, sin, wq_h, wkv_h, q_scale):
    b, t, q_lora = cq.shape
    kv_lora = ckv.shape[2]
    dqk = QK_NOPE + LANES
    nh = wq_h.shape[1] // dqk
    hb = _tile(nh, MLA_UP_HEADS, 1)
    tok = lambda bi, hi: (bi, 0, 0)
    out = lambda bi, hi: (bi, hi, 0, 0)
    return pl.pallas_call(
        functools.partial(_mla_up_kernel, q_scale=q_scale),
        grid=(b, nh // hb),
        in_specs=[pl.BlockSpec((1, t, q_lora), tok),
                  pl.BlockSpec((1, t, kv_lora), tok),
                  pl.BlockSpec((1, t, LANES), tok),
                  pl.BlockSpec((t, LANES), lambda bi, hi: (0, 0)),
                  pl.BlockSpec((t, LANES), lambda bi, hi: (0, 0)),
                  pl.BlockSpec((q_lora, hb * dqk), lambda bi, hi: (0, hi)),
                  pl.BlockSpec((kv_lora, hb * (QK_NOPE + V_DIM)), lambda bi, hi: (0, hi))],
        out_specs=[pl.BlockSpec((1, hb, t, dqk), out),
                   pl.BlockSpec((1, hb, t, dqk), out),
                   pl.BlockSpec((1, hb, t, V_DIM), out)],
        out_shape=[jax.ShapeDtypeStruct((b, nh, t, dqk), BF16),
                   jax.ShapeDtypeStruct((b, nh, t, dqk), BF16),
                   jax.ShapeDtypeStruct((b, nh, t, V_DIM), BF16)],
        compiler_params=_params("arbitrary", "arbitrary"),
        name="mla_up",
    )(cq, ckv, kpe, cos, sin, wq_h, wkv_h)


ATTN_HEADS = 8
ATTN_TILE = 512
ATTN_ROWS = 256


def _attn_kernel(q_ref, k_ref, v_ref, o_ref, *, n_lat, n_ctx):
    def attend(n_rows, key_rows):
        for hh in range(q_ref.shape[1]):
            k = k_ref[0, hh, key_rows, :]
            v = v_ref[0, hh, key_rows, :]
            v1 = jnp.concatenate([v, jnp.ones_like(v)], axis=-1)
            for r in range(n_rows // ATTN_ROWS):
                rows = slice(r * ATTN_ROWS, (r + 1) * ATTN_ROWS)
                s = lax.dot_general(q_ref[0, hh, rows, :], k, (((1,), (1,)), ((), ())),
                                    preferred_element_type=F32)
                p = jnp.exp2(s - jnp.max(s, axis=-1, keepdims=True))
                o = jnp.dot(p.astype(BF16), v1, preferred_element_type=F32)
                o_ref[0, rows, hh * V_DIM:(hh + 1) * V_DIM] = (o[:, :V_DIM] / o[:, V_DIM:]).astype(BF16)

    @pl.when(pl.program_id(2) == 0)
    def _():
        attend(n_ctx, slice(n_lat, n_lat + n_ctx))

    @pl.when(pl.program_id(2) > 0)
    def _():
        attend(ATTN_TILE, slice(0, n_lat + n_ctx))


def attention(q, k, v, n_lat):
    b, nh, t, dqk = q.shape
    n_ctx = t - n_lat
    assert n_lat % ATTN_TILE == 0 and 0 < n_ctx <= ATTN_TILE and n_ctx % ATTN_ROWS == 0
    hb = _tile(nh, ATTN_HEADS, 1)
    n_tiles = n_lat // ATTN_TILE + 1
    tile = lambda i: (i + n_tiles - 1) % n_tiles
    return pl.pallas_call(
        functools.partial(_attn_kernel, n_lat=n_lat, n_ctx=n_ctx),
        grid=(b, nh // hb, n_tiles),
        in_specs=[pl.BlockSpec((1, hb, ATTN_TILE, dqk), lambda bi, hi, i: (bi, hi, tile(i), 0)),
                  pl.BlockSpec((1, hb, t, dqk), lambda bi, hi, i: (bi, hi, 0, 0)),
                  pl.BlockSpec((1, hb, t, V_DIM), lambda bi, hi, i: (bi, hi, 0, 0))],
        out_specs=pl.BlockSpec((1, ATTN_TILE, hb * V_DIM), lambda bi, hi, i: (bi, tile(i), hi)),
        out_shape=jax.ShapeDtypeStruct((b, t, nh * V_DIM), BF16),
        compiler_params=_params("arbitrary", "arbitrary", "arbitrary"),
        name="attention",
    )(q, k, v)


CONV_TILE = 128
CONV_GROUPS = CONV_TILE // SUBLANES


def _dwconv_kernel(prev_ref, cur_ref, next_ref, dw_ref, dwb_ref, g_ref, b_ref, o_ref, buf, cv, *, width):
    i = pl.program_id(1)
    tt = cur_ref.shape[1]
    d = cur_ref.shape[2]
    pad = width // 2
    first = i == 0
    final = i == pl.num_programs(1) - 1
    for c in range(d // LANES):
        cols = slice(c * LANES, (c + 1) * LANES)
        buf[c, 0:HALO, :] = jnp.where(first, 0.0, prev_ref[0, :, cols])
        buf[c, HALO:HALO + tt, :] = cur_ref[0, :, cols]
        buf[c, HALO + tt:, :] = jnp.where(final, 0.0, next_ref[0, :, cols])
        acc = [jnp.broadcast_to(dwb_ref[:, cols], (SUBLANES, LANES)) for _ in range(CONV_GROUPS)]
        for m in range(-pad, CONV_GROUPS + pad):
            xm = buf[c, pl.ds(HALO + m, SUBLANES, stride=CONV_GROUPS), :]
            for j in range(CONV_GROUPS):
                kk = m - j + pad
                if 0 <= kk < width:
                    acc[j] = acc[j] + xm * dw_ref[kk, :, cols]
        for j in range(CONV_GROUPS):
            cv[c, pl.ds(j, SUBLANES, stride=CONV_GROUPS), :] = acc[j]
    a = jnp.concatenate([cv[c] for c in range(d // LANES)], axis=-1)
    mu = jnp.mean(a, axis=-1, keepdims=True)
    xc = a - mu
    var = jnp.mean(xc * xc, axis=-1, keepdims=True)
    y = xc * lax.rsqrt(var + EPS) * g_ref[...] + b_ref[...]
    o_ref[0] = (y * _sigmoid(y)).astype(o_ref.dtype)


def dwconv_ln_swish(u, dw, dwb, ln_g, ln_b):
    b, t, d = u.shape
    width = dw.shape[0]
    tt = CONV_TILE
    assert t % tt == 0 and width // 2 < HALO and d % LANES == 0
    nh = tt // HALO
    last = t // HALO - 1
    vec = lambda bi, i: (0, 0)
    dw8 = jnp.broadcast_to(dw[:, None, :], (width, SUBLANES, d))
    return pl.pallas_call(
        functools.partial(_dwconv_kernel, width=width),
        grid=(b, t // tt),
        in_specs=[pl.BlockSpec((1, HALO, d), lambda bi, i: (bi, jnp.maximum(i * nh - 1, 0), 0)),
                  pl.BlockSpec((1, tt, d), lambda bi, i: (bi, i, 0)),
                  pl.BlockSpec((1, HALO, d), lambda bi, i: (bi, jnp.minimum((i + 1) * nh, last), 0)),
                  pl.BlockSpec((width, SUBLANES, d), lambda bi, i: (0, 0, 0)),
                  pl.BlockSpec((1, d), vec), pl.BlockSpec((1, d), vec), pl.BlockSpec((1, d), vec)],
        out_specs=pl.BlockSpec((1, tt, d), lambda bi, i: (bi, i, 0)),
        out_shape=jax.ShapeDtypeStruct((b, t, d), BF16),
        scratch_shapes=[pltpu.VMEM((d // LANES, tt + 2 * HALO, LANES), F32),
                        pltpu.VMEM((d // LANES, tt, LANES), F32)],
        compiler_params=_params("arbitrary", "arbitrary"),
        name="dwconv_ln_swish",
    )(u, u, u, dw8, dwb.reshape(1, d), ln_g.reshape(1, d), ln_b.reshape(1, d))


def _rope_tables(seq, ctx_len):
    n_axis = QK_ROPE // 4
    t = jnp.arange(seq, dtype=jnp.int32)
    row = (t // GRID_W).astype(F32)
    col = (t % GRID_W).astype(F32)
    freqs = ROPE_THETA ** (-jnp.arange(n_axis, dtype=F32) / n_axis)
    ang = jnp.concatenate([row[:, None] * freqs, col[:, None] * freqs], axis=-1)
    ang = jnp.concatenate([ang, jnp.zeros((ctx_len, QK_ROPE // 2), F32)], axis=0)
    cos, sin = jnp.cos(ang), jnp.sin(ang)
    zeros = jnp.zeros((seq + ctx_len, LANES - QK_ROPE), F32)
    cos_t = jnp.concatenate([cos, cos, zeros], axis=-1)
    sin_t = jnp.concatenate([-sin, sin, zeros], axis=-1)
    return cos_t, sin_t


def _rope_select():
    half = QK_ROPE // 2
    sel = np.zeros((QK_ROPE, LANES), np.float32)
    for p in range(half):
        sel[2 * p, p] = sel[2 * p, QK_ROPE + p] = 1.0
        sel[2 * p + 1, half + p] = sel[2 * p + 1, QK_ROPE + half + p] = 1.0
    return sel


def _rope_select_pair():
    sel = _rope_select()
    out = np.zeros((QK_ROPE, 2 * LANES), np.float32)
    out[:, :QK_ROPE] = sel[:, :QK_ROPE]
    out[:, LANES:LANES + QK_ROPE] = sel[:, QK_ROPE // 2:QK_ROPE // 2 + QK_ROPE]
    return out


def _rope(g, cos, sin):
    return g * cos + pltpu.roll(g, QK_ROPE // 2, axis=1) * sin


def _mod_tiles(m, layout, k, d):
    b, n_lat, n_ctx = layout
    mk = m[:, k * d:(k + 1) * d]
    parts = [jnp.broadcast_to(mk[:b, None, :], (b, n_lat, d))]
    if n_ctx:
        parts.append(jnp.broadcast_to(mk[b][None, None, :], (b, n_ctx, d)))
    return jnp.concatenate(parts, axis=1).reshape(b * (n_lat + n_ctx), 1, d)


def kernel(x, c, ctx, c_ctx, ada_w, ada_b, norm_g, ffn_w1, ffn_w3, ffn_w2, mla_wdq, mla_gq, mla_wuq, mla_wdkv, mla_gkv, mla_wukv, mla_wo, conv_w1, conv_b1, conv_dw, conv_dwb, conv_ln_g, conv_ln_b, conv_w2, conv_b2, final_g):
    b, s, d = x.shape
    lc = ctx.shape[1]
    depth = ada_w.shape[0]
    d_ff = ffn_w1.shape[-1]
    nh = mla_wo.shape[1] // V_DIM
    assert s % MOD_TILE == 0 and lc % MOD_TILE == 0 and b < COND_ROWS

    cond = jnp.concatenate([c, c_ctx[None, :], jnp.zeros((COND_ROWS - b - 1, d), F32)], axis=0)
    mods = ada_mods(cond, ada_w, ada_b)

    t_all = s + lc
    rows_all = (b, s // MOD_TILE, lc // MOD_TILE)
    rows_lat = (b, s // MOD_TILE, 0)

    src = Hidden((x, ctx), t_all)
    combined = True

    def ffn(src, m, rows, i, k, f):
        xn = norm_mod(src, norm_g[i, k], _mod_tiles(m, rows, 3 * k, d), _mod_tiles(m, rows, 3 * k + 1, d))
        g = gated_up(xn, ffn_w1, (i, f), 0, ffn_w3, (i, f), 0, d_ff, "swiglu")
        return _flat(proj_residual(g, ffn_w2, (i, f), src, _mod_tiles(m, rows, 3 * k + 2, d), FFN_RES_WEIGHT))

    def latent_only(src):
        if src.split:
            return Hidden((src.arrays[0],), s)
        return Hidden((src.arrays[0].reshape(b, t_all, d),), s)

    for i in range(depth):
        last = i == depth - 1
        mixer = i % 2
        j = i // 2
        ctx_into_mixer = (not last) or mixer == 0
        ctx_out = not last
        m = mods[i]
        if combined and not ctx_into_mixer:
            src, combined = latent_only(src), False
        rows = rows_all if combined else rows_lat

        src = ffn(src, m, rows, i, 0, 0)

        shift, scale, gate = (_mod_tiles(m, rows, k, d) for k in (3, 4, 5))
        if mixer == 0:
            assert combined
            cos_t, sin_t = _rope_tables(s, lc)
            cq, ckv, kpe = mla_down(src, norm_g[i, 1], shift, scale, mla_wdq[j], mla_wdkv[j], mla_gq[j], mla_gkv[j],
                                    jnp.tile(cos_t, (b, 1)), jnp.tile(sin_t, (b, 1)))
            wq_h = mla_wq_layout(mla_wuq[j], nh)
            wkv_h = mla_wukv[j]
            q_scale = (QK_NOPE + QK_ROPE) ** -0.5 * math.log2(math.e)
            q, k, v = mla_up(cq.reshape(b, t_all, -1), ckv.reshape(b, t_all, -1), kpe.reshape(b, t_all, -1),
                             cos_t, sin_t, wq_h, wkv_h, q_scale)
            o = attention(q, k, v, s).reshape(b * t_all, nh * V_DIM)
            src = _flat(proj_residual(o, mla_wo, (j,), src, gate, 1.0))
        else:
            assert not combined
            xn = norm_mod(src, norm_g[i, 1], shift, scale)
            u = gated_up(xn, conv_w1, (j,), 0, conv_w1, (j,), 1, d, "glu",
                         bias=(conv_b1[j, :d], conv_b1[j, d:]), out_dtype=F32)
            vv = dwconv_ln_swish(u.reshape(b, s, d), conv_dw[j], conv_dwb[j], conv_ln_g[j], conv_ln_b[j])
            src = _flat(proj_residual(vv.reshape(-1, d), conv_w2, (j,), src, gate, 1.0, bias=conv_b2[j]))

        if combined and not ctx_out:
            src, combined = latent_only(src), False
            rows = rows_lat
        src = ffn(src, m, rows, i, 2, 1)

    if combined:
        src = latent_only(src)
    return norm_mod(src, final_g, out_dtype=F32).reshape(b, s, d)
```

```python
import functools
import math
from typing import NamedTuple

import numpy as np
import jax
import jax.numpy as jnp
from jax import lax
from jax.experimental import pallas as pl
from jax.experimental.pallas import tpu as pltpu

F32 = jnp.float32
BF16 = jnp.bfloat16

GRID_W = 64
QK_NOPE = 128
QK_ROPE = 64
V_DIM = 128
ROPE_THETA = 10000.0
FFN_RES_WEIGHT = 0.5
N_MOD = 9
EPS = 1e-6

LANES = 128
SUBLANES = 8
VMEM_LIMIT_BYTES = 58 * 1024 * 1024

MOD_TILE = 256
COND_ROWS = 8
HALO = 16
CAST_ROWS = 256

MXU_TILE = 1024
UP_COLS = 512
DEEP_ROWS = 512
NORM_ROWS = 1024
MLA_DOWN_ROWS = 512


def _params(*sem):
    return pltpu.CompilerParams(dimension_semantics=sem, vmem_limit_bytes=VMEM_LIMIT_BYTES)


def _tile(n, pref, unit):
    if n <= pref:
        return n
    t = (pref // unit) * unit
    while t > unit and n % t:
        t -= unit
    assert n % t == 0, (n, pref, unit)
    return t


def _cast_weight(src_ref, dst_ref):
    k = src_ref.shape[0]
    rows = _tile(k, CAST_ROWS, SUBLANES)

    def body(c, carry):
        s = pl.multiple_of(c * rows, rows)
        dst_ref[pl.ds(s, rows), :] = src_ref[pl.ds(s, rows), :].astype(BF16)
        return carry

    lax.fori_loop(0, k // rows, body, 0)


def _sigmoid(x):
    return 1.0 / (1.0 + jnp.exp(-x))


def _ada_kernel(c_ref, w_ref, b_ref, o_ref):
    c = c_ref[...]
    s = (c * _sigmoid(c)).astype(BF16)
    o_ref[0] = jnp.dot(s, w_ref[0].astype(BF16), preferred_element_type=F32) + b_ref[0]


def ada_mods(cond, ada_w, ada_b):
    depth, d, n = ada_w.shape
    tn = _tile(n, MXU_TILE, LANES)
    return pl.pallas_call(
        _ada_kernel,
        grid=(depth, n // tn),
        in_specs=[
            pl.BlockSpec((COND_ROWS, d), lambda l, j: (0, 0)),
            pl.BlockSpec((1, d, tn), lambda l, j: (l, 0, j)),
            pl.BlockSpec((1, 1, tn), lambda l, j: (l, 0, j)),
        ],
        out_specs=pl.BlockSpec((1, COND_ROWS, tn), lambda l, j: (l, 0, j)),
        out_shape=jax.ShapeDtypeStruct((depth, COND_ROWS, n), F32),
        compiler_params=_params("arbitrary", "arbitrary"),
        name="ada_mods",
    )(cond, ada_w, ada_b.reshape(depth, 1, n))


class Hidden(NamedTuple):
    arrays: tuple
    t_use: int

    @property
    def split(self):
        if len(self.arrays) == 1:
            return None
        return (self.arrays[0].shape[1] // MOD_TILE, self.arrays[1].shape[1] // MOD_TILE)

    @property
    def shape(self):
        return (self.arrays[0].shape[0] * self.t_use, self.arrays[0].shape[2])


def _flat(h):
    return Hidden((h.reshape(1, *h.shape),), h.shape[0])


def _hidden_tile(src, pref):
    return _tile(src.shape[0] if src.split else src.t_use, pref, MOD_TILE)


def _hidden_specs(src, tm, cols, tile_col):
    if src.split is None:
        nt = src.t_use // tm

        def hmap(*g):
            i, j = tile_col(*g)
            return (i // nt, i % nt, j)

        return [pl.BlockSpec((None, tm, cols), hmap)]
    n_lat, n_ctx = src.split
    n_sub = tm // MOD_TILE
    specs = []
    for r in range(n_sub):
        def xmap(*g, r=r):
            i, j = tile_col(*g)
            t = i * n_sub + r
            return (t // (n_lat + n_ctx), jnp.minimum(t % (n_lat + n_ctx), n_lat - 1), j)

        def cmap(*g, r=r):
            i, j = tile_col(*g)
            t = i * n_sub + r
            return (t // (n_lat + n_ctx), jnp.maximum(t % (n_lat + n_ctx) - n_lat, 0), j)

        specs += [pl.BlockSpec((None, MOD_TILE, cols), xmap), pl.BlockSpec((None, MOD_TILE, cols), cmap)]
    return specs


def _hidden_args(src, specs):
    return list(src.arrays) * (len(specs) // len(src.arrays))


def _hidden_rows(h_refs, r, row_tile, n_sub, split):
    if split is None:
        return h_refs[0][r * MOD_TILE:(r + 1) * MOD_TILE, :]
    n_lat, n_ctx = split
    is_ctx = (row_tile * n_sub + r) % (n_lat + n_ctx) >= n_lat
    return jnp.where(is_ctx, h_refs[2 * r + 1][...], h_refs[2 * r][...])


def _rms(x, g):
    return x * lax.rsqrt(jnp.mean(x * x, axis=-1, keepdims=True) + EPS) * g


def _norm_kernel(*refs, modulate, n_sub, n_h, split):
    h_refs, refs = refs[:n_h], refs[n_h:]
    if modulate:
        g_ref, sh_ref, sc_ref, o_ref = refs
    else:
        g_ref, o_ref = refs
    for r in range(n_sub):
        y = _rms(_hidden_rows(h_refs, r, pl.program_id(0), n_sub, split), g_ref[...])
        if modulate:
            y = y * (1.0 + sc_ref[r]) + sh_ref[r]
        o_ref[r * MOD_TILE:(r + 1) * MOD_TILE, :] = y.astype(o_ref.dtype)


def norm_mod(src, g, shift=None, scale=None, out_dtype=BF16):
    m, d = src.shape
    tm = _hidden_tile(src, NORM_ROWS)
    n_sub = tm // MOD_TILE
    modulate = shift is not None
    h_specs = _hidden_specs(src, tm, d, lambda i: (i, 0))
    in_specs = h_specs + [pl.BlockSpec((1, d), lambda i: (0, 0))]
    args = _hidden_args(src, h_specs) + [g.reshape(1, d)]
    if modulate:
        in_specs += [pl.BlockSpec((n_sub, 1, d), lambda i: (i, 0, 0))] * 2
        args += [shift, scale]
    return pl.pallas_call(
        functools.partial(_norm_kernel, modulate=modulate, n_sub=n_sub, n_h=len(h_specs), split=src.split),
        grid=(m // tm,),
        in_specs=in_specs,
        out_specs=pl.BlockSpec((tm, d), lambda i: (i, 0)),
        out_shape=jax.ShapeDtypeStruct((m, d), out_dtype),
        compiler_params=_params("arbitrary"),
        name="norm_mod",
    )(*args)


def _up_kernel(*refs, act, has_bias):
    if has_bias:
        x_ref, wa_ref, wb_ref, ba_ref, bb_ref, o_ref, wa_s, wb_s = refs
    else:
        x_ref, wa_ref, wb_ref, o_ref, wa_s, wb_s = refs

    @pl.when(pl.program_id(1) == 0)
    def _():
        _cast_weight(wa_ref, wa_s)
        _cast_weight(wb_ref, wb_s)

    x = x_ref[...]
    a = jnp.dot(x, wa_s[...], preferred_element_type=F32)
    b = jnp.dot(x, wb_s[...], preferred_element_type=F32)
    if has_bias:
        a = a + ba_ref[...]
        b = b + bb_ref[...]
    if act == "swiglu":
        o = (a * _sigmoid(a)) * b
    else:
        o = a * _sigmoid(b)
    o_ref[...] = o.astype(o_ref.dtype)


def gated_up(x, wa, wa_idx, wa_col, wb, wb_idx, wb_col, n, act, bias=None, out_dtype=BF16):
    m, k = x.shape
    tm = _tile(m, MXU_TILE, MOD_TILE)
    tn = _tile(n, UP_COLS, LANES)
    nj = n // tn

    def wspec(idx, col):
        lead = (None,) * len(idx)
        return pl.BlockSpec(lead + (k, tn), lambda j, i: idx + (0, col * nj + j))

    in_specs = [pl.BlockSpec((tm, k), lambda j, i: (i, 0)), wspec(wa_idx, wa_col), wspec(wb_idx, wb_col)]
    args = [x, wa, wb]
    if bias is not None:
        ba, bb = bias
        in_specs += [pl.BlockSpec((1, tn), lambda j, i: (0, j))] * 2
        args += [ba.reshape(1, n), bb.reshape(1, n)]
    return pl.pallas_call(
        functools.partial(_up_kernel, act=act, has_bias=bias is not None),
        grid=(nj, m // tm),
        in_specs=in_specs,
        out_specs=pl.BlockSpec((tm, tn), lambda j, i: (i, j)),
        out_shape=jax.ShapeDtypeStruct((m, n), out_dtype),
        scratch_shapes=[pltpu.VMEM((k, tn), BF16), pltpu.VMEM((k, tn), BF16)],
        compiler_params=_params("arbitrary", "arbitrary"),
        name="gated_up_" + act,
    )(*args)


def _res_kernel(*refs, coef, n_sub, has_bias, n_h, split):
    x_ref, w_ref = refs[:2]
    refs = refs[2:]
    if has_bias:
        b_ref, refs = refs[0], refs[1:]
    h_refs, (gate_ref, o_ref, w_s) = refs[:n_h], refs[n_h:]

    @pl.when(pl.program_id(1) == 0)
    def _():
        _cast_weight(w_ref, w_s)

    y = jnp.dot(x_ref[...], w_s[...], preferred_element_type=F32)
    if has_bias:
        y = y + b_ref[...]
    for r in range(n_sub):
        rows = slice(r * MOD_TILE, (r + 1) * MOD_TILE)
        h = _hidden_rows(h_refs, r, pl.program_id(1), n_sub, split)
        o_ref[rows, :] = h + (coef * gate_ref[r]) * y[rows, :]


def proj_residual(x, w, w_idx, src, gate, coef, bias=None):
    m, k = x.shape
    n = src.shape[1]
    assert src.shape[0] == m
    tn = _tile(n, MXU_TILE, LANES)
    if k * tn * 14 <= VMEM_LIMIT_BYTES // 2:
        tm, w_mode = _hidden_tile(src, MXU_TILE), {}
    else:
        tm, w_mode = _hidden_tile(src, DEEP_ROWS), {"pipeline_mode": pl.Buffered(1)}
    n_sub = tm // MOD_TILE
    lead = (None,) * len(w_idx)
    in_specs = [pl.BlockSpec((tm, k), lambda j, i: (i, 0)),
                pl.BlockSpec(lead + (k, tn), lambda j, i: w_idx + (0, j), **w_mode)]
    args = [x, w]
    if bias is not None:
        in_specs.append(pl.BlockSpec((1, tn), lambda j, i: (0, j)))
        args.append(bias.reshape(1, n))
    h_specs = _hidden_specs(src, tm, tn, lambda j, i: (i, j))
    in_specs += h_specs + [pl.BlockSpec((n_sub, 1, tn), lambda j, i: (i, 0, j))]
    args += _hidden_args(src, h_specs) + [gate]
    return pl.pallas_call(
        functools.partial(_res_kernel, coef=coef, n_sub=n_sub, has_bias=bias is not None,
                          n_h=len(h_specs), split=src.split),
        grid=(n // tn, m // tm),
        in_specs=in_specs,
        out_specs=pl.BlockSpec((tm, tn), lambda j, i: (i, j)),
        out_shape=jax.ShapeDtypeStruct((m, n), F32),
        scratch_shapes=[pltpu.VMEM((k, tn), BF16)],
        compiler_params=_params("arbitrary", "arbitrary"),
        name="proj_residual",
    )(*args)


def _res_norm_kernel(*refs, coef, n_sub, has_bias):
    x_ref, w_ref = refs[:2]
    refs = refs[2:]
    if has_bias:
        b_ref, refs = refs[0], refs[1:]
    h_ref, gate_ref, g_ref, sh_ref, sc_ref, o_ref, xn_ref, w_s = refs

    @pl.when(pl.program_id(0) == 0)
    def _():
        _cast_weight(w_ref, w_s)

    y = jnp.dot(x_ref[...], w_s[...], preferred_element_type=F32)
    if has_bias:
        y = y + b_ref[...]
    for r in range(n_sub):
        rows = slice(r * MOD_TILE, (r + 1) * MOD_TILE)
        h = h_ref[rows, :] + (coef * gate_ref[r]) * y[rows, :]
        o_ref[rows, :] = h
        xn_ref[rows, :] = (_rms(h, g_ref[...]) * (1.0 + sc_ref[r]) + sh_ref[r]).astype(BF16)


def proj_residual_norm(x, w, w_idx, h, gate, coef, g, shift, scale, bias=None):
    m, k = x.shape
    n = h.shape[1]
    tm = _tile(m, DEEP_ROWS, MOD_TILE)
    n_sub = tm // MOD_TILE
    lead = (None,) * len(w_idx)
    row = lambda i: (i, 0)
    vec = pl.BlockSpec((1, n), lambda i: (0, 0))
    mod = pl.BlockSpec((n_sub, 1, n), lambda i: (i, 0, 0))
    in_specs = [pl.BlockSpec((tm, k), row),
                pl.BlockSpec(lead + (k, n), lambda i: w_idx + (0, 0), pipeline_mode=pl.Buffered(1))]
    args = [x, w]
    if bias is not None:
        in_specs.append(vec)
        args.append(bias.reshape(1, n))
    in_specs += [pl.BlockSpec((tm, n), row), mod, vec, mod, mod]
    args += [h, gate, g.reshape(1, n), shift, scale]
    return pl.pallas_call(
        functools.partial(_res_norm_kernel, coef=coef, n_sub=n_sub, has_bias=bias is not None),
        grid=(m // tm,),
        in_specs=in_specs,
        out_specs=[pl.BlockSpec((tm, n), row), pl.BlockSpec((tm, n), row)],
        out_shape=[jax.ShapeDtypeStruct((m, n), F32), jax.ShapeDtypeStruct((m, n), BF16)],
        scratch_shapes=[pltpu.VMEM((k, n), BF16)],
        compiler_params=_params("arbitrary"),
        name="proj_residual_norm",
    )(*args)


def _mla_down_kernel(*refs, lora, n_sub, n_h, split):
    h_refs, refs = refs[:n_h], refs[n_h:]
    (g_ref, sh_ref, sc_ref, wq_ref, wkv_ref, sel_ref, gq_ref, gkv_ref, cos_ref, sin_ref,
     cq_ref, ckv_ref, kpe_ref, wq_s, wkv_s, xn_s) = refs

    @pl.when(pl.program_id(0) == 0)
    def _():
        _cast_weight(wq_ref, wq_s)
        rows = _tile(wkv_ref.shape[0], CAST_ROWS, SUBLANES)

        def body(c, carry):
            r = pl.ds(pl.multiple_of(c * rows, rows), rows)
            w = wkv_ref[r, :].astype(BF16)
            wkv_s[r, :lora] = w[:, :lora]
            wkv_s[r, lora:] = jnp.dot(w[:, lora:], sel_ref[...], preferred_element_type=F32).astype(BF16)
            return carry

        lax.fori_loop(0, wkv_ref.shape[0] // rows, body, 0)

    for r in range(n_sub):
        h = _hidden_rows(h_refs, r, pl.program_id(0), n_sub, split)
        y = _rms(h, g_ref[...]) * (1.0 + sc_ref[r]) + sh_ref[r]
        xn_s[r * MOD_TILE:(r + 1) * MOD_TILE, :] = y.astype(BF16)
    x = xn_s[...]
    cq = jnp.dot(x, wq_s[...], preferred_element_type=F32)
    cq_ref[...] = _rms(cq, gq_ref[...]).astype(BF16)
    kv = jnp.dot(x, wkv_s[...], preferred_element_type=F32)
    ckv_ref[...] = _rms(kv[:, :lora], gkv_ref[...]).astype(BF16)
    kpe = kv[:, lora:lora + LANES] * cos_ref[...] + kv[:, lora + LANES:] * sin_ref[...]
    kpe_ref[...] = kpe.astype(BF16)


def mla_down(src, g, shift, scale, wdq, wdkv, gq, gkv, cos, sin):
    m, d = src.shape
    q_lora = wdq.shape[1]
    kv_lora = gkv.shape[0]
    sel = jnp.asarray(_rope_select_pair(), BF16)
    nkv = kv_lora + sel.shape[1]
    tm = _hidden_tile(src, MLA_DOWN_ROWS)
    n_sub = tm // MOD_TILE
    const = lambda i: (0, 0)
    row = lambda i: (i, 0)
    mod = pl.BlockSpec((n_sub, 1, d), lambda i: (i, 0, 0))
    h_specs = _hidden_specs(src, tm, d, lambda i: (i, 0))
    return pl.pallas_call(
        functools.partial(_mla_down_kernel, lora=kv_lora, n_sub=n_sub, n_h=len(h_specs), split=src.split),
        grid=(m // tm,),
        in_specs=h_specs + [pl.BlockSpec((1, d), const), mod, mod,
                            pl.BlockSpec((d, q_lora), const),
                            pl.BlockSpec(wdkv.shape, const),
                            pl.BlockSpec(sel.shape, const),
                            pl.BlockSpec((1, q_lora), const),
                            pl.BlockSpec((1, kv_lora), const),
                            pl.BlockSpec((tm, LANES), row),
                            pl.BlockSpec((tm, LANES), row)],
        out_specs=[pl.BlockSpec((tm, q_lora), row),
                   pl.BlockSpec((tm, kv_lora), row),
                   pl.BlockSpec((tm, LANES), row)],
        out_shape=[jax.ShapeDtypeStruct((m, q_lora), BF16),
                   jax.ShapeDtypeStruct((m, kv_lora), BF16),
                   jax.ShapeDtypeStruct((m, LANES), BF16)],
        scratch_shapes=[pltpu.VMEM((d, q_lora), BF16), pltpu.VMEM((d, nkv), BF16), pltpu.VMEM((tm, d), BF16)],
        compiler_params=_params("arbitrary"),
        name="mla_down",
    )(*_hidden_args(src, h_specs), g.reshape(1, d), shift, scale,
      wdq, wdkv, sel, gq.reshape(1, -1), gkv.reshape(1, -1), cos, sin)


def _wq_layout_kernel(w_ref, sel_ref, o_ref):
    o_ref[...] = jnp.dot(w_ref[...].astype(BF16), sel_ref[...], preferred_element_type=F32).astype(BF16)


def mla_wq_layout(wuq, nh):
    q_lora = wuq.shape[0]
    dh = QK_NOPE + QK_ROPE
    rope = _rope_select()
    one = np.zeros((dh, QK_NOPE + rope.shape[1]), np.float32)
    one[:QK_NOPE, :QK_NOPE] = np.eye(QK_NOPE)
    one[QK_NOPE:, QK_NOPE:] = rope
    pair = np.zeros((2 * one.shape[0], 2 * one.shape[1]), np.float32)
    pair[:one.shape[0], :one.shape[1]] = one
    pair[one.shape[0]:, one.shape[1]:] = one
    assert nh % 2 == 0
    return pl.pallas_call(
        _wq_layout_kernel,
        grid=(nh // 2,),
        in_specs=[pl.BlockSpec((q_lora, pair.shape[0]), lambda h: (0, h)),
                  pl.BlockSpec(pair.shape, lambda h: (0, 0))],
        out_specs=pl.BlockSpec((q_lora, pair.shape[1]), lambda h: (0, h)),
        out_shape=jax.ShapeDtypeStruct((q_lora, nh * one.shape[1]), BF16),
        compiler_params=_params("arbitrary"),
        name="mla_wq_layout",
    )(wuq, jnp.asarray(pair, BF16))


MLA_UP_HEADS = 2


def _mla_up_kernel(cq_ref, ckv_ref, kpe_ref, cos_ref, sin_ref, wq_ref, wkv_ref,
                   q_ref, k_ref, v_ref, *, q_scale):
    dq, dkv = QK_NOPE + LANES, QK_NOPE + V_DIM
    qf = jnp.dot(cq_ref[0], wq_ref[...], preferred_element_type=F32)
    kvf = jnp.dot(ckv_ref[0], wkv_ref[...].astype(BF16), preferred_element_type=F32)
    for hh in range(q_ref.shape[1]):
        q_pe = _rope(qf[:, hh * dq + QK_NOPE:(hh + 1) * dq], cos_ref[...], sin_ref[...])
        q_ref[0, hh, :, :QK_NOPE] = (qf[:, hh * dq:hh * dq + QK_NOPE] * q_scale).astype(BF16)
        q_ref[0, hh, :, QK_NOPE:] = (q_pe * q_scale).astype(BF16)
        k_ref[0, hh, :, :QK_NOPE] = kvf[:, hh * dkv:hh * dkv + QK_NOPE].astype(BF16)
        k_ref[0, hh, :, QK_NOPE:] = kpe_ref[0]
        v_ref[0, hh] = kvf[:, hh * dkv + QK_NOPE:(hh + 1) * dkv].astype(BF16)


def mla_up(cq, ckv, kpe, cos, sin, wq_h, wkv_h, q_scale):
    b, t, q_lora = cq.shape
    kv_lora = ckv.shape[2]
    dqk = QK_NOPE + LANES
    nh = wq_h.shape[1] // dqk
    hb = _tile(nh, MLA_UP_HEADS, 1)
    tok = lambda bi, hi: (bi, 0, 0)
    out = lambda bi, hi: (bi, hi, 0, 0)
    return pl.pallas_call(
        functools.partial(_mla_up_kernel, q_scale=q_scale),
        grid=(b, nh // hb),
        in_specs=[pl.BlockSpec((1, t, q_lora), tok),
                  pl.BlockSpec((1, t, kv_lora), tok),
                  pl.BlockSpec((1, t, LANES), tok),
                  pl.BlockSpec((t, LANES), lambda bi, hi: (0, 0)),
                  pl.BlockSpec((t, LANES), lambda bi, hi: (0, 0)),
                  pl.BlockSpec((q_lora, hb * dqk), lambda bi, hi: (0, hi)),
                  pl.BlockSpec((kv_lora, hb * (QK_NOPE + V_DIM)), lambda bi, hi: (0, hi))],
        out_specs=[pl.BlockSpec((1, hb, t, dqk), out),
                   pl.BlockSpec((1, hb, t, dqk), out),
                   pl.BlockSpec((1, hb, t, V_DIM), out)],
        out_shape=[jax.ShapeDtypeStruct((b, nh, t, dqk), BF16),
                   jax.ShapeDtypeStruct((b, nh, t, dqk), BF16),
                   jax.ShapeDtypeStruct((b, nh, t, V_DIM), BF16)],
        compiler_params=_params("arbitrary", "arbitrary"),
        name="mla_up",
    )(cq, ckv, kpe, cos, sin, wq_h, wkv_h)


ATTN_HEADS = 8
ATTN_TILE = 512
ATTN_ROWS = 256


def _attn_kernel(q_ref, k_ref, v_ref, o_ref, *, n_lat, n_ctx):
    def attend(n_rows, key_rows):
        for hh in range(q_ref.shape[1]):
            k = k_ref[0, hh, key_rows, :]
            v = v_ref[0, hh, key_rows, :]
            v1 = jnp.concatenate([v, jnp.ones_like(v)], axis=-1)
            for r in range(n_rows // ATTN_ROWS):
                rows = slice(r * ATTN_ROWS, (r + 1) * ATTN_ROWS)
                s = lax.dot_general(q_ref[0, hh, rows, :], k, (((1,), (1,)), ((), ())),
                                    preferred_element_type=F32)
                p = jnp.exp2(s - jnp.max(s, axis=-1, keepdims=True))
                o = jnp.dot(p.astype(BF16), v1, preferred_element_type=F32)
                o_ref[0, rows, hh * V_DIM:(hh + 1) * V_DIM] = (o[:, :V_DIM] / o[:, V_DIM:]).astype(BF16)

    @pl.when(pl.program_id(2) == 0)
    def _():
        attend(n_ctx, slice(n_lat, n_lat + n_ctx))

    @pl.when(pl.program_id(2) > 0)
    def _():
        attend(ATTN_TILE, slice(0, n_lat + n_ctx))


def attention(q, k, v, n_lat):
    b, nh, t, dqk = q.shape
    n_ctx = t - n_lat
    assert n_lat % ATTN_TILE == 0 and 0 < n_ctx <= ATTN_TILE and n_ctx % ATTN_ROWS == 0
    hb = _tile(nh, ATTN_HEADS, 1)
    n_tiles = n_lat // ATTN_TILE + 1
    tile = lambda i: (i + n_tiles - 1) % n_tiles
    return pl.pallas_call(
        functools.partial(_attn_kernel, n_lat=n_lat, n_ctx=n_ctx),
        grid=(b, nh // hb, n_tiles),
        in_specs=[pl.BlockSpec((1, hb, ATTN_TILE, dqk), lambda bi, hi, i: (bi, hi, tile(i), 0)),
                  pl.BlockSpec((1, hb, t, dqk), lambda bi, hi, i: (bi, hi, 0, 0)),
                  pl.BlockSpec((1, hb, t, V_DIM), lambda bi, hi, i: (bi, hi, 0, 0))],
        out_specs=pl.BlockSpec((1, ATTN_TILE, hb * V_DIM), lambda bi, hi, i: (bi, tile(i), hi)),
        out_shape=jax.ShapeDtypeStruct((b, t, nh * V_DIM), BF16),
        compiler_params=_params("arbitrary", "arbitrary", "arbitrary"),
        name="attention",
    )(q, k, v)


CONV_TILE = 128
CONV_GROUPS = CONV_TILE // SUBLANES


def _dwconv_kernel(prev_ref, cur_ref, next_ref, dw_ref, dwb_ref, g_ref, b_ref, o_ref, buf, cv, *, width):
    i = pl.program_id(1)
    tt = cur_ref.shape[1]
    d = cur_ref.shape[2]
    pad = width // 2
    first = i == 0
    final = i == pl.num_programs(1) - 1
    for c in range(d // LANES):
        cols = slice(c * LANES, (c + 1) * LANES)
        buf[c, 0:HALO, :] = jnp.where(first, 0.0, prev_ref[0, :, cols])
        buf[c, HALO:HALO + tt, :] = cur_ref[0, :, cols]
        buf[c, HALO + tt:, :] = jnp.where(final, 0.0, next_ref[0, :, cols])
        acc = [jnp.broadcast_to(dwb_ref[:, cols], (SUBLANES, LANES)) for _ in range(CONV_GROUPS)]
        for m in range(-pad, CONV_GROUPS + pad):
            xm = buf[c, pl.ds(HALO + m, SUBLANES, stride=CONV_GROUPS), :]
            for j in range(CONV_GROUPS):
                kk = m - j + pad
                if 0 <= kk < width:
                    acc[j] = acc[j] + xm * dw_ref[kk, :, cols]
        for j in range(CONV_GROUPS):
            cv[c, pl.ds(j, SUBLANES, stride=CONV_GROUPS), :] = acc[j]
    a = jnp.concatenate([cv[c] for c in range(d // LANES)], axis=-1)
    mu = jnp.mean(a, axis=-1, keepdims=True)
    xc = a - mu
    var = jnp.mean(xc * xc, axis=-1, keepdims=True)
    y = xc * lax.rsqrt(var + EPS) * g_ref[...] + b_ref[...]
    o_ref[0] = (y * _sigmoid(y)).astype(o_ref.dtype)


def dwconv_ln_swish(u, dw, dwb, ln_g, ln_b):
    b, t, d = u.shape
    width = dw.shape[0]
    tt = CONV_TILE
    assert t % tt == 0 and width // 2 < HALO and d % LANES == 0
    nh = tt // HALO
    last = t // HALO - 1
    vec = lambda bi, i: (0, 0)
    dw8 = jnp.broadcast_to(dw[:, None, :], (width, SUBLANES, d))
    return pl.pallas_call(
        functools.partial(_dwconv_kernel, width=width),
        grid=(b, t // tt),
        in_specs=[pl.BlockSpec((1, HALO, d), lambda bi, i: (bi, jnp.maximum(i * nh - 1, 0), 0)),
                  pl.BlockSpec((1, tt, d), lambda bi, i: (bi, i, 0)),
                  pl.BlockSpec((1, HALO, d), lambda bi, i: (bi, jnp.minimum((i + 1) * nh, last), 0)),
                  pl.BlockSpec((width, SUBLANES, d), lambda bi, i: (0, 0, 0)),
                  pl.BlockSpec((1, d), vec), pl.BlockSpec((1, d), vec), pl.BlockSpec((1, d), vec)],
        out_specs=pl.BlockSpec((1, tt, d), lambda bi, i: (bi, i, 0)),
        out_shape=jax.ShapeDtypeStruct((b, t, d), BF16),
        scratch_shapes=[pltpu.VMEM((d // LANES, tt + 2 * HALO, LANES), F32),
                        pltpu.VMEM((d // LANES, tt, LANES), F32)],
        compiler_params=_params("arbitrary", "arbitrary"),
        name="dwconv_ln_swish",
    )(u, u, u, dw8, dwb.reshape(1, d), ln_g.reshape(1, d), ln_b.reshape(1, d))


def _rope_tables(seq, ctx_len):
    n_axis = QK_ROPE // 4
    t = jnp.arange(seq, dtype=jnp.int32)
    row = (t // GRID_W).astype(F32)
    col = (t % GRID_W).astype(F32)
    freqs = ROPE_THETA ** (-jnp.arange(n_axis, dtype=F32) / n_axis)
    ang = jnp.concatenate([row[:, None] * freqs, col[:, None] * freqs], axis=-1)
    ang = jnp.concatenate([ang, jnp.zeros((ctx_len, QK_ROPE // 2), F32)], axis=0)
    cos, sin = jnp.cos(ang), jnp.sin(ang)
    zeros = jnp.zeros((seq + ctx_len, LANES - QK_ROPE), F32)
    cos_t = jnp.concatenate([cos, cos, zeros], axis=-1)
    sin_t = jnp.concatenate([-sin, sin, zeros], axis=-1)
    return cos_t, sin_t


def _rope_select():
    half = QK_ROPE // 2
    sel = np.zeros((QK_ROPE, LANES), np.float32)
    for p in range(half):
        sel[2 * p, p] = sel[2 * p, QK_ROPE + p] = 1.0
        sel[2 * p + 1, half + p] = sel[2 * p + 1, QK_ROPE + half + p] = 1.0
    return sel


def _rope_select_pair():
    sel = _rope_select()
    out = np.zeros((QK_ROPE, 2 * LANES), np.float32)
    out[:, :QK_ROPE] = sel[:, :QK_ROPE]
    out[:, LANES:LANES + QK_ROPE] = sel[:, QK_ROPE // 2:QK_ROPE // 2 + QK_ROPE]
    return out


def _rope(g, cos, sin):
    return g * cos + pltpu.roll(g, QK_ROPE // 2, axis=1) * sin


def _mod_tiles(m, layout, k, d):
    b, n_lat, n_ctx = layout
    mk = m[:, k * d:(k + 1) * d]
    parts = [jnp.broadcast_to(mk[:b, None, :], (b, n_lat, d))]
    if n_ctx:
        parts.append(jnp.broadcast_to(mk[b][None, None, :], (b, n_ctx, d)))
    return jnp.concatenate(parts, axis=1).reshape(b * (n_lat + n_ctx), 1, d)


def kernel(x, c, ctx, c_ctx, ada_w, ada_b, norm_g, ffn_w1, ffn_w3, ffn_w2, mla_wdq, mla_gq, mla_wuq, mla_wdkv, mla_gkv, mla_wukv, mla_wo, conv_w1, conv_b1, conv_dw, conv_dwb, conv_ln_g, conv_ln_b, conv_w2, conv_b2, final_g):
    b, s, d = x.shape
    lc = ctx.shape[1]
    depth = ada_w.shape[0]
    d_ff = ffn_w1.shape[-1]
    nh = mla_wo.shape[1] // V_DIM
    assert s % MOD_TILE == 0 and lc % MOD_TILE == 0 and b < COND_ROWS

    cond = jnp.concatenate([c, c_ctx[None, :], jnp.zeros((COND_ROWS - b - 1, d), F32)], axis=0)
    mods = ada_mods(cond, ada_w, ada_b)

    t_all = s + lc
    rows_all = (b, s // MOD_TILE, lc // MOD_TILE)
    rows_lat = (b, s // MOD_TILE, 0)

    src = Hidden((x, ctx), t_all)
    combined = True

    def ffn(src, m, rows, i, k, f, xn=None):
        if xn is None:
            xn = norm_mod(src, norm_g[i, k], _mod_tiles(m, rows, 3 * k, d), _mod_tiles(m, rows, 3 * k + 1, d))
        g = gated_up(xn, ffn_w1, (i, f), 0, ffn_w3, (i, f), 0, d_ff, "swiglu")
        return _flat(proj_residual(g, ffn_w2, (i, f), src, _mod_tiles(m, rows, 3 * k + 2, d), FFN_RES_WEIGHT))

    def latent_only(src):
        if src.split:
            return Hidden((src.arrays[0],), s)
        return Hidden((src.arrays[0].reshape(b, t_all, d),), s)

    for i in range(depth):
        last = i == depth - 1
        mixer = i % 2
        j = i // 2
        ctx_into_mixer = (not last) or mixer == 0
        ctx_out = not last
        m = mods[i]
        if combined and not ctx_into_mixer:
            src, combined = latent_only(src), False
        rows = rows_all if combined else rows_lat

        src = ffn(src, m, rows, i, 0, 0)

        shift, scale, gate = (_mod_tiles(m, rows, k, d) for k in (3, 4, 5))
        fuse_norm = not (combined and not ctx_out)

        def mixer_out(xo, w, bias=None):
            if not fuse_norm:
                return _flat(proj_residual(xo, w, (j,), src, gate, 1.0, bias=bias)), None
            h_new, xn2 = proj_residual_norm(xo, w, (j,), src.arrays[0].reshape(-1, d), gate, 1.0, norm_g[i, 2],
                                            _mod_tiles(m, rows, 6, d), _mod_tiles(m, rows, 7, d), bias=bias)
            return _flat(h_new), xn2

        if mixer == 0:
            assert combined
            cos_t, sin_t = _rope_tables(s, lc)
            cq, ckv, kpe = mla_down(src, norm_g[i, 1], shift, scale, mla_wdq[j], mla_wdkv[j], mla_gq[j], mla_gkv[j],
                                    jnp.tile(cos_t, (b, 1)), jnp.tile(sin_t, (b, 1)))
            wq_h = mla_wq_layout(mla_wuq[j], nh)
            wkv_h = mla_wukv[j]
            q_scale = (QK_NOPE + QK_ROPE) ** -0.5 * math.log2(math.e)
            q, k, v = mla_up(cq.reshape(b, t_all, -1), ckv.reshape(b, t_all, -1), kpe.reshape(b, t_all, -1),
                             cos_t, sin_t, wq_h, wkv_h, q_scale)
            o = attention(q, k, v, s).reshape(b * t_all, nh * V_DIM)
            src, xn2 = mixer_out(o, mla_wo)
        else:
            assert not combined
            xn = norm_mod(src, norm_g[i, 1], shift, scale)
            u = gated_up(xn, conv_w1, (j,), 0, conv_w1, (j,), 1, d, "glu",
                         bias=(conv_b1[j, :d], conv_b1[j, d:]), out_dtype=F32)
            vv = dwconv_ln_swish(u.reshape(b, s, d), conv_dw[j], conv_dwb[j], conv_ln_g[j], conv_ln_b[j])
            src, xn2 = mixer_out(vv.reshape(-1, d), conv_w2, bias=conv_b2[j])

        if combined and not ctx_out:
            src, combined = latent_only(src), False
            rows = rows_lat
        src = ffn(src, m, rows, i, 2, 1, xn=xn2)

    if combined:
        src = latent_only(src)
    return norm_mod(src, final_g, out_dtype=F32).reshape(b, s, d)
```

```python
import functools
import math
from typing import NamedTuple

import numpy as np
import jax
import jax.numpy as jnp
from jax import lax
from jax.experimental import pallas as pl
from jax.experimental.pallas import tpu as pltpu

F32 = jnp.float32
BF16 = jnp.bfloat16

GRID_W = 64
QK_NOPE = 128
QK_ROPE = 64
V_DIM = 128
ROPE_THETA = 10000.0
FFN_RES_WEIGHT = 0.5
N_MOD = 9
EPS = 1e-6

LANES = 128
SUBLANES = 8
VMEM_LIMIT_BYTES = 58 * 1024 * 1024

MOD_TILE = 256
COND_ROWS = 8
HALO = 16
CAST_ROWS = 256

MXU_TILE = 1024
UP_COLS = 512
DEEP_ROWS = 512
NORM_ROWS = 1024
MLA_DOWN_ROWS = 1024


def _params(*sem):
    return pltpu.CompilerParams(dimension_semantics=sem, vmem_limit_bytes=VMEM_LIMIT_BYTES)


def _tile(n, pref, unit):
    if n <= pref:
        return n
    t = (pref // unit) * unit
    while t > unit and n % t:
        t -= unit
    assert n % t == 0, (n, pref, unit)
    return t


def _cast_weight(src_ref, dst_ref):
    k = src_ref.shape[0]
    rows = _tile(k, CAST_ROWS, SUBLANES)

    def body(c, carry):
        s = pl.multiple_of(c * rows, rows)
        dst_ref[pl.ds(s, rows), :] = src_ref[pl.ds(s, rows), :].astype(BF16)
        return carry

    lax.fori_loop(0, k // rows, body, 0)


def _sigmoid(x):
    return 1.0 / (1.0 + jnp.exp(-x))


def _ada_kernel(c_ref, w_ref, b_ref, o_ref):
    c = c_ref[...]
    s = (c * _sigmoid(c)).astype(BF16)
    o_ref[0] = jnp.dot(s, w_ref[0].astype(BF16), preferred_element_type=F32) + b_ref[0]


def ada_mods(cond, ada_w, ada_b):
    depth, d, n = ada_w.shape
    tn = _tile(n, MXU_TILE, LANES)
    return pl.pallas_call(
        _ada_kernel,
        grid=(depth, n // tn),
        in_specs=[
            pl.BlockSpec((COND_ROWS, d), lambda l, j: (0, 0)),
            pl.BlockSpec((1, d, tn), lambda l, j: (l, 0, j)),
            pl.BlockSpec((1, 1, tn), lambda l, j: (l, 0, j)),
        ],
        out_specs=pl.BlockSpec((1, COND_ROWS, tn), lambda l, j: (l, 0, j)),
        out_shape=jax.ShapeDtypeStruct((depth, COND_ROWS, n), F32),
        compiler_params=_params("arbitrary", "arbitrary"),
        name="ada_mods",
    )(cond, ada_w, ada_b.reshape(depth, 1, n))


class Hidden(NamedTuple):
    arrays: tuple
    t_use: int

    @property
    def split(self):
        if len(self.arrays) == 1:
            return None
        return (self.arrays[0].shape[1] // MOD_TILE, self.arrays[1].shape[1] // MOD_TILE)

    @property
    def shape(self):
        return (self.arrays[0].shape[0] * self.t_use, self.arrays[0].shape[2])


def _flat(h):
    return Hidden((h.reshape(1, *h.shape),), h.shape[0])


def _hidden_tile(src, pref):
    return _tile(src.shape[0] if src.split else src.t_use, pref, MOD_TILE)


def _hidden_specs(src, tm, cols, tile_col):
    if src.split is None:
        nt = src.t_use // tm

        def hmap(*g):
            i, j = tile_col(*g)
            return (i // nt, i % nt, j)

        return [pl.BlockSpec((None, tm, cols), hmap)]
    n_lat, n_ctx = src.split
    n_sub = tm // MOD_TILE
    specs = []
    for r in range(n_sub):
        def xmap(*g, r=r):
            i, j = tile_col(*g)
            t = i * n_sub + r
            return (t // (n_lat + n_ctx), jnp.minimum(t % (n_lat + n_ctx), n_lat - 1), j)

        def cmap(*g, r=r):
            i, j = tile_col(*g)
            t = i * n_sub + r
            return (t // (n_lat + n_ctx), jnp.maximum(t % (n_lat + n_ctx) - n_lat, 0), j)

        specs += [pl.BlockSpec((None, MOD_TILE, cols), xmap), pl.BlockSpec((None, MOD_TILE, cols), cmap)]
    return specs


def _hidden_args(src, specs):
    return list(src.arrays) * (len(specs) // len(src.arrays))


def _hidden_rows(h_refs, r, row_tile, n_sub, split):
    if split is None:
        return h_refs[0][r * MOD_TILE:(r + 1) * MOD_TILE, :]
    n_lat, n_ctx = split
    is_ctx = (row_tile * n_sub + r) % (n_lat + n_ctx) >= n_lat
    return jnp.where(is_ctx, h_refs[2 * r + 1][...], h_refs[2 * r][...])


def _rms(x, g):
    return x * lax.rsqrt(jnp.mean(x * x, axis=-1, keepdims=True) + EPS) * g


def _norm_kernel(*refs, modulate, n_sub, n_h, split):
    h_refs, refs = refs[:n_h], refs[n_h:]
    if modulate:
        g_ref, sh_ref, sc_ref, o_ref = refs
    else:
        g_ref, o_ref = refs
    for r in range(n_sub):
        y = _rms(_hidden_rows(h_refs, r, pl.program_id(0), n_sub, split), g_ref[...])
        if modulate:
            y = y * (1.0 + sc_ref[r]) + sh_ref[r]
        o_ref[r * MOD_TILE:(r + 1) * MOD_TILE, :] = y.astype(o_ref.dtype)


def norm_mod(src, g, shift=None, scale=None, out_dtype=BF16):
    m, d = src.shape
    tm = _hidden_tile(src, NORM_ROWS)
    n_sub = tm // MOD_TILE
    modulate = shift is not None
    h_specs = _hidden_specs(src, tm, d, lambda i: (i, 0))
    in_specs = h_specs + [pl.BlockSpec((1, d), lambda i: (0, 0))]
    args = _hidden_args(src, h_specs) + [g.reshape(1, d)]
    if modulate:
        in_specs += [pl.BlockSpec((n_sub, 1, d), lambda i: (i, 0, 0))] * 2
        args += [shift, scale]
    return pl.pallas_call(
        functools.partial(_norm_kernel, modulate=modulate, n_sub=n_sub, n_h=len(h_specs), split=src.split),
        grid=(m // tm,),
        in_specs=in_specs,
        out_specs=pl.BlockSpec((tm, d), lambda i: (i, 0)),
        out_shape=jax.ShapeDtypeStruct((m, d), out_dtype),
        compiler_params=_params("arbitrary"),
        name="norm_mod",
    )(*args)


def _up_kernel(*refs, act, has_bias):
    if has_bias:
        x_ref, wa_ref, wb_ref, ba_ref, bb_ref, o_ref, wa_s, wb_s = refs
    else:
        x_ref, wa_ref, wb_ref, o_ref, wa_s, wb_s = refs

    @pl.when(pl.program_id(1) == 0)
    def _():
        _cast_weight(wa_ref, wa_s)
        _cast_weight(wb_ref, wb_s)

    x = x_ref[...]
    a = jnp.dot(x, wa_s[...], preferred_element_type=F32)
    b = jnp.dot(x, wb_s[...], preferred_element_type=F32)
    if has_bias:
        a = a + ba_ref[...]
        b = b + bb_ref[...]
    if act == "swiglu":
        o = (a * _sigmoid(a)) * b
    else:
        o = a * _sigmoid(b)
    o_ref[...] = o.astype(o_ref.dtype)


def gated_up(x, wa, wa_idx, wa_col, wb, wb_idx, wb_col, n, act, bias=None, out_dtype=BF16):
    m, k = x.shape
    tm = _tile(m, MXU_TILE, MOD_TILE)
    tn = _tile(n, UP_COLS, LANES)
    nj = n // tn

    def wspec(idx, col):
        lead = (None,) * len(idx)
        return pl.BlockSpec(lead + (k, tn), lambda j, i: idx + (0, col * nj + j))

    in_specs = [pl.BlockSpec((tm, k), lambda j, i: (i, 0)), wspec(wa_idx, wa_col), wspec(wb_idx, wb_col)]
    args = [x, wa, wb]
    if bias is not None:
        ba, bb = bias
        in_specs += [pl.BlockSpec((1, tn), lambda j, i: (0, j))] * 2
        args += [ba.reshape(1, n), bb.reshape(1, n)]
    return pl.pallas_call(
        functools.partial(_up_kernel, act=act, has_bias=bias is not None),
        grid=(nj, m // tm),
        in_specs=in_specs,
        out_specs=pl.BlockSpec((tm, tn), lambda j, i: (i, j)),
        out_shape=jax.ShapeDtypeStruct((m, n), out_dtype),
        scratch_shapes=[pltpu.VMEM((k, tn), BF16), pltpu.VMEM((k, tn), BF16)],
        compiler_params=_params("arbitrary", "arbitrary"),
        name="gated_up_" + act,
    )(*args)


def _res_kernel(*refs, coef, n_sub, has_bias, n_h, split):
    x_ref, w_ref = refs[:2]
    refs = refs[2:]
    if has_bias:
        b_ref, refs = refs[0], refs[1:]
    h_refs, (gate_ref, o_ref, w_s) = refs[:n_h], refs[n_h:]

    @pl.when(pl.program_id(1) == 0)
    def _():
        _cast_weight(w_ref, w_s)

    y = jnp.dot(x_ref[...], w_s[...], preferred_element_type=F32)
    if has_bias:
        y = y + b_ref[...]
    for r in range(n_sub):
        rows = slice(r * MOD_TILE, (r + 1) * MOD_TILE)
        h = _hidden_rows(h_refs, r, pl.program_id(1), n_sub, split)
        o_ref[rows, :] = h + (coef * gate_ref[r]) * y[rows, :]


def proj_residual(x, w, w_idx, src, gate, coef, bias=None):
    m, k = x.shape
    n = src.shape[1]
    assert src.shape[0] == m
    tn = _tile(n, MXU_TILE, LANES)
    if k * tn * 14 <= VMEM_LIMIT_BYTES // 2:
        tm, w_mode = _hidden_tile(src, MXU_TILE), {}
    else:
        tm, w_mode = _hidden_tile(src, DEEP_ROWS), {"pipeline_mode": pl.Buffered(1)}
    n_sub = tm // MOD_TILE
    lead = (None,) * len(w_idx)
    in_specs = [pl.BlockSpec((tm, k), lambda j, i: (i, 0)),
                pl.BlockSpec(lead + (k, tn), lambda j, i: w_idx + (0, j), **w_mode)]
    args = [x, w]
    if bias is not None:
        in_specs.append(pl.BlockSpec((1, tn), lambda j, i: (0, j)))
        args.append(bias.reshape(1, n))
    h_specs = _hidden_specs(src, tm, tn, lambda j, i: (i, j))
    in_specs += h_specs + [pl.BlockSpec((n_sub, 1, tn), lambda j, i: (i, 0, j))]
    args += _hidden_args(src, h_specs) + [gate]
    return pl.pallas_call(
        functools.partial(_res_kernel, coef=coef, n_sub=n_sub, has_bias=bias is not None,
                          n_h=len(h_specs), split=src.split),
        grid=(n // tn, m // tm),
        in_specs=in_specs,
        out_specs=pl.BlockSpec((tm, tn), lambda j, i: (i, j)),
        out_shape=jax.ShapeDtypeStruct((m, n), F32),
        scratch_shapes=[pltpu.VMEM((k, tn), BF16)],
        compiler_params=_params("arbitrary", "arbitrary"),
        name="proj_residual",
    )(*args)


def _res_norm_kernel(*refs, coef, n_sub, has_bias):
    x_ref, w_ref = refs[:2]
    refs = refs[2:]
    if has_bias:
        b_ref, refs = refs[0], refs[1:]
    h_ref, gate_ref, g_ref, sh_ref, sc_ref, o_ref, xn_ref, w_s = refs

    @pl.when(pl.program_id(0) == 0)
    def _():
        _cast_weight(w_ref, w_s)

    for r in range(n_sub):
        rows = slice(r * MOD_TILE, (r + 1) * MOD_TILE)
        y = jnp.dot(x_ref[rows, :], w_s[...], preferred_element_type=F32)
        if has_bias:
            y = y + b_ref[...]
        h = h_ref[rows, :] + (coef * gate_ref[r]) * y
        o_ref[rows, :] = h
        xn_ref[rows, :] = (_rms(h, g_ref[...]) * (1.0 + sc_ref[r]) + sh_ref[r]).astype(BF16)


def proj_residual_norm(x, w, w_idx, h, gate, coef, g, shift, scale, bias=None):
    m, k = x.shape
    n = h.shape[1]
    tm = _tile(m, DEEP_ROWS, MOD_TILE)
    n_sub = tm // MOD_TILE
    lead = (None,) * len(w_idx)
    row = lambda i: (i, 0)
    vec = pl.BlockSpec((1, n), lambda i: (0, 0))
    mod = pl.BlockSpec((n_sub, 1, n), lambda i: (i, 0, 0))
    in_specs = [pl.BlockSpec((tm, k), row),
                pl.BlockSpec(lead + (k, n), lambda i: w_idx + (0, 0), pipeline_mode=pl.Buffered(1))]
    args = [x, w]
    if bias is not None:
        in_specs.append(vec)
        args.append(bias.reshape(1, n))
    in_specs += [pl.BlockSpec((tm, n), row), mod, vec, mod, mod]
    args += [h, gate, g.reshape(1, n), shift, scale]
    return pl.pallas_call(
        functools.partial(_res_norm_kernel, coef=coef, n_sub=n_sub, has_bias=bias is not None),
        grid=(m // tm,),
        in_specs=in_specs,
        out_specs=[pl.BlockSpec((tm, n), row), pl.BlockSpec((tm, n), row)],
        out_shape=[jax.ShapeDtypeStruct((m, n), F32), jax.ShapeDtypeStruct((m, n), BF16)],
        scratch_shapes=[pltpu.VMEM((k, n), BF16)],
        compiler_params=_params("arbitrary"),
        name="proj_residual_norm",
    )(*args)


def _mla_down_kernel(*refs, lora, n_sub, n_h, split):
    h_refs, refs = refs[:n_h], refs[n_h:]
    (g_ref, sh_ref, sc_ref, wq_ref, wkv_ref, sel_ref, gq_ref, gkv_ref, cos_ref, sin_ref,
     cq_ref, ckv_ref, kpe_ref, wq_s, wkv_s) = refs

    @pl.when(pl.program_id(0) == 0)
    def _():
        _cast_weight(wq_ref, wq_s)
        rows = _tile(wkv_ref.shape[0], CAST_ROWS, SUBLANES)

        def body(c, carry):
            r = pl.ds(pl.multiple_of(c * rows, rows), rows)
            w = wkv_ref[r, :].astype(BF16)
            wkv_s[r, :lora] = w[:, :lora]
            wkv_s[r, lora:] = jnp.dot(w[:, lora:], sel_ref[...], preferred_element_type=F32).astype(BF16)
            return carry

        lax.fori_loop(0, wkv_ref.shape[0] // rows, body, 0)

    for r in range(n_sub):
        rows = slice(r * MOD_TILE, (r + 1) * MOD_TILE)
        h = _hidden_rows(h_refs, r, pl.program_id(0), n_sub, split)
        x = (_rms(h, g_ref[...]) * (1.0 + sc_ref[r]) + sh_ref[r]).astype(BF16)
        cq = jnp.dot(x, wq_s[...], preferred_element_type=F32)
        cq_ref[rows, :] = _rms(cq, gq_ref[...]).astype(BF16)
        kv = jnp.dot(x, wkv_s[...], preferred_element_type=F32)
        ckv_ref[rows, :] = _rms(kv[:, :lora], gkv_ref[...]).astype(BF16)
        kpe = kv[:, lora:lora + LANES] * cos_ref[rows, :] + kv[:, lora + LANES:] * sin_ref[rows, :]
        kpe_ref[rows, :] = kpe.astype(BF16)


def mla_down(src, g, shift, scale, wdq, wdkv, gq, gkv, cos, sin):
    m, d = src.shape
    q_lora = wdq.shape[1]
    kv_lora = gkv.shape[0]
    sel = jnp.asarray(_rope_select_pair(), BF16)
    nkv = kv_lora + sel.shape[1]
    tm = _hidden_tile(src, MLA_DOWN_ROWS)
    n_sub = tm // MOD_TILE
    const = lambda i: (0, 0)
    row = lambda i: (i, 0)
    mod = pl.BlockSpec((n_sub, 1, d), lambda i: (i, 0, 0))
    h_specs = _hidden_specs(src, tm, d, lambda i: (i, 0))
    return pl.pallas_call(
        functools.partial(_mla_down_kernel, lora=kv_lora, n_sub=n_sub, n_h=len(h_specs), split=src.split),
        grid=(m // tm,),
        in_specs=h_specs + [pl.BlockSpec((1, d), const), mod, mod,
                            pl.BlockSpec((d, q_lora), const),
                            pl.BlockSpec(wdkv.shape, const),
                            pl.BlockSpec(sel.shape, const),
                            pl.BlockSpec((1, q_lora), const),
                            pl.BlockSpec((1, kv_lora), const),
                            pl.BlockSpec((tm, LANES), row),
                            pl.BlockSpec((tm, LANES), row)],
        out_specs=[pl.BlockSpec((tm, q_lora), row),
                   pl.BlockSpec((tm, kv_lora), row),
                   pl.BlockSpec((tm, LANES), row)],
        out_shape=[jax.ShapeDtypeStruct((m, q_lora), BF16),
                   jax.ShapeDtypeStruct((m, kv_lora), BF16),
                   jax.ShapeDtypeStruct((m, LANES), BF16)],
        scratch_shapes=[pltpu.VMEM((d, q_lora), BF16), pltpu.VMEM((d, nkv), BF16)],
        compiler_params=_params("arbitrary"),
        name="mla_down",
    )(*_hidden_args(src, h_specs), g.reshape(1, d), shift, scale,
      wdq, wdkv, sel, gq.reshape(1, -1), gkv.reshape(1, -1), cos, sin)


def _wq_layout_kernel(w_ref, sel_ref, o_ref):
    o_ref[...] = jnp.dot(w_ref[...].astype(BF16), sel_ref[...], preferred_element_type=F32).astype(BF16)


def mla_wq_layout(wuq, nh):
    q_lora = wuq.shape[0]
    dh = QK_NOPE + QK_ROPE
    rope = _rope_select()
    one = np.zeros((dh, QK_NOPE + rope.shape[1]), np.float32)
    one[:QK_NOPE, :QK_NOPE] = np.eye(QK_NOPE)
    one[QK_NOPE:, QK_NOPE:] = rope
    pair = np.zeros((2 * one.shape[0], 2 * one.shape[1]), np.float32)
    pair[:one.shape[0], :one.shape[1]] = one
    pair[one.shape[0]:, one.shape[1]:] = one
    assert nh % 2 == 0
    return pl.pallas_call(
        _wq_layout_kernel,
        grid=(nh // 2,),
        in_specs=[pl.BlockSpec((q_lora, pair.shape[0]), lambda h: (0, h)),
                  pl.BlockSpec(pair.shape, lambda h: (0, 0))],
        out_specs=pl.BlockSpec((q_lora, pair.shape[1]), lambda h: (0, h)),
        out_shape=jax.ShapeDtypeStruct((q_lora, nh * one.shape[1]), BF16),
        compiler_params=_params("arbitrary"),
        name="mla_wq_layout",
    )(wuq, jnp.asarray(pair, BF16))


MLA_UP_HEADS = 4


def _mla_up_kernel(cq_ref, ckv_ref, kpe_ref, cos_ref, sin_ref, wq_ref, wkv_ref,
                   q_ref, k_ref, v_ref, *, q_scale):
    dq, dkv = QK_NOPE + LANES, QK_NOPE + V_DIM
    qf = jnp.dot(cq_ref[0], wq_ref[...], preferred_element_type=F32)
    kvf = jnp.dot(ckv_ref[0], wkv_ref[...].astype(BF16), preferred_element_type=F32)
    for hh in range(q_ref.shape[1]):
        q_pe = _rope(qf[:, hh * dq + QK_NOPE:(hh + 1) * dq], cos_ref[...], sin_ref[...])
        q_ref[0, hh, :, :QK_NOPE] = (qf[:, hh * dq:hh * dq + QK_NOPE] * q_scale).astype(BF16)
        q_ref[0, hh, :, QK_NOPE:] = (q_pe * q_scale).astype(BF16)
        k_ref[0, hh, :, :QK_NOPE] = kvf[:, hh * dkv:hh * dkv + QK_NOPE].astype(BF16)
        k_ref[0, hh, :, QK_NOPE:] = kpe_ref[0]
        v_ref[0, hh] = kvf[:, hh * dkv + QK_NOPE:(hh + 1) * dkv].astype(BF16)


def mla_up(cq, ckv, kpe, cos, sin, wq_h, wkv_h, q_scale):
    b, t, q_lora = cq.shape
    kv_lora = ckv.shape[2]
    dqk = QK_NOPE + LANES
    nh = wq_h.shape[1] // dqk
    hb = _tile(nh, MLA_UP_HEADS, 1)
    tok = lambda bi, hi: (bi, 0, 0)
    out = lambda bi, hi: (bi, hi, 0, 0)
    return pl.pallas_call(
        functools.partial(_mla_up_kernel, q_scale=q_scale),
        grid=(b, nh // hb),
        in_specs=[pl.BlockSpec((1, t, q_lora), tok),
                  pl.BlockSpec((1, t, kv_lora), tok),
                  pl.BlockSpec((1, t, LANES), tok),
                  pl.BlockSpec((t, LANES), lambda bi, hi: (0, 0)),
                  pl.BlockSpec((t, LANES), lambda bi, hi: (0, 0)),
                  pl.BlockSpec((q_lora, hb * dqk), lambda bi, hi: (0, hi)),
                  pl.BlockSpec((kv_lora, hb * (QK_NOPE + V_DIM)), lambda bi, hi: (0, hi))],
        out_specs=[pl.BlockSpec((1, hb, t, dqk), out),
                   pl.BlockSpec((1, hb, t, dqk), out),
                   pl.BlockSpec((1, hb, t, V_DIM), out)],
        out_shape=[jax.ShapeDtypeStruct((b, nh, t, dqk), BF16),
                   jax.ShapeDtypeStruct((b, nh, t, dqk), BF16),
                   jax.ShapeDtypeStruct((b, nh, t, V_DIM), BF16)],
        compiler_params=_params("arbitrary", "arbitrary"),
        name="mla_up",
    )(cq, ckv, kpe, cos, sin, wq_h, wkv_h)


ATTN_HEADS = 8
ATTN_TILE = 512
ATTN_ROWS = 256


def _attn_kernel(q_ref, k_ref, v_ref, o_ref, *, n_lat, n_ctx):
    def attend(n_rows, key_rows):
        for hh in range(q_ref.shape[1]):
            k = k_ref[0, hh, key_rows, :]
            v = v_ref[0, hh, key_rows, :]
            v1 = jnp.concatenate([v, jnp.ones_like(v)], axis=-1)
            for r in range(n_rows // ATTN_ROWS):
                rows = slice(r * ATTN_ROWS, (r + 1) * ATTN_ROWS)
                s = lax.dot_general(q_ref[0, hh, rows, :], k, (((1,), (1,)), ((), ())),
                                    preferred_element_type=F32)
                p = jnp.exp2(s - jnp.max(s, axis=-1, keepdims=True))
                o = jnp.dot(p.astype(BF16), v1, preferred_element_type=F32)
                o_ref[0, rows, hh * V_DIM:(hh + 1) * V_DIM] = (o[:, :V_DIM] / o[:, V_DIM:]).astype(BF16)

    @pl.when(pl.program_id(2) == 0)
    def _():
        attend(n_ctx, slice(n_lat, n_lat + n_ctx))

    @pl.when(pl.program_id(2) > 0)
    def _():
        attend(ATTN_TILE, slice(0, n_lat + n_ctx))


def attention(q, k, v, n_lat):
    b, nh, t, dqk = q.shape
    n_ctx = t - n_lat
    assert n_lat % ATTN_TILE == 0 and 0 < n_ctx <= ATTN_TILE and n_ctx % ATTN_ROWS == 0
    hb = _tile(nh, ATTN_HEADS, 1)
    n_tiles = n_lat // ATTN_TILE + 1
    tile = lambda i: (i + n_tiles - 1) % n_tiles
    return pl.pallas_call(
        functools.partial(_attn_kernel, n_lat=n_lat, n_ctx=n_ctx),
        grid=(b, nh // hb, n_tiles),
        in_specs=[pl.BlockSpec((1, hb, ATTN_TILE, dqk), lambda bi, hi, i: (bi, hi, tile(i), 0)),
                  pl.BlockSpec((1, hb, t, dqk), lambda bi, hi, i: (bi, hi, 0, 0)),
                  pl.BlockSpec((1, hb, t, V_DIM), lambda bi, hi, i: (bi, hi, 0, 0))],
        out_specs=pl.BlockSpec((1, ATTN_TILE, hb * V_DIM), lambda bi, hi, i: (bi, tile(i), hi)),
        out_shape=jax.ShapeDtypeStruct((b, t, nh * V_DIM), BF16),
        compiler_params=_params("arbitrary", "arbitrary", "arbitrary"),
        name="attention",
    )(q, k, v)


CONV_TILE = 128
CONV_GROUPS = CONV_TILE // SUBLANES


def _dwconv_kernel(prev_ref, cur_ref, next_ref, dw_ref, dwb_ref, g_ref, b_ref, o_ref, buf, cv, *, width):
    i = pl.program_id(1)
    tt = cur_ref.shape[1]
    d = cur_ref.shape[2]
    pad = width // 2
    first = i == 0
    final = i == pl.num_programs(1) - 1
    for c in range(d // LANES):
        cols = slice(c * LANES, (c + 1) * LANES)
        buf[c, 0:HALO, :] = jnp.where(first, 0.0, prev_ref[0, :, cols])
        buf[c, HALO:HALO + tt, :] = cur_ref[0, :, cols]
        buf[c, HALO + tt:, :] = jnp.where(final, 0.0, next_ref[0, :, cols])
        acc = [jnp.broadcast_to(dwb_ref[:, cols], (SUBLANES, LANES)) for _ in range(CONV_GROUPS)]
        for m in range(-pad, CONV_GROUPS + pad):
            xm = buf[c, pl.ds(HALO + m, SUBLANES, stride=CONV_GROUPS), :]
            for j in range(CONV_GROUPS):
                kk = m - j + pad
                if 0 <= kk < width:
                    acc[j] = acc[j] + xm * dw_ref[kk, :, cols]
        for j in range(CONV_GROUPS):
            cv[c, pl.ds(j, SUBLANES, stride=CONV_GROUPS), :] = acc[j]
    a = jnp.concatenate([cv[c] for c in range(d // LANES)], axis=-1)
    mu = jnp.mean(a, axis=-1, keepdims=True)
    xc = a - mu
    var = jnp.mean(xc * xc, axis=-1, keepdims=True)
    y = xc * lax.rsqrt(var + EPS) * g_ref[...] + b_ref[...]
    o_ref[0] = (y * _sigmoid(y)).astype(o_ref.dtype)


def dwconv_ln_swish(u, dw, dwb, ln_g, ln_b):
    b, t, d = u.shape
    width = dw.shape[0]
    tt = CONV_TILE
    assert t % tt == 0 and width // 2 < HALO and d % LANES == 0
    nh = tt // HALO
    last = t // HALO - 1
    vec = lambda bi, i: (0, 0)
    dw8 = jnp.broadcast_to(dw[:, None, :], (width, SUBLANES, d))
    return pl.pallas_call(
        functools.partial(_dwconv_kernel, width=width),
        grid=(b, t // tt),
        in_specs=[pl.BlockSpec((1, HALO, d), lambda bi, i: (bi, jnp.maximum(i * nh - 1, 0), 0)),
                  pl.BlockSpec((1, tt, d), lambda bi, i: (bi, i, 0)),
                  pl.BlockSpec((1, HALO, d), lambda bi, i: (bi, jnp.minimum((i + 1) * nh, last), 0)),
                  pl.BlockSpec((width, SUBLANES, d), lambda bi, i: (0, 0, 0)),
                  pl.BlockSpec((1, d), vec), pl.BlockSpec((1, d), vec), pl.BlockSpec((1, d), vec)],
        out_specs=pl.BlockSpec((1, tt, d), lambda bi, i: (bi, i, 0)),
        out_shape=jax.ShapeDtypeStruct((b, t, d), BF16),
        scratch_shapes=[pltpu.VMEM((d // LANES, tt + 2 * HALO, LANES), F32),
                        pltpu.VMEM((d // LANES, tt, LANES), F32)],
        compiler_params=_params("arbitrary", "arbitrary"),
        name="dwconv_ln_swish",
    )(u, u, u, dw8, dwb.reshape(1, d), ln_g.reshape(1, d), ln_b.reshape(1, d))


def _rope_tables(seq, ctx_len):
    n_axis = QK_ROPE // 4
    t = jnp.arange(seq, dtype=jnp.int32)
    row = (t // GRID_W).astype(F32)
    col = (t % GRID_W).astype(F32)
    freqs = ROPE_THETA ** (-jnp.arange(n_axis, dtype=F32) / n_axis)
    ang = jnp.concatenate([row[:, None] * freqs, col[:, None] * freqs], axis=-1)
    ang = jnp.concatenate([ang, jnp.zeros((ctx_len, QK_ROPE // 2), F32)], axis=0)
    cos, sin = jnp.cos(ang), jnp.sin(ang)
    zeros = jnp.zeros((seq + ctx_len, LANES - QK_ROPE), F32)
    cos_t = jnp.concatenate([cos, cos, zeros], axis=-1)
    sin_t = jnp.concatenate([-sin, sin, zeros], axis=-1)
    return cos_t, sin_t


def _rope_select():
    half = QK_ROPE // 2
    sel = np.zeros((QK_ROPE, LANES), np.float32)
    for p in range(half):
        sel[2 * p, p] = sel[2 * p, QK_ROPE + p] = 1.0
        sel[2 * p + 1, half + p] = sel[2 * p + 1, QK_ROPE + half + p] = 1.0
    return sel


def _rope_select_pair():
    sel = _rope_select()
    out = np.zeros((QK_ROPE, 2 * LANES), np.float32)
    out[:, :QK_ROPE] = sel[:, :QK_ROPE]
    out[:, LANES:LANES + QK_ROPE] = sel[:, QK_ROPE // 2:QK_ROPE // 2 + QK_ROPE]
    return out


def _rope(g, cos, sin):
    return g * cos + pltpu.roll(g, QK_ROPE // 2, axis=1) * sin


def _mod_tiles(m, layout, k, d):
    b, n_lat, n_ctx = layout
    mk = m[:, k * d:(k + 1) * d]
    parts = [jnp.broadcast_to(mk[:b, None, :], (b, n_lat, d))]
    if n_ctx:
        parts.append(jnp.broadcast_to(mk[b][None, None, :], (b, n_ctx, d)))
    return jnp.concatenate(parts, axis=1).reshape(b * (n_lat + n_ctx), 1, d)


def kernel(x, c, ctx, c_ctx, ada_w, ada_b, norm_g, ffn_w1, ffn_w3, ffn_w2, mla_wdq, mla_gq, mla_wuq, mla_wdkv, mla_gkv, mla_wukv, mla_wo, conv_w1, conv_b1, conv_dw, conv_dwb, conv_ln_g, conv_ln_b, conv_w2, conv_b2, final_g):
    b, s, d = x.shape
    lc = ctx.shape[1]
    depth = ada_w.shape[0]
    d_ff = ffn_w1.shape[-1]
    nh = mla_wo.shape[1] // V_DIM
    assert s % MOD_TILE == 0 and lc % MOD_TILE == 0 and b < COND_ROWS

    cond = jnp.concatenate([c, c_ctx[None, :], jnp.zeros((COND_ROWS - b - 1, d), F32)], axis=0)
    mods = ada_mods(cond, ada_w, ada_b)

    t_all = s + lc
    rows_all = (b, s // MOD_TILE, lc // MOD_TILE)
    rows_lat = (b, s // MOD_TILE, 0)

    src = Hidden((x, ctx), t_all)
    combined = True

    def ffn(src, m, rows, i, k, f, xn=None):
        if xn is None:
            xn = norm_mod(src, norm_g[i, k], _mod_tiles(m, rows, 3 * k, d), _mod_tiles(m, rows, 3 * k + 1, d))
        g = gated_up(xn, ffn_w1, (i, f), 0, ffn_w3, (i, f), 0, d_ff, "swiglu")
        return _flat(proj_residual(g, ffn_w2, (i, f), src, _mod_tiles(m, rows, 3 * k + 2, d), FFN_RES_WEIGHT))

    def latent_only(src):
        if src.split:
            return Hidden((src.arrays[0],), s)
        return Hidden((src.arrays[0].reshape(b, t_all, d),), s)

    for i in range(depth):
        last = i == depth - 1
        mixer = i % 2
        j = i // 2
        ctx_into_mixer = (not last) or mixer == 0
        ctx_out = not last
        m = mods[i]
        if combined and not ctx_into_mixer:
            src, combined = latent_only(src), False
        rows = rows_all if combined else rows_lat

        src = ffn(src, m, rows, i, 0, 0)

        shift, scale, gate = (_mod_tiles(m, rows, k, d) for k in (3, 4, 5))
        fuse_norm = not (combined and not ctx_out)

        def mixer_out(xo, w, bias=None):
            if not fuse_norm:
                return _flat(proj_residual(xo, w, (j,), src, gate, 1.0, bias=bias)), None
            h_new, xn2 = proj_residual_norm(xo, w, (j,), src.arrays[0].reshape(-1, d), gate, 1.0, norm_g[i, 2],
                                            _mod_tiles(m, rows, 6, d), _mod_tiles(m, rows, 7, d), bias=bias)
            return _flat(h_new), xn2

        if mixer == 0:
            assert combined
            cos_t, sin_t = _rope_tables(s, lc)
            cq, ckv, kpe = mla_down(src, norm_g[i, 1], shift, scale, mla_wdq[j], mla_wdkv[j], mla_gq[j], mla_gkv[j],
                                    jnp.tile(cos_t, (b, 1)), jnp.tile(sin_t, (b, 1)))
            wq_h = mla_wq_layout(mla_wuq[j], nh)
            wkv_h = mla_wukv[j]
            q_scale = (QK_NOPE + QK_ROPE) ** -0.5 * math.log2(math.e)
            q, k, v = mla_up(cq.reshape(b, t_all, -1), ckv.reshape(b, t_all, -1), kpe.reshape(b, t_all, -1),
                             cos_t, sin_t, wq_h, wkv_h, q_scale)
            o = attention(q, k, v, s).reshape(b * t_all, nh * V_DIM)
            src, xn2 = mixer_out(o, mla_wo)
        else:
            assert not combined
            xn = norm_mod(src, norm_g[i, 1], shift, scale)
            u = gated_up(xn, conv_w1, (j,), 0, conv_w1, (j,), 1, d, "glu",
                         bias=(conv_b1[j, :d], conv_b1[j, d:]), out_dtype=F32)
            vv = dwconv_ln_swish(u.reshape(b, s, d), conv_dw[j], conv_dwb[j], conv_ln_g[j], conv_ln_b[j])
            src, xn2 = mixer_out(vv.reshape(-1, d), conv_w2, bias=conv_b2[j])

        if combined and not ctx_out:
            src, combined = latent_only(src), False
            rows = rows_lat
        src = ffn(src, m, rows, i, 2, 1, xn=xn2)

    if combined:
        src = latent_only(src)
    return norm_mod(src, final_g, out_dtype=F32).reshape(b, s, d)
```

```python
import functools
import math
from typing import NamedTuple

import numpy as np
import jax
import jax.numpy as jnp
from jax import lax
from jax.experimental import pallas as pl
from jax.experimental.pallas import tpu as pltpu

F32 = jnp.float32
BF16 = jnp.bfloat16

GRID_W = 64
QK_NOPE = 128
QK_ROPE = 64
V_DIM = 128
ROPE_THETA = 10000.0
FFN_RES_WEIGHT = 0.5
N_MOD = 9
EPS = 1e-6

LANES = 128
SUBLANES = 8
VMEM_LIMIT_BYTES = 58 * 1024 * 1024

MOD_TILE = 256
COND_ROWS = 8
HALO = 16
CAST_ROWS = 256

MXU_TILE = 1024
UP_COLS = 512
DEEP_ROWS = 512
NORM_ROWS = 1024
MLA_DOWN_ROWS = 1024


def _params(*sem):
    return pltpu.CompilerParams(dimension_semantics=sem, vmem_limit_bytes=VMEM_LIMIT_BYTES)


def _tile(n, pref, unit):
    if n <= pref:
        return n
    t = (pref // unit) * unit
    while t > unit and n % t:
        t -= unit
    assert n % t == 0, (n, pref, unit)
    return t


def _cast_weight(src_ref, dst_ref):
    k = src_ref.shape[0]
    rows = _tile(k, CAST_ROWS, SUBLANES)

    def body(c, carry):
        s = pl.multiple_of(c * rows, rows)
        dst_ref[pl.ds(s, rows), :] = src_ref[pl.ds(s, rows), :].astype(BF16)
        return carry

    lax.fori_loop(0, k // rows, body, 0)


def _sigmoid(x):
    return 1.0 / (1.0 + jnp.exp(-x))


def _ada_kernel(c_ref, w_ref, b_ref, o_ref):
    c = c_ref[...]
    s = (c * _sigmoid(c)).astype(BF16)
    o_ref[0] = jnp.dot(s, w_ref[0].astype(BF16), preferred_element_type=F32) + b_ref[0]


def ada_mods(cond, ada_w, ada_b):
    depth, d, n = ada_w.shape
    tn = _tile(n, MXU_TILE, LANES)
    return pl.pallas_call(
        _ada_kernel,
        grid=(depth, n // tn),
        in_specs=[
            pl.BlockSpec((COND_ROWS, d), lambda l, j: (0, 0)),
            pl.BlockSpec((1, d, tn), lambda l, j: (l, 0, j)),
            pl.BlockSpec((1, 1, tn), lambda l, j: (l, 0, j)),
        ],
        out_specs=pl.BlockSpec((1, COND_ROWS, tn), lambda l, j: (l, 0, j)),
        out_shape=jax.ShapeDtypeStruct((depth, COND_ROWS, n), F32),
        compiler_params=_params("arbitrary", "arbitrary"),
        name="ada_mods",
    )(cond, ada_w, ada_b.reshape(depth, 1, n))


class Hidden(NamedTuple):
    arrays: tuple
    t_use: int

    @property
    def split(self):
        if len(self.arrays) == 1:
            return None
        return (self.arrays[0].shape[1] // MOD_TILE, self.arrays[1].shape[1] // MOD_TILE)

    @property
    def shape(self):
        return (self.arrays[0].shape[0] * self.t_use, self.arrays[0].shape[2])


def _flat(h):
    return Hidden((h.reshape(1, *h.shape),), h.shape[0])


def _hidden_tile(src, pref):
    return _tile(src.shape[0] if src.split else src.t_use, pref, MOD_TILE)


def _hidden_specs(src, tm, cols, tile_col):
    if src.split is None:
        nt = src.t_use // tm

        def hmap(*g):
            i, j = tile_col(*g)
            return (i // nt, i % nt, j)

        return [pl.BlockSpec((None, tm, cols), hmap)]
    n_lat, n_ctx = src.split
    n_sub = tm // MOD_TILE
    specs = []
    for r in range(n_sub):
        def xmap(*g, r=r):
            i, j = tile_col(*g)
            t = i * n_sub + r
            return (t // (n_lat + n_ctx), jnp.minimum(t % (n_lat + n_ctx), n_lat - 1), j)

        def cmap(*g, r=r):
            i, j = tile_col(*g)
            t = i * n_sub + r
            return (t // (n_lat + n_ctx), jnp.maximum(t % (n_lat + n_ctx) - n_lat, 0), j)

        specs += [pl.BlockSpec((None, MOD_TILE, cols), xmap), pl.BlockSpec((None, MOD_TILE, cols), cmap)]
    return specs


def _hidden_args(src, specs):
    return list(src.arrays) * (len(specs) // len(src.arrays))


def _hidden_rows(h_refs, r, row_tile, n_sub, split):
    if split is None:
        return h_refs[0][r * MOD_TILE:(r + 1) * MOD_TILE, :]
    n_lat, n_ctx = split
    is_ctx = (row_tile * n_sub + r) % (n_lat + n_ctx) >= n_lat
    return jnp.where(is_ctx, h_refs[2 * r + 1][...], h_refs[2 * r][...])


def _rms(x, g):
    return x * lax.rsqrt(jnp.mean(x * x, axis=-1, keepdims=True) + EPS) * g


def _norm_kernel(*refs, modulate, n_sub, n_h, split):
    h_refs, refs = refs[:n_h], refs[n_h:]
    if modulate:
        g_ref, sh_ref, sc_ref, o_ref = refs
    else:
        g_ref, o_ref = refs
    for r in range(n_sub):
        y = _rms(_hidden_rows(h_refs, r, pl.program_id(0), n_sub, split), g_ref[...])
        if modulate:
            y = y * (1.0 + sc_ref[r]) + sh_ref[r]
        o_ref[r * MOD_TILE:(r + 1) * MOD_TILE, :] = y.astype(o_ref.dtype)


def norm_mod(src, g, shift=None, scale=None, out_dtype=BF16):
    m, d = src.shape
    tm = _hidden_tile(src, NORM_ROWS)
    n_sub = tm // MOD_TILE
    modulate = shift is not None
    h_specs = _hidden_specs(src, tm, d, lambda i: (i, 0))
    in_specs = h_specs + [pl.BlockSpec((1, d), lambda i: (0, 0))]
    args = _hidden_args(src, h_specs) + [g.reshape(1, d)]
    if modulate:
        in_specs += [pl.BlockSpec((n_sub, 1, d), lambda i: (i, 0, 0))] * 2
        args += [shift, scale]
    return pl.pallas_call(
        functools.partial(_norm_kernel, modulate=modulate, n_sub=n_sub, n_h=len(h_specs), split=src.split),
        grid=(m // tm,),
        in_specs=in_specs,
        out_specs=pl.BlockSpec((tm, d), lambda i: (i, 0)),
        out_shape=jax.ShapeDtypeStruct((m, d), out_dtype),
        compiler_params=_params("arbitrary"),
        name="norm_mod",
    )(*args)


def _up_kernel(*refs, act, has_bias):
    if has_bias:
        x_ref, wa_ref, wb_ref, ba_ref, bb_ref, o_ref, wa_s, wb_s = refs
    else:
        x_ref, wa_ref, wb_ref, o_ref, wa_s, wb_s = refs

    @pl.when(pl.program_id(1) == 0)
    def _():
        _cast_weight(wa_ref, wa_s)
        _cast_weight(wb_ref, wb_s)

    x = x_ref[...]
    a = jnp.dot(x, wa_s[...], preferred_element_type=F32)
    b = jnp.dot(x, wb_s[...], preferred_element_type=F32)
    if has_bias:
        a = a + ba_ref[...]
        b = b + bb_ref[...]
    if act == "swiglu":
        o = (a * _sigmoid(a)) * b
    else:
        o = a * _sigmoid(b)
    o_ref[...] = o.astype(o_ref.dtype)


def gated_up(x, wa, wa_idx, wa_col, wb, wb_idx, wb_col, n, act, bias=None, out_dtype=BF16):
    m, k = x.shape
    tm = _tile(m, MXU_TILE, MOD_TILE)
    tn = _tile(n, UP_COLS, LANES)
    nj = n // tn

    def wspec(idx, col):
        lead = (None,) * len(idx)
        return pl.BlockSpec(lead + (k, tn), lambda j, i: idx + (0, col * nj + j))

    in_specs = [pl.BlockSpec((tm, k), lambda j, i: (i, 0)), wspec(wa_idx, wa_col), wspec(wb_idx, wb_col)]
    args = [x, wa, wb]
    if bias is not None:
        ba, bb = bias
        in_specs += [pl.BlockSpec((1, tn), lambda j, i: (0, j))] * 2
        args += [ba.reshape(1, n), bb.reshape(1, n)]
    return pl.pallas_call(
        functools.partial(_up_kernel, act=act, has_bias=bias is not None),
        grid=(nj, m // tm),
        in_specs=in_specs,
        out_specs=pl.BlockSpec((tm, tn), lambda j, i: (i, j)),
        out_shape=jax.ShapeDtypeStruct((m, n), out_dtype),
        scratch_shapes=[pltpu.VMEM((k, tn), BF16), pltpu.VMEM((k, tn), BF16)],
        compiler_params=_params("arbitrary", "arbitrary"),
        name="gated_up_" + act,
    )(*args)


def _res_kernel(*refs, coef, n_sub, has_bias, n_h, split):
    x_ref, w_ref = refs[:2]
    refs = refs[2:]
    if has_bias:
        b_ref, refs = refs[0], refs[1:]
    h_refs, (gate_ref, o_ref, w_s) = refs[:n_h], refs[n_h:]

    @pl.when(pl.program_id(1) == 0)
    def _():
        _cast_weight(w_ref, w_s)

    y = jnp.dot(x_ref[...], w_s[...], preferred_element_type=F32)
    if has_bias:
        y = y + b_ref[...]
    for r in range(n_sub):
        rows = slice(r * MOD_TILE, (r + 1) * MOD_TILE)
        h = _hidden_rows(h_refs, r, pl.program_id(1), n_sub, split)
        o_ref[rows, :] = h + (coef * gate_ref[r]) * y[rows, :]


def proj_residual(x, w, w_idx, src, gate, coef, bias=None):
    m, k = x.shape
    n = src.shape[1]
    assert src.shape[0] == m
    tn = _tile(n, MXU_TILE, LANES)
    if k * tn * 14 <= VMEM_LIMIT_BYTES // 2:
        tm, w_mode = _hidden_tile(src, MXU_TILE), {}
    else:
        tm, w_mode = _hidden_tile(src, DEEP_ROWS), {"pipeline_mode": pl.Buffered(1)}
    n_sub = tm // MOD_TILE
    lead = (None,) * len(w_idx)
    in_specs = [pl.BlockSpec((tm, k), lambda j, i: (i, 0)),
                pl.BlockSpec(lead + (k, tn), lambda j, i: w_idx + (0, j), **w_mode)]
    args = [x, w]
    if bias is not None:
        in_specs.append(pl.BlockSpec((1, tn), lambda j, i: (0, j)))
        args.append(bias.reshape(1, n))
    h_specs = _hidden_specs(src, tm, tn, lambda j, i: (i, j))
    in_specs += h_specs + [pl.BlockSpec((n_sub, 1, tn), lambda j, i: (i, 0, j))]
    args += _hidden_args(src, h_specs) + [gate]
    return pl.pallas_call(
        functools.partial(_res_kernel, coef=coef, n_sub=n_sub, has_bias=bias is not None,
                          n_h=len(h_specs), split=src.split),
        grid=(n // tn, m // tm),
        in_specs=in_specs,
        out_specs=pl.BlockSpec((tm, tn), lambda j, i: (i, j)),
        out_shape=jax.ShapeDtypeStruct((m, n), F32),
        scratch_shapes=[pltpu.VMEM((k, tn), BF16)],
        compiler_params=_params("arbitrary", "arbitrary"),
        name="proj_residual",
    )(*args)


def _res_norm_kernel(*refs, coef, n_sub, has_bias):
    x_ref, w_ref = refs[:2]
    refs = refs[2:]
    if has_bias:
        b_ref, refs = refs[0], refs[1:]
    h_ref, gate_ref, g_ref, sh_ref, sc_ref, o_ref, xn_ref, w_s = refs

    @pl.when(pl.program_id(0) == 0)
    def _():
        _cast_weight(w_ref, w_s)

    for r in range(n_sub):
        rows = slice(r * MOD_TILE, (r + 1) * MOD_TILE)
        y = jnp.dot(x_ref[rows, :], w_s[...], preferred_element_type=F32)
        if has_bias:
            y = y + b_ref[...]
        h = h_ref[rows, :] + (coef * gate_ref[r]) * y
        o_ref[rows, :] = h
        xn_ref[rows, :] = (_rms(h, g_ref[...]) * (1.0 + sc_ref[r]) + sh_ref[r]).astype(BF16)


def proj_residual_norm(x, w, w_idx, h, gate, coef, g, shift, scale, bias=None):
    m, k = x.shape
    n = h.shape[1]
    tm = _tile(m, DEEP_ROWS, MOD_TILE)
    n_sub = tm // MOD_TILE
    lead = (None,) * len(w_idx)
    row = lambda i: (i, 0)
    vec = pl.BlockSpec((1, n), lambda i: (0, 0))
    mod = pl.BlockSpec((n_sub, 1, n), lambda i: (i, 0, 0))
    in_specs = [pl.BlockSpec((tm, k), row),
                pl.BlockSpec(lead + (k, n), lambda i: w_idx + (0, 0), pipeline_mode=pl.Buffered(1))]
    args = [x, w]
    if bias is not None:
        in_specs.append(vec)
        args.append(bias.reshape(1, n))
    in_specs += [pl.BlockSpec((tm, n), row), mod, vec, mod, mod]
    args += [h, gate, g.reshape(1, n), shift, scale]
    return pl.pallas_call(
        functools.partial(_res_norm_kernel, coef=coef, n_sub=n_sub, has_bias=bias is not None),
        grid=(m // tm,),
        in_specs=in_specs,
        out_specs=[pl.BlockSpec((tm, n), row), pl.BlockSpec((tm, n), row)],
        out_shape=[jax.ShapeDtypeStruct((m, n), F32), jax.ShapeDtypeStruct((m, n), BF16)],
        scratch_shapes=[pltpu.VMEM((k, n), BF16)],
        compiler_params=_params("arbitrary"),
        name="proj_residual_norm",
    )(*args)


def _mla_down_kernel(*refs, lora, n_sub, n_h, split):
    h_refs, refs = refs[:n_h], refs[n_h:]
    (g_ref, sh_ref, sc_ref, wq_ref, wkv_ref, sel_ref, gq_ref, gkv_ref, cos_ref, sin_ref,
     cq_ref, ckv_ref, kpe_ref, wq_s, wkv_s) = refs

    @pl.when(pl.program_id(0) == 0)
    def _():
        _cast_weight(wq_ref, wq_s)
        rows = _tile(wkv_ref.shape[0], CAST_ROWS, SUBLANES)

        def body(c, carry):
            r = pl.ds(pl.multiple_of(c * rows, rows), rows)
            w = wkv_ref[r, :].astype(BF16)
            wkv_s[r, :lora] = w[:, :lora]
            wkv_s[r, lora:] = jnp.dot(w[:, lora:], sel_ref[...], preferred_element_type=F32).astype(BF16)
            return carry

        lax.fori_loop(0, wkv_ref.shape[0] // rows, body, 0)

    for r in range(n_sub):
        rows = slice(r * MOD_TILE, (r + 1) * MOD_TILE)
        h = _hidden_rows(h_refs, r, pl.program_id(0), n_sub, split)
        x = (_rms(h, g_ref[...]) * (1.0 + sc_ref[r]) + sh_ref[r]).astype(BF16)
        cq = jnp.dot(x, wq_s[...], preferred_element_type=F32)
        cq_ref[rows, :] = _rms(cq, gq_ref[...]).astype(BF16)
        kv = jnp.dot(x, wkv_s[...], preferred_element_type=F32)
        ckv_ref[rows, :] = _rms(kv[:, :lora], gkv_ref[...]).astype(BF16)
        kpe = kv[:, lora:lora + LANES] * cos_ref[rows, :] + kv[:, lora + LANES:] * sin_ref[rows, :]
        kpe_ref[rows, :] = kpe.astype(BF16)


def mla_down(src, g, shift, scale, wdq, wdkv, gq, gkv, cos, sin):
    m, d = src.shape
    q_lora = wdq.shape[1]
    kv_lora = gkv.shape[0]
    sel = jnp.asarray(_rope_select_pair(), BF16)
    nkv = kv_lora + sel.shape[1]
    tm = _hidden_tile(src, MLA_DOWN_ROWS)
    n_sub = tm // MOD_TILE
    const = lambda i: (0, 0)
    row = lambda i: (i, 0)
    mod = pl.BlockSpec((n_sub, 1, d), lambda i: (i, 0, 0))
    h_specs = _hidden_specs(src, tm, d, lambda i: (i, 0))
    return pl.pallas_call(
        functools.partial(_mla_down_kernel, lora=kv_lora, n_sub=n_sub, n_h=len(h_specs), split=src.split),
        grid=(m // tm,),
        in_specs=h_specs + [pl.BlockSpec((1, d), const), mod, mod,
                            pl.BlockSpec((d, q_lora), const),
                            pl.BlockSpec(wdkv.shape, const),
                            pl.BlockSpec(sel.shape, const),
                            pl.BlockSpec((1, q_lora), const),
                            pl.BlockSpec((1, kv_lora), const),
                            pl.BlockSpec((tm, LANES), row),
                            pl.BlockSpec((tm, LANES), row)],
        out_specs=[pl.BlockSpec((tm, q_lora), row),
                   pl.BlockSpec((tm, kv_lora), row),
                   pl.BlockSpec((tm, LANES), row)],
        out_shape=[jax.ShapeDtypeStruct((m, q_lora), BF16),
                   jax.ShapeDtypeStruct((m, kv_lora), BF16),
                   jax.ShapeDtypeStruct((m, LANES), BF16)],
        scratch_shapes=[pltpu.VMEM((d, q_lora), BF16), pltpu.VMEM((d, nkv), BF16)],
        compiler_params=_params("arbitrary"),
        name="mla_down",
    )(*_hidden_args(src, h_specs), g.reshape(1, d), shift, scale,
      wdq, wdkv, sel, gq.reshape(1, -1), gkv.reshape(1, -1), cos, sin)


def _wq_layout_kernel(w_ref, sel_ref, o_ref):
    o_ref[...] = jnp.dot(w_ref[...].astype(BF16), sel_ref[...], preferred_element_type=F32).astype(BF16)


def mla_wq_layout(wuq, nh):
    q_lora = wuq.shape[0]
    dh = QK_NOPE + QK_ROPE
    rope = _rope_select()
    one = np.zeros((dh, QK_NOPE + rope.shape[1]), np.float32)
    one[:QK_NOPE, :QK_NOPE] = np.eye(QK_NOPE)
    one[QK_NOPE:, QK_NOPE:] = rope
    pair = np.zeros((2 * one.shape[0], 2 * one.shape[1]), np.float32)
    pair[:one.shape[0], :one.shape[1]] = one
    pair[one.shape[0]:, one.shape[1]:] = one
    assert nh % 2 == 0
    return pl.pallas_call(
        _wq_layout_kernel,
        grid=(nh // 2,),
        in_specs=[pl.BlockSpec((q_lora, pair.shape[0]), lambda h: (0, h)),
                  pl.BlockSpec(pair.shape, lambda h: (0, 0))],
        out_specs=pl.BlockSpec((q_lora, pair.shape[1]), lambda h: (0, h)),
        out_shape=jax.ShapeDtypeStruct((q_lora, nh * one.shape[1]), BF16),
        compiler_params=_params("arbitrary"),
        name="mla_wq_layout",
    )(wuq, jnp.asarray(pair, BF16))


MLA_UP_HEADS = 4


def _mla_up_kernel(cq_ref, ckv_ref, kpe_ref, cos_ref, sin_ref, wq_ref, wkv_ref,
                   q_ref, k_ref, v_ref, *, q_scale):
    dq, dkv = QK_NOPE + LANES, QK_NOPE + V_DIM
    qf = jnp.dot(cq_ref[0], wq_ref[...], preferred_element_type=F32)
    kvf = jnp.dot(ckv_ref[0], wkv_ref[...].astype(BF16), preferred_element_type=F32)
    for hh in range(q_ref.shape[1]):
        q_pe = _rope(qf[:, hh * dq + QK_NOPE:(hh + 1) * dq], cos_ref[...], sin_ref[...])
        q_ref[0, hh, :, :QK_NOPE] = (qf[:, hh * dq:hh * dq + QK_NOPE] * q_scale).astype(BF16)
        q_ref[0, hh, :, QK_NOPE:] = (q_pe * q_scale).astype(BF16)
        k_ref[0, hh, :, :QK_NOPE] = kvf[:, hh * dkv:hh * dkv + QK_NOPE].astype(BF16)
        k_ref[0, hh, :, QK_NOPE:] = kpe_ref[0]
        v_ref[0, hh] = kvf[:, hh * dkv + QK_NOPE:(hh + 1) * dkv].astype(BF16)


def mla_up(cq, ckv, kpe, cos, sin, wq_h, wkv_h, q_scale):
    b, t, q_lora = cq.shape
    kv_lora = ckv.shape[2]
    dqk = QK_NOPE + LANES
    nh = wq_h.shape[1] // dqk
    hb = _tile(nh, MLA_UP_HEADS, 1)
    tok = lambda bi, hi: (bi, 0, 0)
    out = lambda bi, hi: (bi, hi, 0, 0)
    return pl.pallas_call(
        functools.partial(_mla_up_kernel, q_scale=q_scale),
        grid=(b, nh // hb),
        in_specs=[pl.BlockSpec((1, t, q_lora), tok),
                  pl.BlockSpec((1, t, kv_lora), tok),
                  pl.BlockSpec((1, t, LANES), tok),
                  pl.BlockSpec((t, LANES), lambda bi, hi: (0, 0)),
                  pl.BlockSpec((t, LANES), lambda bi, hi: (0, 0)),
                  pl.BlockSpec((q_lora, hb * dqk), lambda bi, hi: (0, hi)),
                  pl.BlockSpec((kv_lora, hb * (QK_NOPE + V_DIM)), lambda bi, hi: (0, hi))],
        out_specs=[pl.BlockSpec((1, hb, t, dqk), out),
                   pl.BlockSpec((1, hb, t, dqk), out),
                   pl.BlockSpec((1, hb, t, V_DIM), out)],
        out_shape=[jax.ShapeDtypeStruct((b, nh, t, dqk), BF16),
                   jax.ShapeDtypeStruct((b, nh, t, dqk), BF16),
                   jax.ShapeDtypeStruct((b, nh, t, V_DIM), BF16)],
        compiler_params=_params("arbitrary", "arbitrary"),
        name="mla_up",
    )(cq, ckv, kpe, cos, sin, wq_h, wkv_h)


ATTN_HEADS = 8
ATTN_TILE = 512
ATTN_ROWS = 256


def _attn_kernel(q_ref, k_ref, v_ref, o_ref, *, n_lat, n_ctx):
    def attend(n_rows, key_rows):
        for hh in range(q_ref.shape[1]):
            k = k_ref[0, hh, key_rows, :]
            v = v_ref[0, hh, key_rows, :]
            v1 = jnp.concatenate([v, jnp.ones_like(v)], axis=-1)
            for r in range(n_rows // ATTN_ROWS):
                rows = slice(r * ATTN_ROWS, (r + 1) * ATTN_ROWS)
                s = lax.dot_general(q_ref[0, hh, rows, :], k, (((1,), (1,)), ((), ())),
                                    preferred_element_type=F32)
                p = jnp.exp2(s - jnp.max(s, axis=-1, keepdims=True))
                o = jnp.dot(p.astype(BF16), v1, preferred_element_type=F32)
                o_ref[0, rows, hh * V_DIM:(hh + 1) * V_DIM] = (o[:, :V_DIM] / o[:, V_DIM:]).astype(BF16)

    @pl.when(pl.program_id(2) == 0)
    def _():
        attend(n_ctx, slice(n_lat, n_lat + n_ctx))

    @pl.when(pl.program_id(2) > 0)
    def _():
        attend(ATTN_TILE, slice(0, n_lat + n_ctx))


def attention(q, k, v, n_lat):
    b, nh, t, dqk = q.shape
    n_ctx = t - n_lat
    assert n_lat % ATTN_TILE == 0 and 0 < n_ctx <= ATTN_TILE and n_ctx % ATTN_ROWS == 0
    hb = _tile(nh, ATTN_HEADS, 1)
    n_tiles = n_lat // ATTN_TILE + 1
    tile = lambda i: (i + n_tiles - 1) % n_tiles
    return pl.pallas_call(
        functools.partial(_attn_kernel, n_lat=n_lat, n_ctx=n_ctx),
        grid=(b, nh // hb, n_tiles),
        in_specs=[pl.BlockSpec((1, hb, ATTN_TILE, dqk), lambda bi, hi, i: (bi, hi, tile(i), 0)),
                  pl.BlockSpec((1, hb, t, dqk), lambda bi, hi, i: (bi, hi, 0, 0)),
                  pl.BlockSpec((1, hb, t, V_DIM), lambda bi, hi, i: (bi, hi, 0, 0))],
        out_specs=pl.BlockSpec((1, ATTN_TILE, hb * V_DIM), lambda bi, hi, i: (bi, tile(i), hi)),
        out_shape=jax.ShapeDtypeStruct((b, t, nh * V_DIM), BF16),
        compiler_params=_params("arbitrary", "arbitrary", "arbitrary"),
        name="attention",
    )(q, k, v)


CONV_TILE = 128
CONV_GROUPS = CONV_TILE // SUBLANES


def _dwconv_kernel(prev_ref, cur_ref, next_ref, dw_ref, dwb_ref, g_ref, b_ref, o_ref, buf, cv, *, width):
    i = pl.program_id(1)
    tt = cur_ref.shape[1]
    d = cur_ref.shape[2]
    pad = width // 2
    first = i == 0
    final = i == pl.num_programs(1) - 1
    for c in range(d // LANES):
        cols = slice(c * LANES, (c + 1) * LANES)
        buf[c, 0:HALO, :] = jnp.where(first, 0.0, prev_ref[0, :, cols])
        buf[c, HALO:HALO + tt, :] = cur_ref[0, :, cols]
        buf[c, HALO + tt:, :] = jnp.where(final, 0.0, next_ref[0, :, cols])
        acc = [jnp.broadcast_to(dwb_ref[:, cols], (SUBLANES, LANES)) for _ in range(CONV_GROUPS)]
        for m in range(-pad, CONV_GROUPS + pad):
            xm = buf[c, pl.ds(HALO + m, SUBLANES, stride=CONV_GROUPS), :]
            for j in range(CONV_GROUPS):
                kk = m - j + pad
                if 0 <= kk < width:
                    acc[j] = acc[j] + xm * dw_ref[kk, :, cols]
        for j in range(CONV_GROUPS):
            cv[c, pl.ds(j, SUBLANES, stride=CONV_GROUPS), :] = acc[j]
    a = jnp.concatenate([cv[c] for c in range(d // LANES)], axis=-1)
    mu = jnp.mean(a, axis=-1, keepdims=True)
    xc = a - mu
    var = jnp.mean(xc * xc, axis=-1, keepdims=True)
    y = xc * lax.rsqrt(var + EPS) * g_ref[...] + b_ref[...]
    o_ref[0] = (y * _sigmoid(y)).astype(o_ref.dtype)


def dwconv_ln_swish(u, dw, dwb, ln_g, ln_b):
    b, t, d = u.shape
    width = dw.shape[0]
    tt = CONV_TILE
    assert t % tt == 0 and width // 2 < HALO and d % LANES == 0
    nh = tt // HALO
    last = t // HALO - 1
    vec = lambda bi, i: (0, 0)
    dw8 = jnp.broadcast_to(dw[:, None, :], (width, SUBLANES, d))
    return pl.pallas_call(
        functools.partial(_dwconv_kernel, width=width),
        grid=(b, t // tt),
        in_specs=[pl.BlockSpec((1, HALO, d), lambda bi, i: (bi, jnp.maximum(i * nh - 1, 0), 0)),
                  pl.BlockSpec((1, tt, d), lambda bi, i: (bi, i, 0)),
                  pl.BlockSpec((1, HALO, d), lambda bi, i: (bi, jnp.minimum((i + 1) * nh, last), 0)),
                  pl.BlockSpec((width, SUBLANES, d), lambda bi, i: (0, 0, 0)),
                  pl.BlockSpec((1, d), vec), pl.BlockSpec((1, d), vec), pl.BlockSpec((1, d), vec)],
        out_specs=pl.BlockSpec((1, tt, d), lambda bi, i: (bi, i, 0)),
        out_shape=jax.ShapeDtypeStruct((b, t, d), BF16),
        scratch_shapes=[pltpu.VMEM((d // LANES, tt + 2 * HALO, LANES), F32),
                        pltpu.VMEM((d // LANES, tt, LANES), F32)],
        compiler_params=_params("arbitrary", "arbitrary"),
        name="dwconv_ln_swish",
    )(u, u, u, dw8, dwb.reshape(1, d), ln_g.reshape(1, d), ln_b.reshape(1, d))


def _rope_tables(seq, ctx_len):
    n_axis = QK_ROPE // 4
    t = jnp.arange(seq, dtype=jnp.int32)
    row = (t // GRID_W).astype(F32)
    col = (t % GRID_W).astype(F32)
    freqs = ROPE_THETA ** (-jnp.arange(n_axis, dtype=F32) / n_axis)
    ang = jnp.concatenate([row[:, None] * freqs, col[:, None] * freqs], axis=-1)
    ang = jnp.concatenate([ang, jnp.zeros((ctx_len, QK_ROPE // 2), F32)], axis=0)
    cos, sin = jnp.cos(ang), jnp.sin(ang)
    zeros = jnp.zeros((seq + ctx_len, LANES - QK_ROPE), F32)
    cos_t = jnp.concatenate([cos, cos, zeros], axis=-1)
    sin_t = jnp.concatenate([-sin, sin, zeros], axis=-1)
    return cos_t, sin_t


def _rope_select():
    half = QK_ROPE // 2
    sel = np.zeros((QK_ROPE, LANES), np.float32)
    for p in range(half):
        sel[2 * p, p] = sel[2 * p, QK_ROPE + p] = 1.0
        sel[2 * p + 1, half + p] = sel[2 * p + 1, QK_ROPE + half + p] = 1.0
    return sel


def _rope_select_pair():
    sel = _rope_select()
    out = np.zeros((QK_ROPE, 2 * LANES), np.float32)
    out[:, :QK_ROPE] = sel[:, :QK_ROPE]
    out[:, LANES:LANES + QK_ROPE] = sel[:, QK_ROPE // 2:QK_ROPE // 2 + QK_ROPE]
    return out


def _rope(g, cos, sin):
    return g * cos + pltpu.roll(g, QK_ROPE // 2, axis=1) * sin


def _mod_tiles(m, layout, k, d):
    b, n_lat, n_ctx = layout
    mk = m[:, k * d:(k + 1) * d]
    parts = [jnp.broadcast_to(mk[:b, None, None, :], (b, n_lat, 1, d))]
    if n_ctx:
        parts.append(jnp.broadcast_to(mk[b][None, None, None, :], (b, n_ctx, 1, d)))
    return jnp.concatenate(parts, axis=1).reshape(b * (n_lat + n_ctx), 1, d)


def kernel(x, c, ctx, c_ctx, ada_w, ada_b, norm_g, ffn_w1, ffn_w3, ffn_w2, mla_wdq, mla_gq, mla_wuq, mla_wdkv, mla_gkv, mla_wukv, mla_wo, conv_w1, conv_b1, conv_dw, conv_dwb, conv_ln_g, conv_ln_b, conv_w2, conv_b2, final_g):
    b, s, d = x.shape
    lc = ctx.shape[1]
    depth = ada_w.shape[0]
    d_ff = ffn_w1.shape[-1]
    nh = mla_wo.shape[1] // V_DIM
    assert s % MOD_TILE == 0 and lc % MOD_TILE == 0 and b < COND_ROWS

    cond = jnp.concatenate([c, c_ctx[None, :], jnp.zeros((COND_ROWS - b - 1, d), F32)], axis=0)
    mods = ada_mods(cond, ada_w, ada_b)

    t_all = s + lc
    rows_all = (b, s // MOD_TILE, lc // MOD_TILE)
    rows_lat = (b, s // MOD_TILE, 0)

    src = Hidden((x, ctx), t_all)
    combined = True

    def ffn(src, m, rows, i, k, f, xn=None):
        if xn is None:
            xn = norm_mod(src, norm_g[i, k], _mod_tiles(m, rows, 3 * k, d), _mod_tiles(m, rows, 3 * k + 1, d))
        g = gated_up(xn, ffn_w1, (i, f), 0, ffn_w3, (i, f), 0, d_ff, "swiglu")
        return _flat(proj_residual(g, ffn_w2, (i, f), src, _mod_tiles(m, rows, 3 * k + 2, d), FFN_RES_WEIGHT))

    def latent_only(src):
        if src.split:
            return Hidden((src.arrays[0],), s)
        return Hidden((src.arrays[0].reshape(b, t_all, d),), s)

    for i in range(depth):
        last = i == depth - 1
        mixer = i % 2
        j = i // 2
        ctx_into_mixer = (not last) or mixer == 0
        ctx_out = not last
        m = mods[i]
        if combined and not ctx_into_mixer:
            src, combined = latent_only(src), False
        rows = rows_all if combined else rows_lat

        src = ffn(src, m, rows, i, 0, 0)

        shift, scale, gate = (_mod_tiles(m, rows, k, d) for k in (3, 4, 5))
        fuse_norm = not (combined and not ctx_out)

        def mixer_out(xo, w, bias=None):
            if not fuse_norm:
                return _flat(proj_residual(xo, w, (j,), src, gate, 1.0, bias=bias)), None
            h_new, xn2 = proj_residual_norm(xo, w, (j,), src.arrays[0].reshape(-1, d), gate, 1.0, norm_g[i, 2],
                                            _mod_tiles(m, rows, 6, d), _mod_tiles(m, rows, 7, d), bias=bias)
            return _flat(h_new), xn2

        if mixer == 0:
            assert combined
            cos_t, sin_t = _rope_tables(s, lc)
            cq, ckv, kpe = mla_down(src, norm_g[i, 1], shift, scale, mla_wdq[j], mla_wdkv[j], mla_gq[j], mla_gkv[j],
                                    jnp.tile(cos_t, (b, 1)), jnp.tile(sin_t, (b, 1)))
            wq_h = mla_wq_layout(mla_wuq[j], nh)
            wkv_h = mla_wukv[j]
            q_scale = (QK_NOPE + QK_ROPE) ** -0.5 * math.log2(math.e)
            q, k, v = mla_up(cq.reshape(b, t_all, -1), ckv.reshape(b, t_all, -1), kpe.reshape(b, t_all, -1),
                             cos_t, sin_t, wq_h, wkv_h, q_scale)
            o = attention(q, k, v, s).reshape(b * t_all, nh * V_DIM)
            src, xn2 = mixer_out(o, mla_wo)
        else:
            assert not combined
            xn = norm_mod(src, norm_g[i, 1], shift, scale)
            u = gated_up(xn, conv_w1, (j,), 0, conv_w1, (j,), 1, d, "glu",
                         bias=(conv_b1[j, :d], conv_b1[j, d:]), out_dtype=F32)
            vv = dwconv_ln_swish(u.reshape(b, s, d), conv_dw[j], conv_dwb[j], conv_ln_g[j], conv_ln_b[j])
            src, xn2 = mixer_out(vv.reshape(-1, d), conv_w2, bias=conv_b2[j])

        if combined and not ctx_out:
            src, combined = latent_only(src), False
            rows = rows_lat
        src = ffn(src, m, rows, i, 2, 1, xn=xn2)

    if combined:
        src = latent_only(src)
    return norm_mod(src, final_g, out_dtype=F32).reshape(b, s, d)
```

```python
import functools
import math
from typing import NamedTuple

import numpy as np
import jax
import jax.numpy as jnp
from jax import lax
from jax.experimental import pallas as pl
from jax.experimental.pallas import tpu as pltpu

F32 = jnp.float32
BF16 = jnp.bfloat16

GRID_W = 64
QK_NOPE = 128
QK_ROPE = 64
V_DIM = 128
ROPE_THETA = 10000.0
FFN_RES_WEIGHT = 0.5
N_MOD = 9
EPS = 1e-6

LANES = 128
SUBLANES = 8
VMEM_LIMIT_BYTES = 58 * 1024 * 1024

MOD_TILE = 256
COND_ROWS = 8
HALO = 16
CAST_ROWS = 256

MXU_TILE = 1024
UP_COLS = 512
DEEP_ROWS = 512
NORM_ROWS = 1024
MLA_DOWN_ROWS = 1024


def _params(*sem):
    return pltpu.CompilerParams(dimension_semantics=sem, vmem_limit_bytes=VMEM_LIMIT_BYTES)


def _tile(n, pref, unit):
    if n <= pref:
        return n
    t = (pref // unit) * unit
    while t > unit and n % t:
        t -= unit
    assert n % t == 0, (n, pref, unit)
    return t


def _cast_weight(src_ref, dst_ref):
    k = src_ref.shape[0]
    rows = _tile(k, CAST_ROWS, SUBLANES)

    def body(c, carry):
        s = pl.multiple_of(c * rows, rows)
        dst_ref[pl.ds(s, rows), :] = src_ref[pl.ds(s, rows), :].astype(BF16)
        return carry

    lax.fori_loop(0, k // rows, body, 0)


def _sigmoid(x):
    return 1.0 / (1.0 + jnp.exp(-x))


def _ada_kernel(c_ref, w_ref, b_ref, o_ref):
    c = c_ref[...]
    s = (c * _sigmoid(c)).astype(BF16)
    o_ref[0] = jnp.dot(s, w_ref[0].astype(BF16), preferred_element_type=F32) + b_ref[0]


def ada_mods(cond, ada_w, ada_b):
    depth, d, n = ada_w.shape
    tn = _tile(n, MXU_TILE, LANES)
    return pl.pallas_call(
        _ada_kernel,
        grid=(depth, n // tn),
        in_specs=[
            pl.BlockSpec((COND_ROWS, d), lambda l, j: (0, 0)),
            pl.BlockSpec((1, d, tn), lambda l, j: (l, 0, j)),
            pl.BlockSpec((1, 1, tn), lambda l, j: (l, 0, j)),
        ],
        out_specs=pl.BlockSpec((1, COND_ROWS, tn), lambda l, j: (l, 0, j)),
        out_shape=jax.ShapeDtypeStruct((depth, COND_ROWS, n), F32),
        compiler_params=_params("arbitrary", "arbitrary"),
        name="ada_mods",
    )(cond, ada_w, ada_b.reshape(depth, 1, n))


class Hidden(NamedTuple):
    arrays: tuple
    t_use: int

    @property
    def split(self):
        if len(self.arrays) == 1:
            return None
        return (self.arrays[0].shape[1] // MOD_TILE, self.arrays[1].shape[1] // MOD_TILE)

    @property
    def shape(self):
        return (self.arrays[0].shape[0] * self.t_use, self.arrays[0].shape[2])


def _flat(h):
    return Hidden((h.reshape(1, *h.shape),), h.shape[0])


def _hidden_tile(src, pref):
    return _tile(src.shape[0] if src.split else src.t_use, pref, MOD_TILE)


def _hidden_specs(src, tm, cols, tile_col):
    if src.split is None:
        nt = src.t_use // tm

        def hmap(*g):
            i, j = tile_col(*g)
            return (i // nt, i % nt, j)

        return [pl.BlockSpec((None, tm, cols), hmap)]
    n_lat, n_ctx = src.split
    n_sub = tm // MOD_TILE
    specs = []
    for r in range(n_sub):
        def xmap(*g, r=r):
            i, j = tile_col(*g)
            t = i * n_sub + r
            return (t // (n_lat + n_ctx), jnp.minimum(t % (n_lat + n_ctx), n_lat - 1), j)

        def cmap(*g, r=r):
            i, j = tile_col(*g)
            t = i * n_sub + r
            return (t // (n_lat + n_ctx), jnp.maximum(t % (n_lat + n_ctx) - n_lat, 0), j)

        specs += [pl.BlockSpec((None, MOD_TILE, cols), xmap), pl.BlockSpec((None, MOD_TILE, cols), cmap)]
    return specs


def _hidden_args(src, specs):
    return list(src.arrays) * (len(specs) // len(src.arrays))


def _hidden_rows(h_refs, r, row_tile, n_sub, split):
    if split is None:
        return h_refs[0][r * MOD_TILE:(r + 1) * MOD_TILE, :]
    n_lat, n_ctx = split
    is_ctx = (row_tile * n_sub + r) % (n_lat + n_ctx) >= n_lat
    return jnp.where(is_ctx, h_refs[2 * r + 1][...], h_refs[2 * r][...])


def _rms(x, g):
    return x * lax.rsqrt(jnp.mean(x * x, axis=-1, keepdims=True) + EPS) * g


def _norm_kernel(*refs, modulate, n_sub, n_h, split):
    h_refs, refs = refs[:n_h], refs[n_h:]
    if modulate:
        g_ref, sh_ref, sc_ref, o_ref = refs
    else:
        g_ref, o_ref = refs
    for r in range(n_sub):
        y = _rms(_hidden_rows(h_refs, r, pl.program_id(0), n_sub, split), g_ref[...])
        if modulate:
            y = y * (1.0 + sc_ref[r]) + sh_ref[r]
        o_ref[r * MOD_TILE:(r + 1) * MOD_TILE, :] = y.astype(o_ref.dtype)


def norm_mod(src, g, shift=None, scale=None, out_dtype=BF16):
    m, d = src.shape
    tm = _hidden_tile(src, NORM_ROWS)
    n_sub = tm // MOD_TILE
    modulate = shift is not None
    h_specs = _hidden_specs(src, tm, d, lambda i: (i, 0))
    in_specs = h_specs + [pl.BlockSpec((1, d), lambda i: (0, 0))]
    args = _hidden_args(src, h_specs) + [g.reshape(1, d)]
    if modulate:
        in_specs += [pl.BlockSpec((n_sub, 1, d), lambda i: (i, 0, 0))] * 2
        args += [shift, scale]
    return pl.pallas_call(
        functools.partial(_norm_kernel, modulate=modulate, n_sub=n_sub, n_h=len(h_specs), split=src.split),
        grid=(m // tm,),
        in_specs=in_specs,
        out_specs=pl.BlockSpec((tm, d), lambda i: (i, 0)),
        out_shape=jax.ShapeDtypeStruct((m, d), out_dtype),
        compiler_params=_params("arbitrary"),
        name="norm_mod",
    )(*args)


def _up_kernel(*refs, act, has_bias):
    if has_bias:
        x_ref, wa_ref, wb_ref, ba_ref, bb_ref, o_ref, wa_s, wb_s = refs
    else:
        x_ref, wa_ref, wb_ref, o_ref, wa_s, wb_s = refs

    @pl.when(pl.program_id(1) == 0)
    def _():
        _cast_weight(wa_ref, wa_s)
        _cast_weight(wb_ref, wb_s)

    x = x_ref[...]
    a = jnp.dot(x, wa_s[...], preferred_element_type=F32)
    b = jnp.dot(x, wb_s[...], preferred_element_type=F32)
    if has_bias:
        a = a + ba_ref[...]
        b = b + bb_ref[...]
    if act == "swiglu":
        o = (a * _sigmoid(a)) * b
    else:
        o = a * _sigmoid(b)
    o_ref[...] = o.astype(o_ref.dtype)


def gated_up(x, wa, wa_idx, wa_col, wb, wb_idx, wb_col, n, act, bias=None, out_dtype=BF16):
    m, k = x.shape
    tm = _tile(m, MXU_TILE, MOD_TILE)
    tn = _tile(n, UP_COLS, LANES)
    nj = n // tn

    def wspec(idx, col):
        lead = (None,) * len(idx)
        return pl.BlockSpec(lead + (k, tn), lambda j, i: idx + (0, col * nj + j))

    in_specs = [pl.BlockSpec((tm, k), lambda j, i: (i, 0)), wspec(wa_idx, wa_col), wspec(wb_idx, wb_col)]
    args = [x, wa, wb]
    if bias is not None:
        ba, bb = bias
        in_specs += [pl.BlockSpec((1, tn), lambda j, i: (0, j))] * 2
        args += [ba.reshape(1, n), bb.reshape(1, n)]
    return pl.pallas_call(
        functools.partial(_up_kernel, act=act, has_bias=bias is not None),
        grid=(nj, m // tm),
        in_specs=in_specs,
        out_specs=pl.BlockSpec((tm, tn), lambda j, i: (i, j)),
        out_shape=jax.ShapeDtypeStruct((m, n), out_dtype),
        scratch_shapes=[pltpu.VMEM((k, tn), BF16), pltpu.VMEM((k, tn), BF16)],
        compiler_params=_params("arbitrary", "arbitrary"),
        name="gated_up_" + act,
    )(*args)


def _res_kernel(*refs, coef, n_sub, has_bias, n_h, split):
    x_ref, w_ref = refs[:2]
    refs = refs[2:]
    if has_bias:
        b_ref, refs = refs[0], refs[1:]
    h_refs, (gate_ref, o_ref, w_s) = refs[:n_h], refs[n_h:]

    @pl.when(pl.program_id(1) == 0)
    def _():
        _cast_weight(w_ref, w_s)

    y = jnp.dot(x_ref[...], w_s[...], preferred_element_type=F32)
    if has_bias:
        y = y + b_ref[...]
    for r in range(n_sub):
        rows = slice(r * MOD_TILE, (r + 1) * MOD_TILE)
        h = _hidden_rows(h_refs, r, pl.program_id(1), n_sub, split)
        o_ref[rows, :] = h + (coef * gate_ref[r]) * y[rows, :]


def proj_residual(x, w, w_idx, src, gate, coef, bias=None):
    m, k = x.shape
    n = src.shape[1]
    assert src.shape[0] == m
    tn = _tile(n, MXU_TILE, LANES)
    if k * tn * 14 > VMEM_LIMIT_BYTES // 2:
        assert bias is None
        return _proj_residual_deep(x, w, w_idx, src, gate, coef, tn)
    tm = _hidden_tile(src, MXU_TILE)
    n_sub = tm // MOD_TILE
    lead = (None,) * len(w_idx)
    in_specs = [pl.BlockSpec((tm, k), lambda j, i: (i, 0)),
                pl.BlockSpec(lead + (k, tn), lambda j, i: w_idx + (0, j))]
    args = [x, w]
    if bias is not None:
        in_specs.append(pl.BlockSpec((1, tn), lambda j, i: (0, j)))
        args.append(bias.reshape(1, n))
    h_specs = _hidden_specs(src, tm, tn, lambda j, i: (i, j))
    in_specs += h_specs + [pl.BlockSpec((n_sub, 1, tn), lambda j, i: (i, 0, j))]
    args += _hidden_args(src, h_specs) + [gate]
    return pl.pallas_call(
        functools.partial(_res_kernel, coef=coef, n_sub=n_sub, has_bias=bias is not None,
                          n_h=len(h_specs), split=src.split),
        grid=(n // tn, m // tm),
        in_specs=in_specs,
        out_specs=pl.BlockSpec((tm, tn), lambda j, i: (i, j)),
        out_shape=jax.ShapeDtypeStruct((m, n), F32),
        scratch_shapes=[pltpu.VMEM((k, tn), BF16)],
        compiler_params=_params("arbitrary", "arbitrary"),
        name="proj_residual",
    )(*args)


def _res_deep_kernel(*refs, coef, n_sub, n_h, split, w_idx, tn):
    x_ref, w_hbm = refs[:2]
    h_refs, (gate_ref, o_ref, w_s, w_f32, sem) = refs[2:2 + n_h], refs[2 + n_h:]
    j, i = pl.program_id(0), pl.program_id(1)

    def fetch(col):
        cols = pl.ds(pl.multiple_of(col * tn, tn), tn)
        return pltpu.make_async_copy(w_hbm.at[w_idx + (slice(None), cols)], w_f32, sem)

    @pl.when(jnp.logical_and(j == 0, i == 0))
    def _():
        fetch(0).start()

    @pl.when(i == 0)
    def _():
        fetch(j).wait()
        _cast_weight(w_f32, w_s)

        @pl.when(j + 1 < pl.num_programs(0))
        def _():
            fetch(j + 1).start()

    y = jnp.dot(x_ref[...], w_s[...], preferred_element_type=F32)
    for r in range(n_sub):
        rows = slice(r * MOD_TILE, (r + 1) * MOD_TILE)
        h = _hidden_rows(h_refs, r, i, n_sub, split)
        o_ref[rows, :] = h + (coef * gate_ref[r]) * y[rows, :]


def _proj_residual_deep(x, w, w_idx, src, gate, coef, tn):
    m, k = x.shape
    n = src.shape[1]
    tm = _hidden_tile(src, DEEP_ROWS)
    n_sub = tm // MOD_TILE
    h_specs = _hidden_specs(src, tm, tn, lambda j, i: (i, j))
    in_specs = [pl.BlockSpec((tm, k), lambda j, i: (i, 0)), pl.BlockSpec(memory_space=pl.ANY)]
    in_specs += h_specs + [pl.BlockSpec((n_sub, 1, tn), lambda j, i: (i, 0, j))]
    return pl.pallas_call(
        functools.partial(_res_deep_kernel, coef=coef, n_sub=n_sub, n_h=len(h_specs), split=src.split,
                          w_idx=tuple(w_idx), tn=tn),
        grid=(n // tn, m // tm),
        in_specs=in_specs,
        out_specs=pl.BlockSpec((tm, tn), lambda j, i: (i, j)),
        out_shape=jax.ShapeDtypeStruct((m, n), F32),
        scratch_shapes=[pltpu.VMEM((k, tn), BF16), pltpu.VMEM((k, tn), F32), pltpu.SemaphoreType.DMA(())],
        compiler_params=_params("arbitrary", "arbitrary"),
        name="proj_residual_deep",
    )(x, w, *_hidden_args(src, h_specs), gate)


def _res_norm_kernel(*refs, coef, n_sub, has_bias):
    x_ref, w_ref = refs[:2]
    refs = refs[2:]
    if has_bias:
        b_ref, refs = refs[0], refs[1:]
    h_ref, gate_ref, g_ref, sh_ref, sc_ref, o_ref, xn_ref, w_s = refs

    @pl.when(pl.program_id(0) == 0)
    def _():
        _cast_weight(w_ref, w_s)

    for r in range(n_sub):
        rows = slice(r * MOD_TILE, (r + 1) * MOD_TILE)
        y = jnp.dot(x_ref[rows, :], w_s[...], preferred_element_type=F32)
        if has_bias:
            y = y + b_ref[...]
        h = h_ref[rows, :] + (coef * gate_ref[r]) * y
        o_ref[rows, :] = h
        xn_ref[rows, :] = (_rms(h, g_ref[...]) * (1.0 + sc_ref[r]) + sh_ref[r]).astype(BF16)


def proj_residual_norm(x, w, w_idx, h, gate, coef, g, shift, scale, bias=None):
    m, k = x.shape
    n = h.shape[1]
    tm = _tile(m, DEEP_ROWS, MOD_TILE)
    n_sub = tm // MOD_TILE
    lead = (None,) * len(w_idx)
    row = lambda i: (i, 0)
    vec = pl.BlockSpec((1, n), lambda i: (0, 0))
    mod = pl.BlockSpec((n_sub, 1, n), lambda i: (i, 0, 0))
    in_specs = [pl.BlockSpec((tm, k), row),
                pl.BlockSpec(lead + (k, n), lambda i: w_idx + (0, 0), pipeline_mode=pl.Buffered(1))]
    args = [x, w]
    if bias is not None:
        in_specs.append(vec)
        args.append(bias.reshape(1, n))
    in_specs += [pl.BlockSpec((tm, n), row), mod, vec, mod, mod]
    args += [h, gate, g.reshape(1, n), shift, scale]
    return pl.pallas_call(
        functools.partial(_res_norm_kernel, coef=coef, n_sub=n_sub, has_bias=bias is not None),
        grid=(m // tm,),
        in_specs=in_specs,
        out_specs=[pl.BlockSpec((tm, n), row), pl.BlockSpec((tm, n), row)],
        out_shape=[jax.ShapeDtypeStruct((m, n), F32), jax.ShapeDtypeStruct((m, n), BF16)],
        scratch_shapes=[pltpu.VMEM((k, n), BF16)],
        compiler_params=_params("arbitrary"),
        name="proj_residual_norm",
    )(*args)


def _mla_down_kernel(*refs, lora, n_sub, n_h, split):
    h_refs, refs = refs[:n_h], refs[n_h:]
    (g_ref, sh_ref, sc_ref, wq_ref, wkv_ref, sel_ref, gq_ref, gkv_ref, cos_ref, sin_ref,
     cq_ref, ckv_ref, kpe_ref, wq_s, wkv_s) = refs

    @pl.when(pl.program_id(0) == 0)
    def _():
        _cast_weight(wq_ref, wq_s)
        rows = _tile(wkv_ref.shape[0], CAST_ROWS, SUBLANES)

        def body(c, carry):
            r = pl.ds(pl.multiple_of(c * rows, rows), rows)
            w = wkv_ref[r, :].astype(BF16)
            wkv_s[r, :lora] = w[:, :lora]
            wkv_s[r, lora:] = jnp.dot(w[:, lora:], sel_ref[...], preferred_element_type=F32).astype(BF16)
            return carry

        lax.fori_loop(0, wkv_ref.shape[0] // rows, body, 0)

    for r in range(n_sub):
        rows = slice(r * MOD_TILE, (r + 1) * MOD_TILE)
        h = _hidden_rows(h_refs, r, pl.program_id(0), n_sub, split)
        x = (_rms(h, g_ref[...]) * (1.0 + sc_ref[r]) + sh_ref[r]).astype(BF16)
        cq = jnp.dot(x, wq_s[...], preferred_element_type=F32)
        cq_ref[rows, :] = _rms(cq, gq_ref[...]).astype(BF16)
        kv = jnp.dot(x, wkv_s[...], preferred_element_type=F32)
        ckv_ref[rows, :] = _rms(kv[:, :lora], gkv_ref[...]).astype(BF16)
        kpe = kv[:, lora:lora + LANES] * cos_ref[rows, :] + kv[:, lora + LANES:] * sin_ref[rows, :]
        kpe_ref[rows, :] = kpe.astype(BF16)


def mla_down(src, g, shift, scale, wdq, wdkv, gq, gkv, cos, sin):
    m, d = src.shape
    q_lora = wdq.shape[1]
    kv_lora = gkv.shape[0]
    sel = jnp.asarray(_rope_select_pair(), BF16)
    nkv = kv_lora + sel.shape[1]
    tm = _hidden_tile(src, MLA_DOWN_ROWS)
    n_sub = tm // MOD_TILE
    const = lambda i: (0, 0)
    row = lambda i: (i, 0)
    mod = pl.BlockSpec((n_sub, 1, d), lambda i: (i, 0, 0))
    h_specs = _hidden_specs(src, tm, d, lambda i: (i, 0))
    return pl.pallas_call(
        functools.partial(_mla_down_kernel, lora=kv_lora, n_sub=n_sub, n_h=len(h_specs), split=src.split),
        grid=(m // tm,),
        in_specs=h_specs + [pl.BlockSpec((1, d), const), mod, mod,
                            pl.BlockSpec((d, q_lora), const),
                            pl.BlockSpec(wdkv.shape, const),
                            pl.BlockSpec(sel.shape, const),
                            pl.BlockSpec((1, q_lora), const),
                            pl.BlockSpec((1, kv_lora), const),
                            pl.BlockSpec((tm, LANES), row),
                            pl.BlockSpec((tm, LANES), row)],
        out_specs=[pl.BlockSpec((tm, q_lora), row),
                   pl.BlockSpec((tm, kv_lora), row),
                   pl.BlockSpec((tm, LANES), row)],
        out_shape=[jax.ShapeDtypeStruct((m, q_lora), BF16),
                   jax.ShapeDtypeStruct((m, kv_lora), BF16),
                   jax.ShapeDtypeStruct((m, LANES), BF16)],
        scratch_shapes=[pltpu.VMEM((d, q_lora), BF16), pltpu.VMEM((d, nkv), BF16)],
        compiler_params=_params("arbitrary"),
        name="mla_down",
    )(*_hidden_args(src, h_specs), g.reshape(1, d), shift, scale,
      wdq, wdkv, sel, gq.reshape(1, -1), gkv.reshape(1, -1), cos, sin)


def _wq_layout_kernel(w_ref, sel_ref, o_ref):
    o_ref[...] = jnp.dot(w_ref[...].astype(BF16), sel_ref[...], preferred_element_type=F32).astype(BF16)


def mla_wq_layout(wuq, nh):
    q_lora = wuq.shape[0]
    dh = QK_NOPE + QK_ROPE
    rope = _rope_select()
    one = np.zeros((dh, QK_NOPE + rope.shape[1]), np.float32)
    one[:QK_NOPE, :QK_NOPE] = np.eye(QK_NOPE)
    one[QK_NOPE:, QK_NOPE:] = rope
    pair = np.zeros((2 * one.shape[0], 2 * one.shape[1]), np.float32)
    pair[:one.shape[0], :one.shape[1]] = one
    pair[one.shape[0]:, one.shape[1]:] = one
    assert nh % 2 == 0
    return pl.pallas_call(
        _wq_layout_kernel,
        grid=(nh // 2,),
        in_specs=[pl.BlockSpec((q_lora, pair.shape[0]), lambda h: (0, h)),
                  pl.BlockSpec(pair.shape, lambda h: (0, 0))],
        out_specs=pl.BlockSpec((q_lora, pair.shape[1]), lambda h: (0, h)),
        out_shape=jax.ShapeDtypeStruct((q_lora, nh * one.shape[1]), BF16),
        compiler_params=_params("arbitrary"),
        name="mla_wq_layout",
    )(wuq, jnp.asarray(pair, BF16))


MLA_UP_HEADS = 4


def _mla_up_kernel(cq_ref, ckv_ref, kpe_ref, cos_ref, sin_ref, wq_ref, wkv_ref,
                   q_ref, k_ref, v_ref, *, q_scale):
    dq, dkv = QK_NOPE + LANES, QK_NOPE + V_DIM
    qf = jnp.dot(cq_ref[0], wq_ref[...], preferred_element_type=F32)
    kvf = jnp.dot(ckv_ref[0], wkv_ref[...].astype(BF16), preferred_element_type=F32)
    for hh in range(q_ref.shape[1]):
        q_pe = _rope(qf[:, hh * dq + QK_NOPE:(hh + 1) * dq], cos_ref[...], sin_ref[...])
        q_ref[0, hh, :, :QK_NOPE] = (qf[:, hh * dq:hh * dq + QK_NOPE] * q_scale).astype(BF16)
        q_ref[0, hh, :, QK_NOPE:] = (q_pe * q_scale).astype(BF16)
        k_ref[0, hh, :, :QK_NOPE] = kvf[:, hh * dkv:hh * dkv + QK_NOPE].astype(BF16)
        k_ref[0, hh, :, QK_NOPE:] = kpe_ref[0]
        v_ref[0, hh] = kvf[:, hh * dkv + QK_NOPE:(hh + 1) * dkv].astype(BF16)


def mla_up(cq, ckv, kpe, cos, sin, wq_h, wkv_h, q_scale):
    b, t, q_lora = cq.shape
    kv_lora = ckv.shape[2]
    dqk = QK_NOPE + LANES
    nh = wq_h.shape[1] // dqk
    hb = _tile(nh, MLA_UP_HEADS, 1)
    tok = lambda bi, hi: (bi, 0, 0)
    out = lambda bi, hi: (bi, hi, 0, 0)
    return pl.pallas_call(
        functools.partial(_mla_up_kernel, q_scale=q_scale),
        grid=(b, nh // hb),
        in_specs=[pl.BlockSpec((1, t, q_lora), tok),
                  pl.BlockSpec((1, t, kv_lora), tok),
                  pl.BlockSpec((1, t, LANES), tok),
                  pl.BlockSpec((t, LANES), lambda bi, hi: (0, 0)),
                  pl.BlockSpec((t, LANES), lambda bi, hi: (0, 0)),
                  pl.BlockSpec((q_lora, hb * dqk), lambda bi, hi: (0, hi)),
                  pl.BlockSpec((kv_lora, hb * (QK_NOPE + V_DIM)), lambda bi, hi: (0, hi))],
        out_specs=[pl.BlockSpec((1, hb, t, dqk), out),
                   pl.BlockSpec((1, hb, t, dqk), out),
                   pl.BlockSpec((1, hb, t, V_DIM), out)],
        out_shape=[jax.ShapeDtypeStruct((b, nh, t, dqk), BF16),
                   jax.ShapeDtypeStruct((b, nh, t, dqk), BF16),
                   jax.ShapeDtypeStruct((b, nh, t, V_DIM), BF16)],
        compiler_params=_params("arbitrary", "arbitrary"),
        name="mla_up",
    )(cq, ckv, kpe, cos, sin, wq_h, wkv_h)


ATTN_HEADS = 8
ATTN_TILE = 512
ATTN_ROWS = 256


def _attn_kernel(q_ref, k_ref, v_ref, o_ref, *, n_lat, n_ctx):
    def attend(n_rows, key_rows):
        for hh in range(q_ref.shape[1]):
            k = k_ref[0, hh, key_rows, :]
            v = v_ref[0, hh, key_rows, :]
            v1 = jnp.concatenate([v, jnp.ones_like(v)], axis=-1)
            for r in range(n_rows // ATTN_ROWS):
                rows = slice(r * ATTN_ROWS, (r + 1) * ATTN_ROWS)
                s = lax.dot_general(q_ref[0, hh, rows, :], k, (((1,), (1,)), ((), ())),
                                    preferred_element_type=F32)
                p = jnp.exp2(s - jnp.max(s, axis=-1, keepdims=True))
                o = jnp.dot(p.astype(BF16), v1, preferred_element_type=F32)
                o_ref[0, rows, hh * V_DIM:(hh + 1) * V_DIM] = (o[:, :V_DIM] / o[:, V_DIM:]).astype(BF16)

    @pl.when(pl.program_id(2) == 0)
    def _():
        attend(n_ctx, slice(n_lat, n_lat + n_ctx))

    @pl.when(pl.program_id(2) > 0)
    def _():
        attend(ATTN_TILE, slice(0, n_lat + n_ctx))


def attention(q, k, v, n_lat):
    b, nh, t, dqk = q.shape
    n_ctx = t - n_lat
    assert n_lat % ATTN_TILE == 0 and 0 < n_ctx <= ATTN_TILE and n_ctx % ATTN_ROWS == 0
    hb = _tile(nh, ATTN_HEADS, 1)
    n_tiles = n_lat // ATTN_TILE + 1
    tile = lambda i: (i + n_tiles - 1) % n_tiles
    return pl.pallas_call(
        functools.partial(_attn_kernel, n_lat=n_lat, n_ctx=n_ctx),
        grid=(b, nh // hb, n_tiles),
        in_specs=[pl.BlockSpec((1, hb, ATTN_TILE, dqk), lambda bi, hi, i: (bi, hi, tile(i), 0)),
                  pl.BlockSpec((1, hb, t, dqk), lambda bi, hi, i: (bi, hi, 0, 0)),
                  pl.BlockSpec((1, hb, t, V_DIM), lambda bi, hi, i: (bi, hi, 0, 0))],
        out_specs=pl.BlockSpec((1, ATTN_TILE, hb * V_DIM), lambda bi, hi, i: (bi, tile(i), hi)),
        out_shape=jax.ShapeDtypeStruct((b, t, nh * V_DIM), BF16),
        compiler_params=_params("arbitrary", "arbitrary", "arbitrary"),
        name="attention",
    )(q, k, v)


CONV_TILE = 128
CONV_GROUPS = CONV_TILE // SUBLANES


def _dwconv_kernel(prev_ref, cur_ref, next_ref, dw_ref, dwb_ref, g_ref, b_ref, o_ref, buf, cv, *, width):
    i = pl.program_id(1)
    tt = cur_ref.shape[1]
    d = cur_ref.shape[2]
    pad = width // 2
    first = i == 0
    final = i == pl.num_programs(1) - 1
    for c in range(d // LANES):
        cols = slice(c * LANES, (c + 1) * LANES)
        buf[c, 0:HALO, :] = jnp.where(first, 0.0, prev_ref[0, :, cols])
        buf[c, HALO:HALO + tt, :] = cur_ref[0, :, cols]
        buf[c, HALO + tt:, :] = jnp.where(final, 0.0, next_ref[0, :, cols])
        acc = [jnp.broadcast_to(dwb_ref[:, cols], (SUBLANES, LANES)) for _ in range(CONV_GROUPS)]
        for m in range(-pad, CONV_GROUPS + pad):
            xm = buf[c, pl.ds(HALO + m, SUBLANES, stride=CONV_GROUPS), :]
            for j in range(CONV_GROUPS):
                kk = m - j + pad
                if 0 <= kk < width:
                    acc[j] = acc[j] + xm * dw_ref[kk, :, cols]
        for j in range(CONV_GROUPS):
            cv[c, pl.ds(j, SUBLANES, stride=CONV_GROUPS), :] = acc[j]
    a = jnp.concatenate([cv[c] for c in range(d // LANES)], axis=-1)
    mu = jnp.mean(a, axis=-1, keepdims=True)
    xc = a - mu
    var = jnp.mean(xc * xc, axis=-1, keepdims=True)
    y = xc * lax.rsqrt(var + EPS) * g_ref[...] + b_ref[...]
    o_ref[0] = (y * _sigmoid(y)).astype(o_ref.dtype)


def dwconv_ln_swish(u, dw, dwb, ln_g, ln_b):
    b, t, d = u.shape
    width = dw.shape[0]
    tt = CONV_TILE
    assert t % tt == 0 and width // 2 < HALO and d % LANES == 0
    nh = tt // HALO
    last = t // HALO - 1
    vec = lambda bi, i: (0, 0)
    dw8 = jnp.broadcast_to(dw[:, None, :], (width, SUBLANES, d))
    return pl.pallas_call(
        functools.partial(_dwconv_kernel, width=width),
        grid=(b, t // tt),
        in_specs=[pl.BlockSpec((1, HALO, d), lambda bi, i: (bi, jnp.maximum(i * nh - 1, 0), 0)),
                  pl.BlockSpec((1, tt, d), lambda bi, i: (bi, i, 0)),
                  pl.BlockSpec((1, HALO, d), lambda bi, i: (bi, jnp.minimum((i + 1) * nh, last), 0)),
                  pl.BlockSpec((width, SUBLANES, d), lambda bi, i: (0, 0, 0)),
                  pl.BlockSpec((1, d), vec), pl.BlockSpec((1, d), vec), pl.BlockSpec((1, d), vec)],
        out_specs=pl.BlockSpec((1, tt, d), lambda bi, i: (bi, i, 0)),
        out_shape=jax.ShapeDtypeStruct((b, t, d), BF16),
        scratch_shapes=[pltpu.VMEM((d // LANES, tt + 2 * HALO, LANES), F32),
                        pltpu.VMEM((d // LANES, tt, LANES), F32)],
        compiler_params=_params("arbitrary", "arbitrary"),
        name="dwconv_ln_swish",
    )(u, u, u, dw8, dwb.reshape(1, d), ln_g.reshape(1, d), ln_b.reshape(1, d))


def _rope_tables(seq, ctx_len):
    n_axis = QK_ROPE // 4
    t = jnp.arange(seq, dtype=jnp.int32)
    row = (t // GRID_W).astype(F32)
    col = (t % GRID_W).astype(F32)
    freqs = ROPE_THETA ** (-jnp.arange(n_axis, dtype=F32) / n_axis)
    ang = jnp.concatenate([row[:, None] * freqs, col[:, None] * freqs], axis=-1)
    ang = jnp.concatenate([ang, jnp.zeros((ctx_len, QK_ROPE // 2), F32)], axis=0)
    cos, sin = jnp.cos(ang), jnp.sin(ang)
    zeros = jnp.zeros((seq + ctx_len, LANES - QK_ROPE), F32)
    cos_t = jnp.concatenate([cos, cos, zeros], axis=-1)
    sin_t = jnp.concatenate([-sin, sin, zeros], axis=-1)
    return cos_t, sin_t


def _rope_select():
    half = QK_ROPE // 2
    sel = np.zeros((QK_ROPE, LANES), np.float32)
    for p in range(half):
        sel[2 * p, p] = sel[2 * p, QK_ROPE + p] = 1.0
        sel[2 * p + 1, half + p] = sel[2 * p + 1, QK_ROPE + half + p] = 1.0
    return sel


def _rope_select_pair():
    sel = _rope_select()
    out = np.zeros((QK_ROPE, 2 * LANES), np.float32)
    out[:, :QK_ROPE] = sel[:, :QK_ROPE]
    out[:, LANES:LANES + QK_ROPE] = sel[:, QK_ROPE // 2:QK_ROPE // 2 + QK_ROPE]
    return out


def _rope(g, cos, sin):
    return g * cos + pltpu.roll(g, QK_ROPE // 2, axis=1) * sin


def _mod_tiles(m, layout, k, d):
    b, n_lat, n_ctx = layout
    mk = m[:, k * d:(k + 1) * d]
    parts = [jnp.broadcast_to(mk[:b, None, None, :], (b, n_lat, 1, d))]
    if n_ctx:
        parts.append(jnp.broadcast_to(mk[b][None, None, None, :], (b, n_ctx, 1, d)))
    return jnp.concatenate(parts, axis=1).reshape(b * (n_lat + n_ctx), 1, d)


def kernel(x, c, ctx, c_ctx, ada_w, ada_b, norm_g, ffn_w1, ffn_w3, ffn_w2, mla_wdq, mla_gq, mla_wuq, mla_wdkv, mla_gkv, mla_wukv, mla_wo, conv_w1, conv_b1, conv_dw, conv_dwb, conv_ln_g, conv_ln_b, conv_w2, conv_b2, final_g):
    b, s, d = x.shape
    lc = ctx.shape[1]
    depth = ada_w.shape[0]
    d_ff = ffn_w1.shape[-1]
    nh = mla_wo.shape[1] // V_DIM
    assert s % MOD_TILE == 0 and lc % MOD_TILE == 0 and b < COND_ROWS

    cond = jnp.concatenate([c, c_ctx[None, :], jnp.zeros((COND_ROWS - b - 1, d), F32)], axis=0)
    mods = ada_mods(cond, ada_w, ada_b)

    t_all = s + lc
    rows_all = (b, s // MOD_TILE, lc // MOD_TILE)
    rows_lat = (b, s // MOD_TILE, 0)

    src = Hidden((x, ctx), t_all)
    combined = True

    def ffn(src, m, rows, i, k, f, xn=None):
        if xn is None:
            xn = norm_mod(src, norm_g[i, k], _mod_tiles(m, rows, 3 * k, d), _mod_tiles(m, rows, 3 * k + 1, d))
        g = gated_up(xn, ffn_w1, (i, f), 0, ffn_w3, (i, f), 0, d_ff, "swiglu")
        return _flat(proj_residual(g, ffn_w2, (i, f), src, _mod_tiles(m, rows, 3 * k + 2, d), FFN_RES_WEIGHT))

    def latent_only(src):
        if src.split:
            return Hidden((src.arrays[0],), s)
        return Hidden((src.arrays[0].reshape(b, t_all, d),), s)

    for i in range(depth):
        last = i == depth - 1
        mixer = i % 2
        j = i // 2
        ctx_into_mixer = (not last) or mixer == 0
        ctx_out = not last
        m = mods[i]
        if combined and not ctx_into_mixer:
            src, combined = latent_only(src), False
        rows = rows_all if combined else rows_lat

        src = ffn(src, m, rows, i, 0, 0)

        shift, scale, gate = (_mod_tiles(m, rows, k, d) for k in (3, 4, 5))
        fuse_norm = not (combined and not ctx_out)

        def mixer_out(xo, w, bias=None):
            if not fuse_norm:
                return _flat(proj_residual(xo, w, (j,), src, gate, 1.0, bias=bias)), None
            h_new, xn2 = proj_residual_norm(xo, w, (j,), src.arrays[0].reshape(-1, d), gate, 1.0, norm_g[i, 2],
                                            _mod_tiles(m, rows, 6, d), _mod_tiles(m, rows, 7, d), bias=bias)
            return _flat(h_new), xn2

        if mixer == 0:
            assert combined
            cos_t, sin_t = _rope_tables(s, lc)
            cq, ckv, kpe = mla_down(src, norm_g[i, 1], shift, scale, mla_wdq[j], mla_wdkv[j], mla_gq[j], mla_gkv[j],
                                    jnp.tile(cos_t, (b, 1)), jnp.tile(sin_t, (b, 1)))
            wq_h = mla_wq_layout(mla_wuq[j], nh)
            wkv_h = mla_wukv[j]
            q_scale = (QK_NOPE + QK_ROPE) ** -0.5 * math.log2(math.e)
            q, k, v = mla_up(cq.reshape(b, t_all, -1), ckv.reshape(b, t_all, -1), kpe.reshape(b, t_all, -1),
                             cos_t, sin_t, wq_h, wkv_h, q_scale)
            o = attention(q, k, v, s).reshape(b * t_all, nh * V_DIM)
            src, xn2 = mixer_out(o, mla_wo)
        else:
            assert not combined
            xn = norm_mod(src, norm_g[i, 1], shift, scale)
            u = gated_up(xn, conv_w1, (j,), 0, conv_w1, (j,), 1, d, "glu",
                         bias=(conv_b1[j, :d], conv_b1[j, d:]), out_dtype=F32)
            vv = dwconv_ln_swish(u.reshape(b, s, d), conv_dw[j], conv_dwb[j], conv_ln_g[j], conv_ln_b[j])
            src, xn2 = mixer_out(vv.reshape(-1, d), conv_w2, bias=conv_b2[j])

        if combined and not ctx_out:
            src, combined = latent_only(src), False
            rows = rows_lat
        src = ffn(src, m, rows, i, 2, 1, xn=xn2)

    if combined:
        src = latent_only(src)
    return norm_mod(src, final_g, out_dtype=F32).reshape(b, s, d)
```
